```python
import jax, jax.numpy as jnp
from jax import lax
import numpy as np

D_MODEL = 2048
BATCH = 2
SEQ = 16384
DEPTH = 2

N_A_LAYERS = DEPTH // 2
N_B_LAYERS = DEPTH - N_A_LAYERS
D_RNN = D_MODEL
N_RNN_BLOCKS = 8
RNN_BLOCK = D_RNN // N_RNN_BLOCKS
CONV_WIDTH = 4
LRU_C = 8.0
N_HEADS = 16
HEAD_DIM = D_MODEL // N_HEADS
Q_BLOCK = 128
FORGET_BIAS_MEAN = 4.0
N_GROUPS = 8
EXPERTS_PER_GROUP = 8
N_EXPERTS = N_GROUPS * EXPERTS_PER_GROUP
TOP_K = 2
D_EXPERT = 512
ROW_BLOCK = 128
EPS = 1e-6

kernel_name = 'hybrid_rglru_fox_hmoe'


def rms_norm(x, gain):
    x32 = x.astype(jnp.float32)
    y = x32 * lax.rsqrt(jnp.mean(x32 * x32, axis=-1, keepdims=True) + EPS)
    return (y * gain.astype(jnp.float32)).astype(x.dtype)


def rglru_mixer(h, w_in, conv_w, conv_b, w_rec, b_rec, w_inp, b_inp, lam, w_out):
    B, S, _ = h.shape
    proj = h @ w_in
    u, y = proj[..., :D_RNN], proj[..., D_RNN:]
    u = lax.conv_general_dilated(u, conv_w[:, None, :], window_strides=(1,),
                                 padding=[(CONV_WIDTH - 1, 0)],
                                 dimension_numbers=('NWC', 'WIO', 'NWC'),
                                 feature_group_count=D_RNN) + conv_b
    ub = u.reshape(B, S, N_RNN_BLOCKS, RNN_BLOCK)
    r = jax.nn.sigmoid(jnp.einsum('bsnc,ncd->bsnd', ub, w_rec).reshape(B, S, D_RNN) + b_rec)
    i = jax.nn.sigmoid(jnp.einsum('bsnc,ncd->bsnd', ub, w_inp).reshape(B, S, D_RNN) + b_inp)
    log_a = (-LRU_C * r.astype(jnp.float32)) * jax.nn.softplus(-lam.astype(jnp.float32))
    a = jnp.exp(log_a)
    b = jnp.sqrt(-jnp.expm1(2.0 * log_a)) * (i * u).astype(jnp.float32)

    def combine(left, right):
        a1, b1 = left
        a2, b2 = right
        return a1 * a2, a2 * b1 + b2

    _, hs = lax.associative_scan(combine, (a, b), axis=1)
    out = hs.astype(h.dtype) * jax.nn.gelu(y)
    return out @ w_out


def shared_kv(x, kv_norm, kv_w, kv_b_forget):
    B, S, _ = x.shape
    HD = N_HEADS * HEAD_DIM
    h = rms_norm(x, kv_norm)
    p = h @ kv_w
    k = p[..., :HD].reshape(B, S, N_HEADS, HEAD_DIM).transpose(0, 2, 1, 3)
    v = p[..., HD:2 * HD].reshape(B, S, N_HEADS, HEAD_DIM).transpose(0, 2, 1, 3)
    f_logit = p[..., 2 * HD:].astype(jnp.float32) + kv_b_forget.astype(jnp.float32)
    cum = jnp.cumsum(jax.nn.log_sigmoid(f_logit), axis=1).transpose(0, 2, 1)
    return k, v, cum


def fox_mixer(h, w_qg, w_o, k, v, cum):
    B, S, _ = h.shape
    HD = N_HEADS * HEAD_DIM
    p = h @ w_qg
    q = p[..., :HD].reshape(B, S, N_HEADS, HEAD_DIM).transpose(0, 2, 1, 3)
    gate = p[..., HD:]
    scale = HEAD_DIM ** -0.5
    k32 = k.astype(jnp.float32)
    v32 = v.astype(jnp.float32)
    key_pos = jnp.arange(S)
    n_blocks = S // Q_BLOCK

    def block(i):
        start = i * Q_BLOCK
        qb = lax.dynamic_slice_in_dim(q, start, Q_BLOCK, axis=2).astype(jnp.float32)
        cb = lax.dynamic_slice_in_dim(cum, start, Q_BLOCK, axis=2)
        s = jnp.einsum('bhqd,bhkd->bhqk', qb, k32) * scale + (cb[..., :, None] - cum[..., None, :])
        q_pos = start + jnp.arange(Q_BLOCK)
        s = jnp.where(key_pos[None, :] <= q_pos[:, None], s, -jnp.inf)
        pr = jax.nn.softmax(s, axis=-1)
        return jnp.einsum('bhqk,bhkd->bhqd', pr, v32)

    o = lax.map(block, jnp.arange(n_blocks))
    o = o.transpose(1, 0, 3, 2, 4).reshape(B, S, HD)
    o = o.astype(h.dtype) * jax.nn.sigmoid(gate)
    return o @ w_o


def hierarchical_moe(h, w_group, b_group, w_router, b_router, w_in, w_out):
    B, S, D = h.shape
    N = B * S
    xf = h.reshape(N, D)
    gl = (xf @ w_group).astype(jnp.float32) + b_group.astype(jnp.float32)
    gp = jax.nn.softmax(gl, axis=-1)
    gp_top, g_idx = lax.top_k(gp, 1)
    el = ((xf @ w_router).astype(jnp.float32) + b_router.astype(jnp.float32)).reshape(N, N_GROUPS, EXPERTS_PER_GROUP)
    el_sel = el[jnp.arange(N), g_idx[:, 0]]
    ep = jax.nn.softmax(el_sel, axis=-1)
    ep_top, e_idx = lax.top_k(ep, TOP_K)
    wts = gp_top * ep_top / jnp.sum(ep_top, axis=-1, keepdims=True)
    eid = (g_idx * EXPERTS_PER_GROUP + e_idx).reshape(-1).astype(jnp.int32)

    A = N * TOP_K
    tok = jnp.repeat(jnp.arange(N, dtype=jnp.int32), TOP_K)
    order = jnp.argsort(eid)
    eid_s = eid[order]
    tok_s = tok[order]
    wt_s = wts.reshape(-1)[order]
    counts = jnp.bincount(eid, length=N_EXPERTS)
    starts = jnp.cumsum(counts) - counts
    padded = (counts + ROW_BLOCK - 1) // ROW_BLOCK * ROW_BLOCK
    pends = jnp.cumsum(padded)
    pstarts = pends - padded
    dest = pstarts[eid_s] + jnp.arange(A, dtype=jnp.int32) - starts[eid_s]
    n_rows = (-(-A // ROW_BLOCK)) * ROW_BLOCK + N_EXPERTS * ROW_BLOCK
    row_tok = jnp.zeros((n_rows,), jnp.int32).at[dest].set(tok_s)
    row_wt = jnp.zeros((n_rows,), wt_s.dtype).at[dest].set(wt_s)
    n_chunks = n_rows // ROW_BLOCK
    chunk_start = jnp.arange(n_chunks, dtype=pends.dtype) * ROW_BLOCK
    chunk_e = jnp.minimum(jnp.searchsorted(pends, chunk_start, side='right'), N_EXPERTS - 1)
    xs = xf[row_tok].reshape(n_chunks, ROW_BLOCK, D)

    def run(args):
        xc, e = args
        gu = xc @ w_in[e]
        return (jax.nn.silu(gu[:, :D_EXPERT]) * gu[:, D_EXPERT:]) @ w_out[e]

    ys = lax.map(run, (xs, chunk_e)).reshape(n_rows, D)
    out = jax.ops.segment_sum(ys * row_wt[:, None].astype(ys.dtype), row_tok, num_segments=N)
    return out.reshape(B, S, D).astype(h.dtype)


def setup_inputs(seed: int = 0) -> dict:
    key = jax.random.key(seed)
    ks = jax.random.split(key, 32)
    f32 = jnp.float32
    HD = N_HEADS * HEAD_DIM

    def nrm(k, shape, scale):
        return jax.random.normal(k, shape, f32) * scale

    x = nrm(ks[0], (BATCH, SEQ, D_MODEL), 1.0)
    a_norm = 1.0 + nrm(ks[1], (N_A_LAYERS, D_MODEL), 0.01)
    a_w_in = nrm(ks[2], (N_A_LAYERS, D_MODEL, 2 * D_RNN), D_MODEL ** -0.5)
    a_conv_w = nrm(ks[3], (N_A_LAYERS, CONV_WIDTH, D_RNN), CONV_WIDTH ** -0.5)
    a_conv_b = nrm(ks[4], (N_A_LAYERS, D_RNN), 0.01)
    a_w_rec = nrm(ks[5], (N_A_LAYERS, N_RNN_BLOCKS, RNN_BLOCK, RNN_BLOCK), RNN_BLOCK ** -0.5)
    a_b_rec = nrm(ks[6], (N_A_LAYERS, D_RNN), 0.01)
    a_w_inp = nrm(ks[7], (N_A_LAYERS, N_RNN_BLOCKS, RNN_BLOCK, RNN_BLOCK), RNN_BLOCK ** -0.5)
    a_b_inp = nrm(ks[8], (N_A_LAYERS, D_RNN), 0.01)
    a_base = jax.random.uniform(ks[9], (N_A_LAYERS, D_RNN), f32, 0.9, 0.999)
    s = a_base ** (1.0 / LRU_C)
    a_lambda = jnp.log(s) - jnp.log1p(-s)
    a_w_out = nrm(ks[10], (N_A_LAYERS, D_RNN, D_MODEL), D_RNN ** -0.5)
    kv_norm = 1.0 + nrm(ks[11], (D_MODEL,), 0.01)
    kv_w = jnp.concatenate([nrm(ks[12], (D_MODEL, 2 * HD), D_MODEL ** -0.5),
                            nrm(ks[13], (D_MODEL, N_HEADS), 0.1 * D_MODEL ** -0.5)], axis=1)
    kv_b_forget = FORGET_BIAS_MEAN + nrm(ks[14], (N_HEADS,), 0.5)
    b_norm = 1.0 + nrm(ks[15], (N_B_LAYERS, D_MODEL), 0.01)
    b_w_qg = nrm(ks[16], (N_B_LAYERS, D_MODEL, 2 * HD), D_MODEL ** -0.5)
    b_w_o = nrm(ks[17], (N_B_LAYERS, HD, D_MODEL), HD ** -0.5)
    m_norm = 1.0 + nrm(ks[18], (DEPTH, D_MODEL), 0.01)
    m_w_group = nrm(ks[19], (DEPTH, D_MODEL, N_GROUPS), D_MODEL ** -0.5)
    m_b_group = nrm(ks[20], (DEPTH, N_GROUPS), 0.01)
    m_w_router = nrm(ks[21], (DEPTH, D_MODEL, N_EXPERTS), D_MODEL ** -0.5)
    m_b_router = nrm(ks[22], (DEPTH, N_EXPERTS), 0.01)
    m_w_in = nrm(ks[23], (DEPTH, N_EXPERTS, D_MODEL, 2 * D_EXPERT), D_MODEL ** -0.5)
    m_w_out = nrm(ks[24], (DEPTH, N_EXPERTS, D_EXPERT, D_MODEL), D_EXPERT ** -0.5)
    final_norm = 1.0 + nrm(ks[25], (D_MODEL,), 0.01)
    return {'x': x, 'a_norm': a_norm, 'a_w_in': a_w_in, 'a_conv_w': a_conv_w, 'a_conv_b': a_conv_b,
            'a_w_rec': a_w_rec, 'a_b_rec': a_b_rec, 'a_w_inp': a_w_inp, 'a_b_inp': a_b_inp,
            'a_lambda': a_lambda, 'a_w_out': a_w_out, 'kv_norm': kv_norm, 'kv_w': kv_w,
            'kv_b_forget': kv_b_forget, 'b_norm': b_norm, 'b_w_qg': b_w_qg, 'b_w_o': b_w_o,
            'm_norm': m_norm, 'm_w_group': m_w_group, 'm_b_group': m_b_group,
            'm_w_router': m_w_router, 'm_b_router': m_b_router, 'm_w_in': m_w_in,
            'm_w_out': m_w_out, 'final_norm': final_norm}


def reference(x, a_norm, a_w_in, a_conv_w, a_conv_b, a_w_rec, a_b_rec, a_w_inp, a_b_inp,
              a_lambda, a_w_out, kv_norm, kv_w, kv_b_forget, b_norm, b_w_qg, b_w_o,
              m_norm, m_w_group, m_b_group, m_w_router, m_b_router, m_w_in, m_w_out,
              final_norm):
    k = v = cum = None
    for layer in range(DEPTH):
        if layer < N_A_LAYERS:
            i = layer
            x = x + rglru_mixer(rms_norm(x, a_norm[i]), a_w_in[i], a_conv_w[i], a_conv_b[i],
                                a_w_rec[i], a_b_rec[i], a_w_inp[i], a_b_inp[i],
                                a_lambda[i], a_w_out[i])
        else:
            j = layer - N_A_LAYERS
            x = x + fox_mixer(rms_norm(x, b_norm[j]), b_w_qg[j], b_w_o[j], k, v, cum)
        x = x + hierarchical_moe(rms_norm(x, m_norm[layer]), m_w_group[layer], m_b_group[layer],
                                 m_w_router[layer], m_b_router[layer], m_w_in[layer], m_w_out[layer])
        if layer == N_A_LAYERS - 1:
            k, v, cum = shared_kv(x, kv_norm, kv_w, kv_b_forget)
    return rms_norm(x, final_norm)
```

```python
import functools
import math

import jax
import jax.numpy as jnp
from jax import lax
from jax.experimental import pallas as pl
from jax.experimental.pallas import tpu as pltpu

F32 = jnp.float32
BF16 = jnp.bfloat16
I32 = jnp.int32

EPS = 1e-6
LRU_C = 8.0
TOP_K = 2
LOG2E = 1.4426950408889634
MASK_VALUE = -1e30

V7X_VMEM_BYTES = 64 * 1024 * 1024
SUBLANES = 8
LANES = 128
MIB = 1024 * 1024


def _cparams(semantics, vmem_mib):
    assert vmem_mib * MIB < V7X_VMEM_BYTES
    return pltpu.CompilerParams(dimension_semantics=semantics, vmem_limit_bytes=vmem_mib * MIB)


def _tile(dim, pref):
    t = min(dim, pref)
    assert dim % t == 0, (dim, pref)
    return t


def _rms(x, gain):
    return x * lax.rsqrt(jnp.mean(x * x, axis=-1, keepdims=True) + EPS) * gain


def _norm_proj_kernel(x_ref, g_ref, w_ref, s_ref, o_ref, hn_ref):
    @pl.when(pl.program_id(1) == 0)
    def _():
        hn_ref[...] = _rms(x_ref[...], g_ref[...]).astype(BF16)

    acc = jnp.dot(hn_ref[...], w_ref[...], preferred_element_type=F32)
    o_ref[...] = (acc * s_ref[...]).astype(o_ref.dtype)


def _norm_proj(x, gain, w, col_scale, out_dtype, name):
    n, d = x.shape
    n_out = w.shape[1]
    tm = _tile(n, 1024)
    tn = _tile(n_out, 1024)
    return pl.pallas_call(
        _norm_proj_kernel,
        out_shape=jax.ShapeDtypeStruct((n, n_out), out_dtype),
        grid=(n // tm, n_out // tn),
        in_specs=[
            pl.BlockSpec((tm, d), lambda i, j: (i, 0)),
            pl.BlockSpec((1, d), lambda i, j: (0, 0)),
            pl.BlockSpec((d, tn), lambda i, j: (0, j)),
            pl.BlockSpec((1, tn), lambda i, j: (0, j)),
        ],
        out_specs=pl.BlockSpec((tm, tn), lambda i, j: (i, j)),
        scratch_shapes=[pltpu.VMEM((tm, d), BF16)],
        compiler_params=_cparams(("parallel", "arbitrary"), 48),
        name=name,
    )(x, gain.reshape(1, d), w, col_scale.reshape(1, n_out))


def _norm_proj_t_kernel(x_ref, g_ref, wt_ref, o_ref, hn_ref):
    @pl.when(pl.program_id(2) == 0)
    def _():
        hn_ref[...] = _rms(x_ref[0], g_ref[...]).astype(BF16)

    acc = lax.dot_general(wt_ref[...], hn_ref[...], (((1,), (1,)), ((), ())),
                          preferred_element_type=F32)
    o_ref[0] = acc.astype(o_ref.dtype)


def _norm_proj_t(x3, gain, w_t, out_dtype, name):
    b, s, d = x3.shape
    n_out = w_t.shape[0]
    tm = _tile(s, 1024)
    tn = _tile(n_out, 1024)
    return pl.pallas_call(
        _norm_proj_t_kernel,
        out_shape=jax.ShapeDtypeStruct((b, n_out, s), out_dtype),
        grid=(b, s // tm, n_out // tn),
        in_specs=[
            pl.BlockSpec((1, tm, d), lambda bi, i, j: (bi, i, 0)),
            pl.BlockSpec((1, d), lambda bi, i, j: (0, 0)),
            pl.BlockSpec((tn, d), lambda bi, i, j: (j, 0)),
        ],
        out_specs=pl.BlockSpec((1, tn, tm), lambda bi, i, j: (bi, j, i)),
        scratch_shapes=[pltpu.VMEM((tm, d), BF16)],
        compiler_params=_cparams(("parallel", "parallel", "arbitrary"), 48),
        name=name,
    )(x3, gain.reshape(1, d), w_t)


SCAN_STRIP = 512


def _rglru_kernel(u_ref, y_ref, x_ref, cw_ref, cb_ref, wr_ref, br_ref, wi_ref, bi_ref,
                  lam_ref, wo_ref, o_ref, ubuf_ref, a_ref, b_ref, h_ref, *, conv_width):
    ts, d = u_ref.shape
    nb, blk, _ = wr_ref.shape
    halo = SUBLANES

    @pl.when(pl.program_id(1) == 0)
    def _():
        ubuf_ref[0:halo, :] = jnp.zeros((halo, d), F32)
        h_ref[...] = jnp.zeros_like(h_ref)

    u = u_ref[...].astype(F32)
    ubuf_ref[halo:halo + ts, :] = u
    uc = u * cw_ref[conv_width - 1:conv_width, :] + cb_ref[...]
    for j in range(conv_width - 1):
        shift = conv_width - 1 - j
        uc = uc + ubuf_ref[halo - shift:halo - shift + ts, :] * cw_ref[j:j + 1, :]
    ubuf_ref[0:halo, :] = u[ts - halo:, :]

    ub = uc.astype(BF16)
    r_parts, i_parts = [], []
    for n in range(nb):
        ubn = ub[:, n * blk:(n + 1) * blk]
        r_parts.append(jnp.dot(ubn, wr_ref[n], preferred_element_type=F32))
        i_parts.append(jnp.dot(ubn, wi_ref[n], preferred_element_type=F32))
    r = jax.nn.sigmoid(jnp.concatenate(r_parts, axis=1) + br_ref[...])
    gi = jax.nn.sigmoid(jnp.concatenate(i_parts, axis=1) + bi_ref[...])
    lam = lam_ref[...]
    sp = jnp.maximum(-lam, 0.0) + jnp.log(1.0 + jnp.exp(-jnp.abs(lam)))
    log_a = (-LRU_C * r) * sp
    a_ref[...] = jnp.exp(log_a)
    th = jnp.tanh(log_a)
    b_ref[...] = jnp.sqrt(-2.0 * th / (1.0 - th)) * (gi * uc)

    row = lax.broadcasted_iota(I32, (SUBLANES, SCAN_STRIP), 0)
    strip = min(SCAN_STRIP, d)
    for c in range(d // strip):
        cols = pl.ds(c * strip, strip)

        def tile_step(t, h):
            rows = pl.ds(pl.multiple_of(t * SUBLANES, SUBLANES), SUBLANES)
            a = a_ref[rows, cols]
            bb = b_ref[rows, cols]
            for sh in (1, 2, 4):
                keep = row[:, :strip] >= sh
                a_prev = jnp.where(keep, pltpu.roll(a, sh, 0), 1.0)
                b_prev = jnp.where(keep, pltpu.roll(bb, sh, 0), 0.0)
                bb = a * b_prev + bb
                a = a * a_prev
            hs = a * h + bb
            b_ref[rows, cols] = hs
            return jnp.broadcast_to(hs[SUBLANES - 1:SUBLANES, :], (SUBLANES, strip))

        h_ref[:, cols] = lax.fori_loop(0, ts // SUBLANES, tile_step, h_ref[:, cols], unroll=2)

    y = y_ref[...].astype(F32)
    gelu = 0.5 * y * (1.0 + jnp.tanh(math.sqrt(2.0 / math.pi) * (y + 0.044715 * (y * y * y))))
    g = (b_ref[...] * gelu).astype(BF16)
    o_ref[...] = x_ref[...] + jnp.dot(g, wo_ref[...], preferred_element_type=F32)


def _rglru(proj, x3, conv_w, conv_b, w_rec, b_rec, w_inp, b_inp, lam, w_out):
    b, s, d = x3.shape
    d_rnn = w_out.shape[0]
    nb, blk, _ = w_rec.shape
    width = conv_w.shape[0]
    assert width - 1 <= SUBLANES
    ts = _tile(s, 256)
    proj3 = proj.reshape(b, s, 2 * d_rnn)
    row = lambda v: v.reshape(1, -1)
    const2 = lambda bi, i: (0, 0)
    const3 = lambda bi, i: (0, 0, 0)
    return pl.pallas_call(
        functools.partial(_rglru_kernel, conv_width=width),
        out_shape=jax.ShapeDtypeStruct((b, s, d), F32),
        grid=(b, s // ts),
        in_specs=[
            pl.BlockSpec((None, ts, d_rnn), lambda bi, i: (bi, i, 0)),
            pl.BlockSpec((None, ts, d_rnn), lambda bi, i: (bi, i, 1)),
            pl.BlockSpec((None, ts, d), lambda bi, i: (bi, i, 0)),
            pl.BlockSpec((width, d_rnn), const2),
            pl.BlockSpec((1, d_rnn), const2),
            pl.BlockSpec((nb, blk, blk), const3),
            pl.BlockSpec((1, d_rnn), const2),
            pl.BlockSpec((nb, blk, blk), const3),
            pl.BlockSpec((1, d_rnn), const2),
            pl.BlockSpec((1, d_rnn), const2),
            pl.BlockSpec((d_rnn, d), const2),
        ],
        out_specs=pl.BlockSpec((None, ts, d), lambda bi, i: (bi, i, 0)),
        scratch_shapes=[
            pltpu.VMEM((SUBLANES + ts, d_rnn), F32),
            pltpu.VMEM((ts, d_rnn), F32),
            pltpu.VMEM((ts, d_rnn), F32),
            pltpu.VMEM((SUBLANES, d_rnn), F32),
        ],
        compiler_params=_cparams(("parallel", "arbitrary"), 56),
        name="rglru",
    )(proj3, proj3, x3, conv_w, row(conv_b), w_rec.astype(BF16), row(b_rec),
      w_inp.astype(BF16), row(b_inp), row(lam), w_out.astype(BF16))


def _split3(v):
    v1 = v.astype(BF16)
    r1 = v - v1.astype(F32)
    v2 = r1.astype(BF16)
    v3 = (r1 - v2.astype(F32)).astype(BF16)
    return v1, v2, v3


def _dot_nt(a, b):
    return lax.dot_general(a, b, (((1,), (1,)), ((), ())), preferred_element_type=F32)


def _first_argmax(v, n):
    idx = lax.broadcasted_iota(I32, v.shape, 0)
    vmax = jnp.max(v, axis=0, keepdims=True)
    amax = jnp.min(jnp.where(v == vmax, idx, n), axis=0, keepdims=True)
    return amax, vmax


def _router_kernel(x_ref, g_ref, w1_ref, w2_ref, w3_ref, bias_ref, hn_ref, eid_ref, wt_ref,
                   rank_ref, cnt_ref, tri_ref, carry_ref, *, n_groups, per_group):
    tm = x_ref.shape[0]
    n_exp = n_groups * per_group

    @pl.when(pl.program_id(0) == 0)
    def _():
        r = lax.broadcasted_iota(I32, (tm, tm), 0)
        c = lax.broadcasted_iota(I32, (tm, tm), 1)
        tri_ref[...] = (r < c).astype(BF16)
        carry_ref[...] = jnp.zeros_like(carry_ref)

    hn = _rms(x_ref[...], g_ref[...])
    hn_ref[...] = hn
    h1, h2, h3 = _split3(hn)
    w1, w2, w3 = w1_ref[...], w2_ref[...], w3_ref[...]
    logits = (_dot_nt(w1, h1) + (_dot_nt(w1, h2) + _dot_nt(w2, h1))
              + (_dot_nt(w1, h3) + _dot_nt(w2, h2) + _dot_nt(w3, h1)))
    logits = logits + bias_ref[:, 0:1]

    gl = logits[0:n_groups, :]
    g_idx, g_max = _first_argmax(gl, n_groups)
    gp_top = 1.0 / jnp.sum(jnp.exp(gl - g_max), axis=0, keepdims=True)
    el = jnp.zeros((per_group, tm), F32)
    for g in range(n_groups):
        lo = n_groups + g * per_group
        el = jnp.where(g_idx == g, logits[lo:lo + per_group, :], el)
    e_max = jnp.max(el, axis=0, keepdims=True)
    ex = jnp.exp(el - e_max)
    ep = ex / jnp.sum(ex, axis=0, keepdims=True)
    e1, p1 = _first_argmax(ep, per_group)
    sub = lax.broadcasted_iota(I32, ep.shape, 0)
    e2, p2 = _first_argmax(jnp.where(sub == e1, -1.0, ep), per_group)
    denom = p1 + p2
    eid1 = g_idx * per_group + e1
    eid2 = g_idx * per_group + e2
    eid_ref[...] = jnp.concatenate([eid1, eid2], axis=0)
    wt_ref[...] = jnp.concatenate([gp_top * p1 / denom, gp_top * p2 / denom], axis=0)

    e_iota = lax.broadcasted_iota(I32, (n_exp, tm), 0)
    hot1 = e_iota == eid1
    hot2 = e_iota == eid2
    chosen = jnp.logical_or(hot1, hot2)
    before = jnp.dot(chosen.astype(BF16), tri_ref[...], preferred_element_type=F32)
    base = (before + carry_ref[:, 0:1]).astype(I32)
    rank1 = jnp.sum(jnp.where(hot1, base, 0), axis=0, keepdims=True)
    rank2 = jnp.sum(jnp.where(hot2, base, 0), axis=0, keepdims=True)
    rank_ref[...] = jnp.concatenate([rank1, rank2], axis=0)
    carry_ref[...] = carry_ref[...] + jnp.sum(chosen.astype(F32), axis=1, keepdims=True)
    cnt_ref[...] = carry_ref[...].astype(I32)


def _router(x2, gain, w_group, b_group, w_router, b_router):
    n, d = x2.shape
    n_groups = w_group.shape[1]
    n_exp = w_router.shape[1]
    per_group = n_exp // n_groups
    assert per_group == SUBLANES and n_groups == SUBLANES
    rows = n_groups + n_exp
    rows_p = -(-rows // LANES) * LANES
    w_t = jnp.concatenate([w_group, w_router], axis=1).T
    w_t = jnp.pad(w_t, ((0, rows_p - rows), (0, 0)))
    w1, w2, w3 = _split3(w_t)
    bias = jnp.pad(jnp.concatenate([b_group, b_router]), (0, rows_p - rows))
    bias = jnp.broadcast_to(bias[:, None], (rows_p, LANES))
    tm = _tile(n, 512)
    const = lambda i: (0, 0)
    return pl.pallas_call(
        functools.partial(_router_kernel, n_groups=n_groups, per_group=per_group),
        out_shape=(
            jax.ShapeDtypeStruct((n, d), F32),
            jax.ShapeDtypeStruct((TOP_K, n), I32),
            jax.ShapeDtypeStruct((TOP_K, n), F32),
            jax.ShapeDtypeStruct((TOP_K, n), I32),
            jax.ShapeDtypeStruct((n_exp, LANES), I32),
        ),
        grid=(n // tm,),
        in_specs=[
            pl.BlockSpec((tm, d), lambda i: (i, 0)),
            pl.BlockSpec((1, d), const),
            pl.BlockSpec((rows_p, d), const),
            pl.BlockSpec((rows_p, d), const),
            pl.BlockSpec((rows_p, d), const),
            pl.BlockSpec((rows_p, LANES), const),
        ],
        out_specs=(
            pl.BlockSpec((tm, d), lambda i: (i, 0)),
            pl.BlockSpec((TOP_K, tm), lambda i: (0, i)),
            pl.BlockSpec((TOP_K, tm), lambda i: (0, i)),
            pl.BlockSpec((TOP_K, tm), lambda i: (0, i)),
            pl.BlockSpec((n_exp, LANES), const),
        ),
        scratch_shapes=[pltpu.VMEM((tm, tm), BF16), pltpu.VMEM((n_exp, LANES), F32)],
        compiler_params=_cparams(("arbitrary",), 40),
        name="moe_router",
    )(x2, gain.reshape(1, d), w1, w2, w3, bias)


def _dest_kernel(eid_ref, rank_ref, start_ref, dest_ref):
    eid = eid_ref[...]
    n_exp = start_ref.shape[0]
    dest = rank_ref[...]
    for k in range(eid.shape[0]):
        hot = lax.broadcasted_iota(I32, (n_exp, eid.shape[1]), 0) == eid[k:k + 1, :]
        off = jnp.sum(jnp.where(hot, start_ref[:, 0:1], 0), axis=0, keepdims=True)
        dest_ref[k:k + 1, :] = dest[k:k + 1, :] + off


def _dest_rows(eid, rank, seg_start):
    k, n = eid.shape
    n_exp = seg_start.shape[0]
    tm = _tile(n, 2048)
    start = jnp.broadcast_to(seg_start[:, None], (n_exp, LANES)).astype(I32)
    return pl.pallas_call(
        _dest_kernel,
        out_shape=jax.ShapeDtypeStruct((k, n), I32),
        grid=(n // tm,),
        in_specs=[
            pl.BlockSpec((k, tm), lambda i: (0, i)),
            pl.BlockSpec((k, tm), lambda i: (0, i)),
            pl.BlockSpec((n_exp, LANES), lambda i: (0, 0)),
        ],
        out_specs=pl.BlockSpec((k, tm), lambda i: (0, i)),
        compiler_params=_cparams(("parallel",), 16),
        name="moe_dest",
    )(eid, rank, start)


def _dispatch_kernel(dest_ref, hn_ref, init_ref, xs_ref, sem):
    del init_ref
    k, tm = dest_ref.shape[1], dest_ref.shape[2]
    base = pl.program_id(0) * tm

    def row_copy(t, kk):
        return pltpu.make_async_copy(hn_ref.at[pl.ds(base + t, 1)],
                                     xs_ref.at[pl.ds(dest_ref[0, kk, t], 1)], sem)

    def issue(t, c):
        for kk in range(k):
            row_copy(t, kk).start()
        return c

    lax.fori_loop(0, tm, issue, 0)

    def drain(t, c):
        for kk in range(k):
            row_copy(t, kk).wait()
        return c

    lax.fori_loop(0, tm, drain, 0)


def _dispatch(dest, hn, n_rows):
    k, n = dest.shape
    d = hn.shape[1]
    tm = _tile(n, 512)
    dest3 = dest.reshape(k, n // tm, tm).transpose(1, 0, 2)
    return pl.pallas_call(
        _dispatch_kernel,
        out_shape=jax.ShapeDtypeStruct((n_rows, d), hn.dtype),
        grid=(n // tm,),
        in_specs=[
            pl.BlockSpec((1, k, tm), lambda i: (i, 0, 0), memory_space=pltpu.SMEM),
            pl.BlockSpec(memory_space=pl.ANY),
            pl.BlockSpec(memory_space=pl.ANY),
        ],
        out_specs=pl.BlockSpec(memory_space=pl.ANY),
        scratch_shapes=[pltpu.SemaphoreType.DMA(())],
        input_output_aliases={2: 0},
        compiler_params=_cparams(("arbitrary",), 16),
        name="moe_dispatch",
    )(dest3, hn, jnp.zeros((n_rows, d), hn.dtype))


def _expert_kernel(ce_ref, nu_ref, xs_ref, wi_ref, wo_ref, ys_ref, *, d_expert):
    @pl.when(pl.program_id(0) < nu_ref[0])
    def _():
        xb = xs_ref[...].astype(BF16)
        gu = jnp.dot(xb, wi_ref[0], preferred_element_type=F32)
        gate = gu[:, :d_expert]
        act = (gate * jax.nn.sigmoid(gate) * gu[:, d_expert:]).astype(BF16)
        ys_ref[...] = jnp.dot(act, wo_ref[0], preferred_element_type=F32)

    @pl.when(pl.program_id(0) >= nu_ref[0])
    def _():
        ys_ref[...] = jnp.zeros_like(ys_ref)


def _experts(xs, chunk_e, n_used, w_in, w_out, chunk_rows):
    n_rows, d = xs.shape
    d_expert = w_out.shape[1]
    n_chunks = n_rows // chunk_rows
    grid_spec = pltpu.PrefetchScalarGridSpec(
        num_scalar_prefetch=2,
        grid=(n_chunks,),
        in_specs=[
            pl.BlockSpec((chunk_rows, d), lambda i, ce, nu: (i, 0)),
            pl.BlockSpec((1, d, 2 * d_expert), lambda i, ce, nu: (ce[i], 0, 0)),
            pl.BlockSpec((1, d_expert, d), lambda i, ce, nu: (ce[i], 0, 0)),
        ],
        out_specs=pl.BlockSpec((chunk_rows, d), lambda i, ce, nu: (i, 0)),
    )
    return pl.pallas_call(
        functools.partial(_expert_kernel, d_expert=d_expert),
        out_shape=jax.ShapeDtypeStruct((n_rows, d), F32),
        grid_spec=grid_spec,
        compiler_params=_cparams(("arbitrary",), 48),
        name="moe_experts",
    )(chunk_e, n_used, xs, w_in, w_out)


def _combine_kernel(dest_ref, next_ref, x_ref, wt_ref, g_ref, ys_ref, o_ref, buf_ref, sem_ref, *,
                    final_norm, n_blocks):
    k, tm = dest_ref.shape[1], dest_ref.shape[2]
    i = pl.program_id(0)

    def row_copy(d_ref, slot, t, kk):
        return pltpu.make_async_copy(ys_ref.at[pl.ds(d_ref[0, kk, t], 1)],
                                     buf_ref.at[slot, kk, pl.ds(t, 1)], sem_ref.at[slot])

    def issue_block(d_ref, slot):
        def body(t, c):
            for kk in range(k):
                row_copy(d_ref, slot, t, kk).start()
            return c
        lax.fori_loop(0, tm, body, 0)

    @pl.when(i == 0)
    def _():
        issue_block(dest_ref, 0)

    @pl.when(i + 1 < n_blocks)
    def _():
        issue_block(next_ref, (i + 1) % 2)

    slot = i % 2

    def drain(t, c):
        for kk in range(k):
            row_copy(dest_ref, slot, t, kk).wait()
        return c

    lax.fori_loop(0, tm, drain, 0)

    out = x_ref[...]
    for kk in range(k):
        out = out + wt_ref[:, kk:kk + 1] * buf_ref[slot, kk]
    if final_norm:
        out = _rms(out, g_ref[...])
    o_ref[...] = out


def _combine(dest, x2, wts, ys, gain, final_norm):
    k, n = dest.shape
    d = x2.shape[1]
    tm = _tile(n, 256)
    n_blocks = n // tm
    dest3 = dest.reshape(k, n_blocks, tm).transpose(1, 0, 2)
    return pl.pallas_call(
        functools.partial(_combine_kernel, final_norm=final_norm, n_blocks=n_blocks),
        out_shape=jax.ShapeDtypeStruct((n, d), F32),
        grid=(n_blocks,),
        in_specs=[
            pl.BlockSpec((1, k, tm), lambda i: (i, 0, 0), memory_space=pltpu.SMEM),
            pl.BlockSpec((1, k, tm), lambda i: (jnp.minimum(i + 1, n_blocks - 1), 0, 0),
                         memory_space=pltpu.SMEM),
            pl.BlockSpec((tm, d), lambda i: (i, 0)),
            pl.BlockSpec((tm, k), lambda i: (i, 0)),
            pl.BlockSpec((1, d), lambda i: (0, 0)),
            pl.BlockSpec(memory_space=pl.ANY),
        ],
        out_specs=pl.BlockSpec((tm, d), lambda i: (i, 0)),
        scratch_shapes=[pltpu.VMEM((2, k, tm, d), F32), pltpu.SemaphoreType.DMA((2,))],
        compiler_params=_cparams(("arbitrary",), 40),
        name="moe_combine",
    )(dest3, dest3, x2, wts.T, gain.reshape(1, d), ys)


EXPERT_CHUNK_ROWS = 256


def _moe(x2, norm_gain, w_group, b_group, w_router, b_router, w_in, w_out, out_gain, final_norm):
    n, d = x2.shape
    n_exp = w_router.shape[1]
    chunk = EXPERT_CHUNK_ROWS
    hn, eid, wts, rank, counts = _router(x2, norm_gain, w_group, b_group, w_router, b_router)
    counts = counts[:, 0]
    padded = (counts + chunk - 1) // chunk * chunk
    seg_end = jnp.cumsum(padded)
    seg_start = seg_end - padded
    n_rows = (-(-(n * TOP_K) // chunk) + n_exp) * chunk
    n_chunks = n_rows // chunk
    chunk_start = jnp.arange(n_chunks, dtype=I32) * chunk
    chunk_e = jnp.minimum(jnp.searchsorted(seg_end, chunk_start, side='right'), n_exp - 1).astype(I32)
    n_used = (seg_end[-1:] // chunk).astype(I32)
    chunk_e = jnp.where(chunk_start < seg_end[-1], chunk_e, chunk_e[jnp.maximum(n_used[0] - 1, 0)])
    dest = _dest_rows(eid, rank, seg_start)
    xs = _dispatch(dest, hn, n_rows)
    ys = _experts(xs, chunk_e, n_used, w_in.astype(BF16), w_out.astype(BF16), chunk)
    return _combine(dest, x2, wts, ys, out_gain, final_norm)


def _forget_kernel(x_ref, g_ref, w_ref, b_ref, aux_ref, tri_ref, carry_ref, *, n_heads):
    ts = x_ref.shape[0]

    @pl.when(pl.program_id(1) == 0)
    def _():
        r = lax.broadcasted_iota(I32, (ts, ts), 0)
        c = lax.broadcasted_iota(I32, (ts, ts), 1)
        tri_ref[...] = (c <= r).astype(BF16)
        carry_ref[...] = jnp.zeros_like(carry_ref)

    hn = _rms(x_ref[...], g_ref[...]).astype(BF16)
    f = jnp.dot(hn, w_ref[...], preferred_element_type=F32) + b_ref[...]
    logf = jnp.minimum(f, 0.0) - jnp.log(1.0 + jnp.exp(-jnp.abs(f)))
    l1, l2, l3 = _split3(logf)
    tri = tri_ref[...]
    cum = (jnp.dot(tri, l1, preferred_element_type=F32)
           + jnp.dot(tri, l2, preferred_element_type=F32)
           + jnp.dot(tri, l3, preferred_element_type=F32)) + carry_ref[0:1, :]
    carry_ref[...] = jnp.broadcast_to(cum[ts - 1:ts, :], carry_ref.shape)
    c1, c2, c3 = [c.astype(F32) for c in _split3(cum * (-LOG2E))]
    lane = lax.broadcasted_iota(I32, (ts, LANES), 1)
    for h in range(n_heads):
        col = lambda v: jnp.broadcast_to(v[:, h:h + 1], (ts, LANES))
        aux = jnp.where(lane == 0, col(c1), jnp.where(lane == 1, col(c2),
                        jnp.where(lane == 2, col(c3), 0.0)))
        aux_ref[h] = aux.astype(BF16)


def _forget_aux(x3, gain, w_f, b_f):
    b, s, d = x3.shape
    n_heads = w_f.shape[1]
    assert n_heads <= LANES
    w_p = jnp.pad(w_f, ((0, 0), (0, LANES - n_heads))).astype(BF16)
    b_p = jnp.pad(b_f, (0, LANES - n_heads)).reshape(1, LANES)
    ts = _tile(s, 512)
    return pl.pallas_call(
        functools.partial(_forget_kernel, n_heads=n_heads),
        out_shape=jax.ShapeDtypeStruct((b, n_heads, s, LANES), BF16),
        grid=(b, s // ts),
        in_specs=[
            pl.BlockSpec((None, ts, d), lambda bi, i: (bi, i, 0)),
            pl.BlockSpec((1, d), lambda bi, i: (0, 0)),
            pl.BlockSpec((d, LANES), lambda bi, i: (0, 0)),
            pl.BlockSpec((1, LANES), lambda bi, i: (0, 0)),
        ],
        out_specs=pl.BlockSpec((None, n_heads, ts, LANES), lambda bi, i: (bi, 0, i, 0)),
        scratch_shapes=[pltpu.VMEM((ts, ts), BF16), pltpu.VMEM((SUBLANES, LANES), F32)],
        compiler_params=_cparams(("parallel", "arbitrary"), 32),
        name="forget_cumsum",
    )(x3, gain.reshape(1, d), w_p, b_p)


def _attn_kernel(q_ref, k_ref, aux_ref, vt_ref, o_ref, qp_ref, m_ref, l_ref, acc_ref):
    t, hd = q_ref.shape
    i = pl.program_id(2)
    lane = lax.broadcasted_iota(I32, (t, LANES), 1)
    qp_ref[:, 0:hd] = q_ref[...]
    qp_ref[:, hd:hd + LANES] = jnp.where(lane < 3, 1.0, 0.0).astype(BF16)
    m_ref[...] = jnp.full_like(m_ref, MASK_VALUE)
    l_ref[...] = jnp.zeros_like(l_ref)
    acc_ref[...] = jnp.zeros_like(acc_ref)

    def block(j, masked):
        rows = pl.ds(pl.multiple_of(j * t, t), t)
        kp = jnp.concatenate([k_ref[rows, :], aux_ref[rows, :]], axis=1)
        s_t = _dot_nt(kp, qp_ref[...])
        if masked:
            kr = lax.broadcasted_iota(I32, (t, t), 0)
            qc = lax.broadcasted_iota(I32, (t, t), 1)
            s_t = jnp.where(kr <= qc, s_t, MASK_VALUE)
        m_old = m_ref[...]
        m_new = jnp.maximum(m_old, jnp.max(s_t, axis=0, keepdims=True))
        alpha = jnp.exp2(m_old - m_new)
        p = jnp.exp2(s_t - m_new)
        l_ref[...] = alpha * l_ref[...] + jnp.sum(p, axis=0, keepdims=True)
        pv = jnp.dot(vt_ref[:, rows], p.astype(BF16), preferred_element_type=F32)
        acc_ref[...] = alpha * acc_ref[...] + pv
        m_ref[...] = m_new

    def body(j, c):
        block(j, False)
        return c

    lax.fori_loop(0, i, body, 0)
    block(i, True)
    o_ref[...] = (acc_ref[...] / l_ref[...]).T.astype(o_ref.dtype)


def _attention(qg3, k3, aux, vt, n_heads):
    b, s, _ = k3.shape
    hd = k3.shape[2] // n_heads
    assert hd == LANES
    t = _tile(s, 512)
    return pl.pallas_call(
        _attn_kernel,
        out_shape=jax.ShapeDtypeStruct((b, s, n_heads * hd), BF16),
        grid=(b, n_heads, s // t),
        in_specs=[
            pl.BlockSpec((None, t, hd), lambda bi, h, i: (bi, i, h)),
            pl.BlockSpec((None, s, hd), lambda bi, h, i: (bi, 0, h)),
            pl.BlockSpec((None, None, s, LANES), lambda bi, h, i: (bi, h, 0, 0)),
            pl.BlockSpec((None, hd, s), lambda bi, h, i: (bi, h, 0)),
        ],
        out_specs=pl.BlockSpec((None, t, hd), lambda bi, h, i: (bi, i, h)),
        scratch_shapes=[
            pltpu.VMEM((t, hd + LANES), BF16),
            pltpu.VMEM((1, t), F32),
            pltpu.VMEM((1, t), F32),
            pltpu.VMEM((hd, t), F32),
        ],
        compiler_params=_cparams(("parallel", "parallel", "arbitrary"), 48),
        name="fox_attention",
    )(qg3, k3, aux, vt)


def _gated_out_kernel(o_ref, gate_ref, x_ref, w_ref, out_ref):
    g = o_ref[...].astype(F32) * jax.nn.sigmoid(gate_ref[...].astype(F32))
    out_ref[...] = x_ref[...] + jnp.dot(g.astype(BF16), w_ref[...], preferred_element_type=F32)


def _gated_out(o2, qg, x2, w_o):
    n, d = x2.shape
    hd_all = o2.shape[1]
    tm = _tile(n, 512)
    return pl.pallas_call(
        _gated_out_kernel,
        out_shape=jax.ShapeDtypeStruct((n, d), F32),
        grid=(n // tm,),
        in_specs=[
            pl.BlockSpec((tm, hd_all), lambda i: (i, 0)),
            pl.BlockSpec((tm, hd_all), lambda i: (i, 1)),
            pl.BlockSpec((tm, d), lambda i: (i, 0)),
            pl.BlockSpec((hd_all, d), lambda i: (0, 0)),
        ],
        out_specs=pl.BlockSpec((tm, d), lambda i: (i, 0)),
        compiler_params=_cparams(("parallel",), 48),
        name="gated_out_proj",
    )(o2, qg, x2, w_o)


def kernel(x, a_norm, a_w_in, a_conv_w, a_conv_b, a_w_rec, a_b_rec, a_w_inp, a_b_inp, a_lambda, a_w_out, kv_norm, kv_w, kv_b_forget, b_norm, b_w_qg, b_w_o, m_norm, m_w_group, m_b_group, m_w_router, m_b_router, m_w_in, m_w_out, final_norm):
    b, s, d = x.shape
    n = b * s
    depth = m_norm.shape[0]
    n_a = a_norm.shape[0]
    n_heads = kv_b_forget.shape[0]
    hd_all = b_w_o.shape[1]
    head_dim = hd_all // n_heads
    x2 = x.reshape(n, d)
    k3 = aux = vt = None
    for layer in range(depth):
        if layer < n_a:
            i = layer
            d_rnn = a_w_out.shape[1]
            proj = _norm_proj(x2, a_norm[i], a_w_in[i].astype(BF16), jnp.ones((2 * d_rnn,), F32),
                              BF16, "rglru_in_proj")
            x2 = _rglru(proj, x2.reshape(b, s, d), a_conv_w[i], a_conv_b[i], a_w_rec[i], a_b_rec[i],
                        a_w_inp[i], a_b_inp[i], a_lambda[i], a_w_out[i]).reshape(n, d)
        else:
            j = layer - n_a
            q_scale = jnp.concatenate([jnp.full((hd_all,), head_dim ** -0.5 * LOG2E, F32),
                                       jnp.ones((hd_all,), F32)])
            qg = _norm_proj(x2, b_norm[j], b_w_qg[j].astype(BF16), q_scale, BF16, "fox_qg_proj")
            o = _attention(qg.reshape(b, s, 2 * hd_all), k3, aux, vt, n_heads)
            x2 = _gated_out(o.reshape(n, hd_all), qg, x2, b_w_o[j].astype(BF16))
        last = layer == depth - 1
        x2 = _moe(x2, m_norm[layer], m_w_group[layer], m_b_group[layer], m_w_router[layer],
                  m_b_router[layer], m_w_in[layer], m_w_out[layer], final_norm, last)
        if layer == n_a - 1:
            x3 = x2.reshape(b, s, d)
            k3 = _norm_proj(x2, kv_norm, kv_w[:, :hd_all].astype(BF16), jnp.ones((hd_all,), F32),
                            BF16, "shared_k_proj").reshape(b, s, hd_all)
            vt = _norm_proj_t(x3, kv_norm, kv_w[:, hd_all:2 * hd_all].T.astype(BF16), BF16,
                              "shared_vt_proj")
            aux = _forget_aux(x3, kv_norm, kv_w[:, 2 * hd_all:], kv_b_forget)
    if depth == 0:
        x2 = _rms(x2, final_norm)
    return x2.reshape(b, s, d)
```

```python
import functools
import math

import jax
import jax.numpy as jnp
from jax import lax
from jax.experimental import pallas as pl
from jax.experimental.pallas import tpu as pltpu

F32 = jnp.float32
BF16 = jnp.bfloat16
I32 = jnp.int32

EPS = 1e-6
LRU_C = 8.0
TOP_K = 2
LOG2E = 1.4426950408889634
MASK_VALUE = -1e30

V7X_VMEM_BYTES = 64 * 1024 * 1024
SUBLANES = 8
LANES = 128
MIB = 1024 * 1024


def _cparams(semantics, vmem_mib):
    assert vmem_mib * MIB < V7X_VMEM_BYTES
    return pltpu.CompilerParams(dimension_semantics=semantics, vmem_limit_bytes=vmem_mib * MIB)


def _tile(dim, pref):
    t = min(dim, pref)
    assert dim % t == 0, (dim, pref)
    return t


def _rms(x, gain):
    return x * lax.rsqrt(jnp.mean(x * x, axis=-1, keepdims=True) + EPS) * gain


def _norm_proj_kernel(x_ref, g_ref, w_ref, s_ref, o_ref, hn_ref):
    @pl.when(pl.program_id(1) == 0)
    def _():
        hn_ref[...] = _rms(x_ref[...], g_ref[...]).astype(BF16)

    acc = jnp.dot(hn_ref[...], w_ref[...], preferred_element_type=F32)
    o_ref[...] = (acc * s_ref[...]).astype(o_ref.dtype)


def _norm_proj(x, gain, w, col_scale, out_dtype, name):
    n, d = x.shape
    n_out = w.shape[1]
    tm = _tile(n, 1024)
    tn = _tile(n_out, 1024)
    return pl.pallas_call(
        _norm_proj_kernel,
        out_shape=jax.ShapeDtypeStruct((n, n_out), out_dtype),
        grid=(n // tm, n_out // tn),
        in_specs=[
            pl.BlockSpec((tm, d), lambda i, j: (i, 0)),
            pl.BlockSpec((1, d), lambda i, j: (0, 0)),
            pl.BlockSpec((d, tn), lambda i, j: (0, j)),
            pl.BlockSpec((1, tn), lambda i, j: (0, j)),
        ],
        out_specs=pl.BlockSpec((tm, tn), lambda i, j: (i, j)),
        scratch_shapes=[pltpu.VMEM((tm, d), BF16)],
        compiler_params=_cparams(("parallel", "arbitrary"), 48),
        name=name,
    )(x, gain.reshape(1, d), w, col_scale.reshape(1, n_out))


def _norm_proj_t_kernel(x_ref, g_ref, wt_ref, o_ref, hn_ref):
    @pl.when(pl.program_id(2) == 0)
    def _():
        hn_ref[...] = _rms(x_ref[0], g_ref[...]).astype(BF16)

    acc = lax.dot_general(wt_ref[...], hn_ref[...], (((1,), (1,)), ((), ())),
                          preferred_element_type=F32)
    o_ref[0] = acc.astype(o_ref.dtype)


def _norm_proj_t(x3, gain, w_t, out_dtype, name):
    b, s, d = x3.shape
    n_out = w_t.shape[0]
    tm = _tile(s, 1024)
    tn = _tile(n_out, 1024)
    return pl.pallas_call(
        _norm_proj_t_kernel,
        out_shape=jax.ShapeDtypeStruct((b, n_out, s), out_dtype),
        grid=(b, s // tm, n_out // tn),
        in_specs=[
            pl.BlockSpec((1, tm, d), lambda bi, i, j: (bi, i, 0)),
            pl.BlockSpec((1, d), lambda bi, i, j: (0, 0)),
            pl.BlockSpec((tn, d), lambda bi, i, j: (j, 0)),
        ],
        out_specs=pl.BlockSpec((1, tn, tm), lambda bi, i, j: (bi, j, i)),
        scratch_shapes=[pltpu.VMEM((tm, d), BF16)],
        compiler_params=_cparams(("parallel", "parallel", "arbitrary"), 48),
        name=name,
    )(x3, gain.reshape(1, d), w_t)


SCAN_STRIP = 512


def _rglru_kernel(u_ref, y_ref, x_ref, cw_ref, cb_ref, wr_ref, br_ref, wi_ref, bi_ref,
                  lam_ref, wo_ref, o_ref, ubuf_ref, a_ref, b_ref, h_ref, *, conv_width):
    ts, d = u_ref.shape
    nb, blk, _ = wr_ref.shape
    halo = SUBLANES

    @pl.when(pl.program_id(1) == 0)
    def _():
        ubuf_ref[0:halo, :] = jnp.zeros((halo, d), F32)
        h_ref[...] = jnp.zeros_like(h_ref)

    u = u_ref[...].astype(F32)
    ubuf_ref[halo:halo + ts, :] = u
    uc = u * cw_ref[conv_width - 1:conv_width, :] + cb_ref[...]
    for j in range(conv_width - 1):
        shift = conv_width - 1 - j
        uc = uc + ubuf_ref[halo - shift:halo - shift + ts, :] * cw_ref[j:j + 1, :]
    ubuf_ref[0:halo, :] = u[ts - halo:, :]

    ub = uc.astype(BF16)
    r_parts, i_parts = [], []
    for n in range(nb):
        ubn = ub[:, n * blk:(n + 1) * blk]
        r_parts.append(jnp.dot(ubn, wr_ref[n], preferred_element_type=F32))
        i_parts.append(jnp.dot(ubn, wi_ref[n], preferred_element_type=F32))
    r = jax.nn.sigmoid(jnp.concatenate(r_parts, axis=1) + br_ref[...])
    gi = jax.nn.sigmoid(jnp.concatenate(i_parts, axis=1) + bi_ref[...])
    lam = lam_ref[...]
    sp = jnp.maximum(-lam, 0.0) + jnp.log(1.0 + jnp.exp(-jnp.abs(lam)))
    log_a = (-LRU_C * r) * sp
    a_ref[...] = jnp.exp(log_a)
    th = jnp.tanh(log_a)
    b_ref[...] = jnp.sqrt(-2.0 * th / (1.0 - th)) * (gi * uc)

    row = lax.broadcasted_iota(I32, (SUBLANES, SCAN_STRIP), 0)
    strip = min(SCAN_STRIP, d)
    for c in range(d // strip):
        cols = pl.ds(c * strip, strip)

        def tile_step(t, h):
            rows = pl.ds(pl.multiple_of(t * SUBLANES, SUBLANES), SUBLANES)
            a = a_ref[rows, cols]
            bb = b_ref[rows, cols]
            for sh in (1, 2, 4):
                keep = row[:, :strip] >= sh
                a_prev = jnp.where(keep, pltpu.roll(a, sh, 0), 1.0)
                b_prev = jnp.where(keep, pltpu.roll(bb, sh, 0), 0.0)
                bb = a * b_prev + bb
                a = a * a_prev
            hs = a * h + bb
            b_ref[rows, cols] = hs
            return jnp.broadcast_to(hs[SUBLANES - 1:SUBLANES, :], (SUBLANES, strip))

        h_ref[:, cols] = lax.fori_loop(0, ts // SUBLANES, tile_step, h_ref[:, cols], unroll=2)

    y = y_ref[...].astype(F32)
    gelu = 0.5 * y * (1.0 + jnp.tanh(math.sqrt(2.0 / math.pi) * (y + 0.044715 * (y * y * y))))
    g = (b_ref[...] * gelu).astype(BF16)
    o_ref[...] = x_ref[...] + jnp.dot(g, wo_ref[...], preferred_element_type=F32)


def _rglru(proj, x3, conv_w, conv_b, w_rec, b_rec, w_inp, b_inp, lam, w_out):
    b, s, d = x3.shape
    d_rnn = w_out.shape[0]
    nb, blk, _ = w_rec.shape
    width = conv_w.shape[0]
    assert width - 1 <= SUBLANES
    ts = _tile(s, 256)
    proj3 = proj.reshape(b, s, 2 * d_rnn)
    row = lambda v: v.reshape(1, -1)
    const2 = lambda bi, i: (0, 0)
    const3 = lambda bi, i: (0, 0, 0)
    return pl.pallas_call(
        functools.partial(_rglru_kernel, conv_width=width),
        out_shape=jax.ShapeDtypeStruct((b, s, d), F32),
        grid=(b, s // ts),
        in_specs=[
            pl.BlockSpec((None, ts, d_rnn), lambda bi, i: (bi, i, 0)),
            pl.BlockSpec((None, ts, d_rnn), lambda bi, i: (bi, i, 1)),
            pl.BlockSpec((None, ts, d), lambda bi, i: (bi, i, 0)),
            pl.BlockSpec((width, d_rnn), const2),
            pl.BlockSpec((1, d_rnn), const2),
            pl.BlockSpec((nb, blk, blk), const3),
            pl.BlockSpec((1, d_rnn), const2),
            pl.BlockSpec((nb, blk, blk), const3),
            pl.BlockSpec((1, d_rnn), const2),
            pl.BlockSpec((1, d_rnn), const2),
            pl.BlockSpec((d_rnn, d), const2),
        ],
        out_specs=pl.BlockSpec((None, ts, d), lambda bi, i: (bi, i, 0)),
        scratch_shapes=[
            pltpu.VMEM((SUBLANES + ts, d_rnn), F32),
            pltpu.VMEM((ts, d_rnn), F32),
            pltpu.VMEM((ts, d_rnn), F32),
            pltpu.VMEM((SUBLANES, d_rnn), F32),
        ],
        compiler_params=_cparams(("parallel", "arbitrary"), 56),
        name="rglru",
    )(proj3, proj3, x3, conv_w, row(conv_b), w_rec.astype(BF16), row(b_rec),
      w_inp.astype(BF16), row(b_inp), row(lam), w_out.astype(BF16))


def _split3(v):
    v1 = v.astype(BF16)
    r1 = v - v1.astype(F32)
    v2 = r1.astype(BF16)
    v3 = (r1 - v2.astype(F32)).astype(BF16)
    return v1, v2, v3


def _dot_nt(a, b):
    return lax.dot_general(a, b, (((1,), (1,)), ((), ())), preferred_element_type=F32)


def _first_argmax(v, n):
    idx = lax.broadcasted_iota(I32, v.shape, 0)
    vmax = jnp.max(v, axis=0, keepdims=True)
    amax = jnp.min(jnp.where(v == vmax, idx, n), axis=0, keepdims=True)
    return amax, vmax


def _router_kernel(x_ref, g_ref, w1_ref, w2_ref, w3_ref, bias_ref, hn_ref, eid_ref, wt_ref,
                   rank_ref, cnt_ref, tri_ref, carry_ref, *, n_groups, per_group):
    tm = x_ref.shape[0]
    n_exp = n_groups * per_group

    @pl.when(pl.program_id(0) == 0)
    def _():
        r = lax.broadcasted_iota(I32, (tm, tm), 0)
        c = lax.broadcasted_iota(I32, (tm, tm), 1)
        tri_ref[...] = (r < c).astype(BF16)
        carry_ref[...] = jnp.zeros_like(carry_ref)

    hn = _rms(x_ref[...], g_ref[...])
    hn_ref[...] = hn
    h1, h2, h3 = _split3(hn)
    w1, w2, w3 = w1_ref[...], w2_ref[...], w3_ref[...]
    logits = (_dot_nt(w1, h1) + (_dot_nt(w1, h2) + _dot_nt(w2, h1))
              + (_dot_nt(w1, h3) + _dot_nt(w2, h2) + _dot_nt(w3, h1)))
    logits = logits + bias_ref[:, 0:1]

    gl = logits[0:n_groups, :]
    g_idx, g_max = _first_argmax(gl, n_groups)
    gp_top = 1.0 / jnp.sum(jnp.exp(gl - g_max), axis=0, keepdims=True)
    el = jnp.zeros((per_group, tm), F32)
    for g in range(n_groups):
        lo = n_groups + g * per_group
        el = jnp.where(g_idx == g, logits[lo:lo + per_group, :], el)
    e_max = jnp.max(el, axis=0, keepdims=True)
    ex = jnp.exp(el - e_max)
    ep = ex / jnp.sum(ex, axis=0, keepdims=True)
    e1, p1 = _first_argmax(ep, per_group)
    sub = lax.broadcasted_iota(I32, ep.shape, 0)
    e2, p2 = _first_argmax(jnp.where(sub == e1, -1.0, ep), per_group)
    denom = p1 + p2
    eid1 = g_idx * per_group + e1
    eid2 = g_idx * per_group + e2
    eid_ref[...] = jnp.concatenate([eid1, eid2], axis=0)
    wt_ref[...] = jnp.concatenate([gp_top * p1 / denom, gp_top * p2 / denom], axis=0)

    e_iota = lax.broadcasted_iota(I32, (n_exp, tm), 0)
    hot1 = e_iota == eid1
    hot2 = e_iota == eid2
    chosen = jnp.logical_or(hot1, hot2)
    before = jnp.dot(chosen.astype(BF16), tri_ref[...], preferred_element_type=F32)
    base = (before + carry_ref[:, 0:1]).astype(I32)
    rank1 = jnp.sum(jnp.where(hot1, base, 0), axis=0, keepdims=True)
    rank2 = jnp.sum(jnp.where(hot2, base, 0), axis=0, keepdims=True)
    rank_ref[...] = jnp.concatenate([rank1, rank2], axis=0)
    carry_ref[...] = carry_ref[...] + jnp.sum(chosen.astype(F32), axis=1, keepdims=True)
    cnt_ref[...] = carry_ref[...].astype(I32)


def _router(x2, gain, w_group, b_group, w_router, b_router):
    n, d = x2.shape
    n_groups = w_group.shape[1]
    n_exp = w_router.shape[1]
    per_group = n_exp // n_groups
    assert per_group == SUBLANES and n_groups == SUBLANES
    rows = n_groups + n_exp
    rows_p = -(-rows // LANES) * LANES
    w_t = jnp.concatenate([w_group, w_router], axis=1).T
    w_t = jnp.pad(w_t, ((0, rows_p - rows), (0, 0)))
    w1, w2, w3 = _split3(w_t)
    bias = jnp.pad(jnp.concatenate([b_group, b_router]), (0, rows_p - rows))
    bias = jnp.broadcast_to(bias[:, None], (rows_p, LANES))
    tm = _tile(n, 512)
    const = lambda i: (0, 0)
    return pl.pallas_call(
        functools.partial(_router_kernel, n_groups=n_groups, per_group=per_group),
        out_shape=(
            jax.ShapeDtypeStruct((n, d), F32),
            jax.ShapeDtypeStruct((TOP_K, n), I32),
            jax.ShapeDtypeStruct((TOP_K, n), F32),
            jax.ShapeDtypeStruct((TOP_K, n), I32),
            jax.ShapeDtypeStruct((n_exp, LANES), I32),
        ),
        grid=(n // tm,),
        in_specs=[
            pl.BlockSpec((tm, d), lambda i: (i, 0)),
            pl.BlockSpec((1, d), const),
            pl.BlockSpec((rows_p, d), const),
            pl.BlockSpec((rows_p, d), const),
            pl.BlockSpec((rows_p, d), const),
            pl.BlockSpec((rows_p, LANES), const),
        ],
        out_specs=(
            pl.BlockSpec((tm, d), lambda i: (i, 0)),
            pl.BlockSpec((TOP_K, tm), lambda i: (0, i)),
            pl.BlockSpec((TOP_K, tm), lambda i: (0, i)),
            pl.BlockSpec((TOP_K, tm), lambda i: (0, i)),
            pl.BlockSpec((n_exp, LANES), const),
        ),
        scratch_shapes=[pltpu.VMEM((tm, tm), BF16), pltpu.VMEM((n_exp, LANES), F32)],
        compiler_params=_cparams(("arbitrary",), 40),
        name="moe_router",
    )(x2, gain.reshape(1, d), w1, w2, w3, bias)


def _dest_kernel(eid_ref, rank_ref, start_ref, dest_ref):
    eid = eid_ref[...]
    n_exp = start_ref.shape[0]
    dest = rank_ref[...]
    for k in range(eid.shape[0]):
        hot = lax.broadcasted_iota(I32, (n_exp, eid.shape[1]), 0) == eid[k:k + 1, :]
        off = jnp.sum(jnp.where(hot, start_ref[:, 0:1], 0), axis=0, keepdims=True)
        dest_ref[k:k + 1, :] = dest[k:k + 1, :] + off


def _dest_rows(eid, rank, seg_start):
    k, n = eid.shape
    n_exp = seg_start.shape[0]
    tm = _tile(n, 2048)
    start = jnp.broadcast_to(seg_start[:, None], (n_exp, LANES)).astype(I32)
    return pl.pallas_call(
        _dest_kernel,
        out_shape=jax.ShapeDtypeStruct((k, n), I32),
        grid=(n // tm,),
        in_specs=[
            pl.BlockSpec((k, tm), lambda i: (0, i)),
            pl.BlockSpec((k, tm), lambda i: (0, i)),
            pl.BlockSpec((n_exp, LANES), lambda i: (0, 0)),
        ],
        out_specs=pl.BlockSpec((k, tm), lambda i: (0, i)),
        compiler_params=_cparams(("parallel",), 16),
        name="moe_dest",
    )(eid, rank, start)


def _dispatch_kernel(dest_ref, hn_ref, init_ref, xs_ref, sem):
    del init_ref
    k, tm = dest_ref.shape[1], dest_ref.shape[2]

    def issue(t, c):
        for kk in range(k):
            pltpu.make_async_copy(hn_ref.at[pl.ds(t, 1)],
                                  xs_ref.at[pl.ds(dest_ref[0, kk, t], 1)], sem).start()
        return c

    lax.fori_loop(0, tm, issue, 0, unroll=8)
    for kk in range(k):
        pltpu.make_async_copy(hn_ref, xs_ref.at[pl.ds(0, tm)], sem).wait()


def _dispatch(dest, hn, n_rows):
    k, n = dest.shape
    d = hn.shape[1]
    tm = _tile(n, 512)
    dest3 = dest.reshape(k, n // tm, tm).transpose(1, 0, 2)
    return pl.pallas_call(
        _dispatch_kernel,
        out_shape=jax.ShapeDtypeStruct((n_rows, d), hn.dtype),
        grid=(n // tm,),
        in_specs=[
            pl.BlockSpec((1, k, tm), lambda i: (i, 0, 0), memory_space=pltpu.SMEM),
            pl.BlockSpec((tm, d), lambda i: (i, 0)),
            pl.BlockSpec(memory_space=pl.ANY),
        ],
        out_specs=pl.BlockSpec(memory_space=pl.ANY),
        scratch_shapes=[pltpu.SemaphoreType.DMA(())],
        input_output_aliases={2: 0},
        compiler_params=_cparams(("arbitrary",), 24),
        name="moe_dispatch",
    )(dest3, hn, jnp.zeros((n_rows, d), hn.dtype))


def _expert_kernel(ce_ref, nu_ref, xs_ref, wi_ref, wo_ref, ys_ref, *, d_expert):
    @pl.when(pl.program_id(0) < nu_ref[0])
    def _():
        xb = xs_ref[...].astype(BF16)
        gu = jnp.dot(xb, wi_ref[0], preferred_element_type=F32)
        gate = gu[:, :d_expert]
        act = (gate * jax.nn.sigmoid(gate) * gu[:, d_expert:]).astype(BF16)
        ys_ref[...] = jnp.dot(act, wo_ref[0], preferred_element_type=F32)

    @pl.when(pl.program_id(0) >= nu_ref[0])
    def _():
        ys_ref[...] = jnp.zeros_like(ys_ref)


def _experts(xs, chunk_e, n_used, w_in, w_out, chunk_rows):
    n_rows, d = xs.shape
    d_expert = w_out.shape[1]
    n_chunks = n_rows // chunk_rows
    grid_spec = pltpu.PrefetchScalarGridSpec(
        num_scalar_prefetch=2,
        grid=(n_chunks,),
        in_specs=[
            pl.BlockSpec((chunk_rows, d), lambda i, ce, nu: (i, 0)),
            pl.BlockSpec((1, d, 2 * d_expert), lambda i, ce, nu: (ce[i], 0, 0)),
            pl.BlockSpec((1, d_expert, d), lambda i, ce, nu: (ce[i], 0, 0)),
        ],
        out_specs=pl.BlockSpec((chunk_rows, d), lambda i, ce, nu: (i, 0)),
    )
    return pl.pallas_call(
        functools.partial(_expert_kernel, d_expert=d_expert),
        out_shape=jax.ShapeDtypeStruct((n_rows, d), F32),
        grid_spec=grid_spec,
        compiler_params=_cparams(("arbitrary",), 48),
        name="moe_experts",
    )(chunk_e, n_used, xs, w_in, w_out)


def _combine_kernel(dest_ref, next_ref, x_ref, wt_ref, g_ref, ys_ref, o_ref, buf_ref, sem_ref, *,
                    final_norm, n_blocks):
    k, tm = dest_ref.shape[1], dest_ref.shape[2]
    i = pl.program_id(0)

    def issue_block(d_ref, slot):
        def body(t, c):
            for kk in range(k):
                pltpu.make_async_copy(ys_ref.at[pl.ds(d_ref[0, kk, t], 1)],
                                      buf_ref.at[slot, kk, pl.ds(t, 1)], sem_ref.at[slot]).start()
            return c
        lax.fori_loop(0, tm, body, 0, unroll=8)

    @pl.when(i == 0)
    def _():
        issue_block(dest_ref, 0)

    @pl.when(i + 1 < n_blocks)
    def _():
        issue_block(next_ref, (i + 1) % 2)

    slot = i % 2
    for kk in range(k):
        pltpu.make_async_copy(ys_ref.at[pl.ds(0, tm)], buf_ref.at[slot, kk], sem_ref.at[slot]).wait()

    out = x_ref[...]
    for kk in range(k):
        out = out + wt_ref[:, kk:kk + 1] * buf_ref[slot, kk]
    if final_norm:
        out = _rms(out, g_ref[...])
    o_ref[...] = out


def _combine(dest, x2, wts, ys, gain, final_norm):
    k, n = dest.shape
    d = x2.shape[1]
    tm = _tile(n, 256)
    n_blocks = n // tm
    dest3 = dest.reshape(k, n_blocks, tm).transpose(1, 0, 2)
    return pl.pallas_call(
        functools.partial(_combine_kernel, final_norm=final_norm, n_blocks=n_blocks),
        out_shape=jax.ShapeDtypeStruct((n, d), F32),
        grid=(n_blocks,),
        in_specs=[
            pl.BlockSpec((1, k, tm), lambda i: (i, 0, 0), memory_space=pltpu.SMEM),
            pl.BlockSpec((1, k, tm), lambda i: (jnp.minimum(i + 1, n_blocks - 1), 0, 0),
                         memory_space=pltpu.SMEM),
            pl.BlockSpec((tm, d), lambda i: (i, 0)),
            pl.BlockSpec((tm, k), lambda i: (i, 0)),
            pl.BlockSpec((1, d), lambda i: (0, 0)),
            pl.BlockSpec(memory_space=pl.ANY),
        ],
        out_specs=pl.BlockSpec((tm, d), lambda i: (i, 0)),
        scratch_shapes=[pltpu.VMEM((2, k, tm, d), F32), pltpu.SemaphoreType.DMA((2,))],
        compiler_params=_cparams(("arbitrary",), 40),
        name="moe_combine",
    )(dest3, dest3, x2, wts.T, gain.reshape(1, d), ys)


EXPERT_CHUNK_ROWS = 256


def _moe(x2, norm_gain, w_group, b_group, w_router, b_router, w_in, w_out, out_gain, final_norm):
    n, d = x2.shape
    n_exp = w_router.shape[1]
    chunk = EXPERT_CHUNK_ROWS
    hn, eid, wts, rank, counts = _router(x2, norm_gain, w_group, b_group, w_router, b_router)
    counts = counts[:, 0]
    padded = (counts + chunk - 1) // chunk * chunk
    seg_end = jnp.cumsum(padded)
    seg_start = seg_end - padded
    n_rows = (-(-(n * TOP_K) // chunk) + n_exp) * chunk
    n_chunks = n_rows // chunk
    chunk_start = jnp.arange(n_chunks, dtype=I32) * chunk
    n_used = (seg_end[-1:] // chunk).astype(I32)
    live_start = jnp.minimum(chunk_start, jnp.maximum(seg_end[-1] - chunk, 0))
    chunk_e = jnp.sum(seg_end[None, :] <= live_start[:, None], axis=1).astype(I32)
    chunk_e = jnp.minimum(chunk_e, n_exp - 1)
    dest = _dest_rows(eid, rank, seg_start)
    xs = _dispatch(dest, hn, n_rows)
    ys = _experts(xs, chunk_e, n_used, w_in.astype(BF16), w_out.astype(BF16), chunk)
    return _combine(dest, x2, wts, ys, out_gain, final_norm)


def _forget_kernel(x_ref, g_ref, w_ref, b_ref, aux_ref, tri_ref, carry_ref, *, n_heads):
    ts = x_ref.shape[0]

    @pl.when(pl.program_id(1) == 0)
    def _():
        r = lax.broadcasted_iota(I32, (ts, ts), 0)
        c = lax.broadcasted_iota(I32, (ts, ts), 1)
        tri_ref[...] = (c <= r).astype(BF16)
        carry_ref[...] = jnp.zeros_like(carry_ref)

    hn = _rms(x_ref[...], g_ref[...]).astype(BF16)
    f = jnp.dot(hn, w_ref[...], preferred_element_type=F32) + b_ref[...]
    logf = jnp.minimum(f, 0.0) - jnp.log(1.0 + jnp.exp(-jnp.abs(f)))
    l1, l2, l3 = _split3(logf)
    tri = tri_ref[...]
    cum = (jnp.dot(tri, l1, preferred_element_type=F32)
           + jnp.dot(tri, l2, preferred_element_type=F32)
           + jnp.dot(tri, l3, preferred_element_type=F32)) + carry_ref[0:1, :]
    carry_ref[...] = jnp.broadcast_to(cum[ts - 1:ts, :], carry_ref.shape)
    c1, c2, c3 = [c.astype(F32) for c in _split3(cum * (-LOG2E))]
    lane = lax.broadcasted_iota(I32, (ts, LANES), 1)
    for h in range(n_heads):
        col = lambda v: jnp.broadcast_to(v[:, h:h + 1], (ts, LANES))
        aux = jnp.where(lane == 0, col(c1), jnp.where(lane == 1, col(c2),
                        jnp.where(lane == 2, col(c3), 0.0)))
        aux_ref[h] = aux.astype(BF16)


def _forget_aux(x3, gain, w_f, b_f):
    b, s, d = x3.shape
    n_heads = w_f.shape[1]
    assert n_heads <= LANES
    w_p = jnp.pad(w_f, ((0, 0), (0, LANES - n_heads))).astype(BF16)
    b_p = jnp.pad(b_f, (0, LANES - n_heads)).reshape(1, LANES)
    ts = _tile(s, 512)
    return pl.pallas_call(
        functools.partial(_forget_kernel, n_heads=n_heads),
        out_shape=jax.ShapeDtypeStruct((b, n_heads, s, LANES), BF16),
        grid=(b, s // ts),
        in_specs=[
            pl.BlockSpec((None, ts, d), lambda bi, i: (bi, i, 0)),
            pl.BlockSpec((1, d), lambda bi, i: (0, 0)),
            pl.BlockSpec((d, LANES), lambda bi, i: (0, 0)),
            pl.BlockSpec((1, LANES), lambda bi, i: (0, 0)),
        ],
        out_specs=pl.BlockSpec((None, n_heads, ts, LANES), lambda bi, i: (bi, 0, i, 0)),
        scratch_shapes=[pltpu.VMEM((ts, ts), BF16), pltpu.VMEM((SUBLANES, LANES), F32)],
        compiler_params=_cparams(("parallel", "arbitrary"), 32),
        name="forget_cumsum",
    )(x3, gain.reshape(1, d), w_p, b_p)


def _attn_kernel(q_ref, k_ref, aux_ref, vt_ref, o_ref, qp_ref, m_ref, l_ref, acc_ref):
    t, hd = q_ref.shape
    i = pl.program_id(2)
    lane = lax.broadcasted_iota(I32, (t, LANES), 1)
    qp_ref[:, 0:hd] = q_ref[...]
    qp_ref[:, hd:hd + LANES] = jnp.where(lane < 3, 1.0, 0.0).astype(BF16)
    m_ref[...] = jnp.full_like(m_ref, MASK_VALUE)
    l_ref[...] = jnp.zeros_like(l_ref)
    acc_ref[...] = jnp.zeros_like(acc_ref)

    def block(j, masked):
        rows = pl.ds(pl.multiple_of(j * t, t), t)
        kp = jnp.concatenate([k_ref[rows, :], aux_ref[rows, :]], axis=1)
        s_t = _dot_nt(kp, qp_ref[...])
        if masked:
            kr = lax.broadcasted_iota(I32, (t, t), 0)
            qc = lax.broadcasted_iota(I32, (t, t), 1)
            s_t = jnp.where(kr <= qc, s_t, MASK_VALUE)
        m_old = m_ref[...]
        m_new = jnp.maximum(m_old, jnp.max(s_t, axis=0, keepdims=True))
        alpha = jnp.exp2(m_old - m_new)
        p = jnp.exp2(s_t - m_new)
        l_ref[...] = alpha * l_ref[...] + jnp.sum(p, axis=0, keepdims=True)
        pv = jnp.dot(vt_ref[:, rows], p.astype(BF16), preferred_element_type=F32)
        acc_ref[...] = alpha * acc_ref[...] + pv
        m_ref[...] = m_new

    def body(j, c):
        block(j, False)
        return c

    lax.fori_loop(0, i, body, 0)
    block(i, True)
    o_ref[...] = (acc_ref[...] / l_ref[...]).T.astype(o_ref.dtype)


def _attention(qg3, k3, aux, vt, n_heads):
    b, s, _ = k3.shape
    hd = k3.shape[2] // n_heads
    assert hd == LANES
    t = _tile(s, 512)
    return pl.pallas_call(
        _attn_kernel,
        out_shape=jax.ShapeDtypeStruct((b, s, n_heads * hd), BF16),
        grid=(b, n_heads, s // t),
        in_specs=[
            pl.BlockSpec((None, t, hd), lambda bi, h, i: (bi, i, h)),
            pl.BlockSpec((None, s, hd), lambda bi, h, i: (bi, 0, h)),
            pl.BlockSpec((None, None, s, LANES), lambda bi, h, i: (bi, h, 0, 0)),
            pl.BlockSpec((None, hd, s), lambda bi, h, i: (bi, h, 0)),
        ],
        out_specs=pl.BlockSpec((None, t, hd), lambda bi, h, i: (bi, i, h)),
        scratch_shapes=[
            pltpu.VMEM((t, hd + LANES), BF16),
            pltpu.VMEM((1, t), F32),
            pltpu.VMEM((1, t), F32),
            pltpu.VMEM((hd, t), F32),
        ],
        compiler_params=_cparams(("parallel", "parallel", "arbitrary"), 48),
        name="fox_attention",
    )(qg3, k3, aux, vt)


def _gated_out_kernel(o_ref, gate_ref, x_ref, w_ref, out_ref):
    g = o_ref[...].astype(F32) * jax.nn.sigmoid(gate_ref[...].astype(F32))
    out_ref[...] = x_ref[...] + jnp.dot(g.astype(BF16), w_ref[...], preferred_element_type=F32)


def _gated_out(o2, qg, x2, w_o):
    n, d = x2.shape
    hd_all = o2.shape[1]
    tm = _tile(n, 512)
    return pl.pallas_call(
        _gated_out_kernel,
        out_shape=jax.ShapeDtypeStruct((n, d), F32),
        grid=(n // tm,),
        in_specs=[
            pl.BlockSpec((tm, hd_all), lambda i: (i, 0)),
            pl.BlockSpec((tm, hd_all), lambda i: (i, 1)),
            pl.BlockSpec((tm, d), lambda i: (i, 0)),
            pl.BlockSpec((hd_all, d), lambda i: (0, 0)),
        ],
        out_specs=pl.BlockSpec((tm, d), lambda i: (i, 0)),
        compiler_params=_cparams(("parallel",), 48),
        name="gated_out_proj",
    )(o2, qg, x2, w_o)


def kernel(x, a_norm, a_w_in, a_conv_w, a_conv_b, a_w_rec, a_b_rec, a_w_inp, a_b_inp, a_lambda, a_w_out, kv_norm, kv_w, kv_b_forget, b_norm, b_w_qg, b_w_o, m_norm, m_w_group, m_b_group, m_w_router, m_b_router, m_w_in, m_w_out, final_norm):
    b, s, d = x.shape
    n = b * s
    depth = m_norm.shape[0]
    n_a = a_norm.shape[0]
    n_heads = kv_b_forget.shape[0]
    hd_all = b_w_o.shape[1]
    head_dim = hd_all // n_heads
    x2 = x.reshape(n, d)
    k3 = aux = vt = None
    for layer in range(depth):
        if layer < n_a:
            i = layer
            d_rnn = a_w_out.shape[1]
            proj = _norm_proj(x2, a_norm[i], a_w_in[i].astype(BF16), jnp.ones((2 * d_rnn,), F32),
                              BF16, "rglru_in_proj")
            x2 = _rglru(proj, x2.reshape(b, s, d), a_conv_w[i], a_conv_b[i], a_w_rec[i], a_b_rec[i],
                        a_w_inp[i], a_b_inp[i], a_lambda[i], a_w_out[i]).reshape(n, d)
        else:
            j = layer - n_a
            q_scale = jnp.concatenate([jnp.full((hd_all,), head_dim ** -0.5 * LOG2E, F32),
                                       jnp.ones((hd_all,), F32)])
            qg = _norm_proj(x2, b_norm[j], b_w_qg[j].astype(BF16), q_scale, BF16, "fox_qg_proj")
            o = _attention(qg.reshape(b, s, 2 * hd_all), k3, aux, vt, n_heads)
            x2 = _gated_out(o.reshape(n, hd_all), qg, x2, b_w_o[j].astype(BF16))
        last = layer == depth - 1
        x2 = _moe(x2, m_norm[layer], m_w_group[layer], m_b_group[layer], m_w_router[layer],
                  m_b_router[layer], m_w_in[layer], m_w_out[layer], final_norm, last)
        if layer == n_a - 1:
            x3 = x2.reshape(b, s, d)
            k3 = _norm_proj(x2, kv_norm, kv_w[:, :hd_all].astype(BF16), jnp.ones((hd_all,), F32),
                            BF16, "shared_k_proj").reshape(b, s, hd_all)
            vt = _norm_proj_t(x3, kv_norm, kv_w[:, hd_all:2 * hd_all].T.astype(BF16), BF16,
                              "shared_vt_proj")
            aux = _forget_aux(x3, kv_norm, kv_w[:, 2 * hd_all:], kv_b_forget)
    if depth == 0:
        x2 = _rms(x2, final_norm)
    return x2.reshape(b, s, d)
```

```python
import functools
import math

import jax
import jax.numpy as jnp
from jax import lax
from jax.experimental import pallas as pl
from jax.experimental.pallas import tpu as pltpu

F32 = jnp.float32
BF16 = jnp.bfloat16
I32 = jnp.int32

EPS = 1e-6
LRU_C = 8.0
TOP_K = 2
LOG2E = 1.4426950408889634
MASK_VALUE = -1e30

V7X_VMEM_BYTES = 64 * 1024 * 1024
SUBLANES = 8
LANES = 128
MIB = 1024 * 1024


def _cparams(semantics, vmem_mib):
    assert vmem_mib * MIB < V7X_VMEM_BYTES
    return pltpu.CompilerParams(dimension_semantics=semantics, vmem_limit_bytes=vmem_mib * MIB)


def _tile(dim, pref):
    t = min(dim, pref)
    assert dim % t == 0, (dim, pref)
    return t


def _rms(x, gain):
    return x * lax.rsqrt(jnp.mean(x * x, axis=-1, keepdims=True) + EPS) * gain


def _norm_proj_kernel(x_ref, g_ref, w_ref, s_ref, o_ref, hn_ref):
    @pl.when(pl.program_id(1) == 0)
    def _():
        hn_ref[...] = _rms(x_ref[...], g_ref[...]).astype(BF16)

    acc = jnp.dot(hn_ref[...], w_ref[...], preferred_element_type=F32)
    o_ref[...] = (acc * s_ref[...]).astype(o_ref.dtype)


def _norm_proj(x, gain, w, col_scale, out_dtype, name):
    n, d = x.shape
    n_out = w.shape[1]
    tm = _tile(n, 1024)
    tn = _tile(n_out, 1024)
    return pl.pallas_call(
        _norm_proj_kernel,
        out_shape=jax.ShapeDtypeStruct((n, n_out), out_dtype),
        grid=(n // tm, n_out // tn),
        in_specs=[
            pl.BlockSpec((tm, d), lambda i, j: (i, 0)),
            pl.BlockSpec((1, d), lambda i, j: (0, 0)),
            pl.BlockSpec((d, tn), lambda i, j: (0, j)),
            pl.BlockSpec((1, tn), lambda i, j: (0, j)),
        ],
        out_specs=pl.BlockSpec((tm, tn), lambda i, j: (i, j)),
        scratch_shapes=[pltpu.VMEM((tm, d), BF16)],
        compiler_params=_cparams(("parallel", "arbitrary"), 48),
        name=name,
    )(x, gain.reshape(1, d), w, col_scale.reshape(1, n_out))


def _norm_proj_t_kernel(x_ref, g_ref, wt_ref, o_ref, hn_ref):
    @pl.when(pl.program_id(2) == 0)
    def _():
        hn_ref[...] = _rms(x_ref[0], g_ref[...]).astype(BF16)

    acc = lax.dot_general(wt_ref[...], hn_ref[...], (((1,), (1,)), ((), ())),
                          preferred_element_type=F32)
    o_ref[0] = acc.astype(o_ref.dtype)


def _norm_proj_t(x3, gain, w_t, out_dtype, name):
    b, s, d = x3.shape
    n_out = w_t.shape[0]
    tm = _tile(s, 1024)
    tn = _tile(n_out, 1024)
    return pl.pallas_call(
        _norm_proj_t_kernel,
        out_shape=jax.ShapeDtypeStruct((b, n_out, s), out_dtype),
        grid=(b, s // tm, n_out // tn),
        in_specs=[
            pl.BlockSpec((1, tm, d), lambda bi, i, j: (bi, i, 0)),
            pl.BlockSpec((1, d), lambda bi, i, j: (0, 0)),
            pl.BlockSpec((tn, d), lambda bi, i, j: (j, 0)),
        ],
        out_specs=pl.BlockSpec((1, tn, tm), lambda bi, i, j: (bi, j, i)),
        scratch_shapes=[pltpu.VMEM((tm, d), BF16)],
        compiler_params=_cparams(("parallel", "parallel", "arbitrary"), 48),
        name=name,
    )(x3, gain.reshape(1, d), w_t)


SCAN_STRIP = 512


def _rglru_kernel(u_ref, y_ref, x_ref, cw_ref, cb_ref, wr_ref, br_ref, wi_ref, bi_ref,
                  lam_ref, wo_ref, o_ref, ubuf_ref, a_ref, b_ref, h_ref, *, conv_width):
    ts, d = u_ref.shape
    nb, blk, _ = wr_ref.shape
    halo = SUBLANES

    @pl.when(pl.program_id(1) == 0)
    def _():
        ubuf_ref[0:halo, :] = jnp.zeros((halo, d), F32)
        h_ref[...] = jnp.zeros_like(h_ref)

    u = u_ref[...].astype(F32)
    ubuf_ref[halo:halo + ts, :] = u
    uc = u * cw_ref[conv_width - 1:conv_width, :] + cb_ref[...]
    for j in range(conv_width - 1):
        shift = conv_width - 1 - j
        uc = uc + ubuf_ref[halo - shift:halo - shift + ts, :] * cw_ref[j:j + 1, :]
    ubuf_ref[0:halo, :] = u[ts - halo:, :]

    ub = uc.astype(BF16)
    r_parts, i_parts = [], []
    for n in range(nb):
        ubn = ub[:, n * blk:(n + 1) * blk]
        r_parts.append(jnp.dot(ubn, wr_ref[n], preferred_element_type=F32))
        i_parts.append(jnp.dot(ubn, wi_ref[n], preferred_element_type=F32))
    r = jax.nn.sigmoid(jnp.concatenate(r_parts, axis=1) + br_ref[...])
    gi = jax.nn.sigmoid(jnp.concatenate(i_parts, axis=1) + bi_ref[...])
    lam = lam_ref[...]
    sp = jnp.maximum(-lam, 0.0) + jnp.log(1.0 + jnp.exp(-jnp.abs(lam)))
    log_a = (-LRU_C * r) * sp
    a_ref[...] = jnp.exp(log_a)
    th = jnp.tanh(log_a)
    b_ref[...] = jnp.sqrt(-2.0 * th / (1.0 - th)) * (gi * uc)

    row = lax.broadcasted_iota(I32, (SUBLANES, SCAN_STRIP), 0)
    strip = min(SCAN_STRIP, d)
    for c in range(d // strip):
        cols = pl.ds(c * strip, strip)

        def tile_step(t, h):
            rows = pl.ds(pl.multiple_of(t * SUBLANES, SUBLANES), SUBLANES)
            a = a_ref[rows, cols]
            bb = b_ref[rows, cols]
            for sh in (1, 2, 4):
                keep = row[:, :strip] >= sh
                a_prev = jnp.where(keep, pltpu.roll(a, sh, 0), 1.0)
                b_prev = jnp.where(keep, pltpu.roll(bb, sh, 0), 0.0)
                bb = a * b_prev + bb
                a = a * a_prev
            hs = a * h + bb
            b_ref[rows, cols] = hs
            return jnp.broadcast_to(hs[SUBLANES - 1:SUBLANES, :], (SUBLANES, strip))

        h_ref[:, cols] = lax.fori_loop(0, ts // SUBLANES, tile_step, h_ref[:, cols], unroll=2)

    y = y_ref[...].astype(F32)
    gelu = 0.5 * y * (1.0 + jnp.tanh(math.sqrt(2.0 / math.pi) * (y + 0.044715 * (y * y * y))))
    g = (b_ref[...] * gelu).astype(BF16)
    o_ref[...] = x_ref[...] + jnp.dot(g, wo_ref[...], preferred_element_type=F32)


def _rglru(proj, x3, conv_w, conv_b, w_rec, b_rec, w_inp, b_inp, lam, w_out):
    b, s, d = x3.shape
    d_rnn = w_out.shape[0]
    nb, blk, _ = w_rec.shape
    width = conv_w.shape[0]
    assert width - 1 <= SUBLANES
    ts = _tile(s, 256)
    proj3 = proj.reshape(b, s, 2 * d_rnn)
    row = lambda v: v.reshape(1, -1)
    const2 = lambda bi, i: (0, 0)
    const3 = lambda bi, i: (0, 0, 0)
    return pl.pallas_call(
        functools.partial(_rglru_kernel, conv_width=width),
        out_shape=jax.ShapeDtypeStruct((b, s, d), F32),
        grid=(b, s // ts),
        in_specs=[
            pl.BlockSpec((None, ts, d_rnn), lambda bi, i: (bi, i, 0)),
            pl.BlockSpec((None, ts, d_rnn), lambda bi, i: (bi, i, 1)),
            pl.BlockSpec((None, ts, d), lambda bi, i: (bi, i, 0)),
            pl.BlockSpec((width, d_rnn), const2),
            pl.BlockSpec((1, d_rnn), const2),
            pl.BlockSpec((nb, blk, blk), const3),
            pl.BlockSpec((1, d_rnn), const2),
            pl.BlockSpec((nb, blk, blk), const3),
            pl.BlockSpec((1, d_rnn), const2),
            pl.BlockSpec((1, d_rnn), const2),
            pl.BlockSpec((d_rnn, d), const2),
        ],
        out_specs=pl.BlockSpec((None, ts, d), lambda bi, i: (bi, i, 0)),
        scratch_shapes=[
            pltpu.VMEM((SUBLANES + ts, d_rnn), F32),
            pltpu.VMEM((ts, d_rnn), F32),
            pltpu.VMEM((ts, d_rnn), F32),
            pltpu.VMEM((SUBLANES, d_rnn), F32),
        ],
        compiler_params=_cparams(("parallel", "arbitrary"), 56),
        name="rglru",
    )(proj3, proj3, x3, conv_w, row(conv_b), w_rec.astype(BF16), row(b_rec),
      w_inp.astype(BF16), row(b_inp), row(lam), w_out.astype(BF16))


def _split3(v):
    v1 = v.astype(BF16)
    r1 = v - v1.astype(F32)
    v2 = r1.astype(BF16)
    v3 = (r1 - v2.astype(F32)).astype(BF16)
    return v1, v2, v3


def _dot_nt(a, b):
    return lax.dot_general(a, b, (((1,), (1,)), ((), ())), preferred_element_type=F32)


def _first_argmax(v, n):
    idx = lax.broadcasted_iota(I32, v.shape, 0)
    vmax = jnp.max(v, axis=0, keepdims=True)
    amax = jnp.min(jnp.where(v == vmax, idx, n), axis=0, keepdims=True)
    return amax, vmax


def _router_kernel(x_ref, g_ref, w1_ref, w2_ref, w3_ref, bias_ref, hn_ref, eid_ref, wt_ref,
                   rank_ref, cnt_ref, tri_ref, carry_ref, *, n_groups, per_group):
    tm = x_ref.shape[0]
    n_exp = n_groups * per_group

    @pl.when(pl.program_id(0) == 0)
    def _():
        r = lax.broadcasted_iota(I32, (tm, tm), 0)
        c = lax.broadcasted_iota(I32, (tm, tm), 1)
        tri_ref[...] = (r < c).astype(BF16)
        carry_ref[...] = jnp.zeros_like(carry_ref)

    hn = _rms(x_ref[...], g_ref[...])
    hn_ref[...] = hn
    h1, h2, h3 = _split3(hn)
    w1, w2, w3 = w1_ref[...], w2_ref[...], w3_ref[...]
    logits = (_dot_nt(w1, h1) + (_dot_nt(w1, h2) + _dot_nt(w2, h1))
              + (_dot_nt(w1, h3) + _dot_nt(w2, h2) + _dot_nt(w3, h1)))
    logits = logits + bias_ref[:, 0:1]

    gl = logits[0:n_groups, :]
    g_idx, g_max = _first_argmax(gl, n_groups)
    gp_top = 1.0 / jnp.sum(jnp.exp(gl - g_max), axis=0, keepdims=True)
    el = jnp.zeros((per_group, tm), F32)
    for g in range(n_groups):
        lo = n_groups + g * per_group
        el = jnp.where(g_idx == g, logits[lo:lo + per_group, :], el)
    e_max = jnp.max(el, axis=0, keepdims=True)
    ex = jnp.exp(el - e_max)
    ep = ex / jnp.sum(ex, axis=0, keepdims=True)
    e1, p1 = _first_argmax(ep, per_group)
    sub = lax.broadcasted_iota(I32, ep.shape, 0)
    e2, p2 = _first_argmax(jnp.where(sub == e1, -1.0, ep), per_group)
    denom = p1 + p2
    eid1 = g_idx * per_group + e1
    eid2 = g_idx * per_group + e2
    eid_ref[...] = jnp.concatenate([eid1, eid2], axis=0)
    wt_ref[...] = jnp.concatenate([gp_top * p1 / denom, gp_top * p2 / denom], axis=0)

    e_iota = lax.broadcasted_iota(I32, (n_exp, tm), 0)
    hot1 = e_iota == eid1
    hot2 = e_iota == eid2
    chosen = jnp.logical_or(hot1, hot2)
    before = jnp.dot(chosen.astype(BF16), tri_ref[...], preferred_element_type=F32)
    base = (before + carry_ref[:, 0:1]).astype(I32)
    rank1 = jnp.sum(jnp.where(hot1, base, 0), axis=0, keepdims=True)
    rank2 = jnp.sum(jnp.where(hot2, base, 0), axis=0, keepdims=True)
    rank_ref[...] = jnp.concatenate([rank1, rank2], axis=0)
    carry_ref[...] = carry_ref[...] + jnp.sum(chosen.astype(F32), axis=1, keepdims=True)
    cnt_ref[...] = carry_ref[...].astype(I32)


def _router(x2, gain, w_group, b_group, w_router, b_router):
    n, d = x2.shape
    n_groups = w_group.shape[1]
    n_exp = w_router.shape[1]
    per_group = n_exp // n_groups
    assert per_group == SUBLANES and n_groups == SUBLANES
    rows = n_groups + n_exp
    rows_p = -(-rows // LANES) * LANES
    w_t = jnp.concatenate([w_group, w_router], axis=1).T
    w_t = jnp.pad(w_t, ((0, rows_p - rows), (0, 0)))
    w1, w2, w3 = _split3(w_t)
    bias = jnp.pad(jnp.concatenate([b_group, b_router]), (0, rows_p - rows))
    bias = jnp.broadcast_to(bias[:, None], (rows_p, LANES))
    tm = _tile(n, 512)
    const = lambda i: (0, 0)
    return pl.pallas_call(
        functools.partial(_router_kernel, n_groups=n_groups, per_group=per_group),
        out_shape=(
            jax.ShapeDtypeStruct((n, d), F32),
            jax.ShapeDtypeStruct((TOP_K, n), I32),
            jax.ShapeDtypeStruct((TOP_K, n), F32),
            jax.ShapeDtypeStruct((TOP_K, n), I32),
            jax.ShapeDtypeStruct((n_exp, LANES), I32),
        ),
        grid=(n // tm,),
        in_specs=[
            pl.BlockSpec((tm, d), lambda i: (i, 0)),
            pl.BlockSpec((1, d), const),
            pl.BlockSpec((rows_p, d), const),
            pl.BlockSpec((rows_p, d), const),
            pl.BlockSpec((rows_p, d), const),
            pl.BlockSpec((rows_p, LANES), const),
        ],
        out_specs=(
            pl.BlockSpec((tm, d), lambda i: (i, 0)),
            pl.BlockSpec((TOP_K, tm), lambda i: (0, i)),
            pl.BlockSpec((TOP_K, tm), lambda i: (0, i)),
            pl.BlockSpec((TOP_K, tm), lambda i: (0, i)),
            pl.BlockSpec((n_exp, LANES), const),
        ),
        scratch_shapes=[pltpu.VMEM((tm, tm), BF16), pltpu.VMEM((n_exp, LANES), F32)],
        compiler_params=_cparams(("arbitrary",), 40),
        name="moe_router",
    )(x2, gain.reshape(1, d), w1, w2, w3, bias)


def _dest_kernel(eid_ref, rank_ref, start_ref, dest_ref):
    eid = eid_ref[...]
    n_exp = start_ref.shape[0]
    dest = rank_ref[...]
    for k in range(eid.shape[0]):
        hot = lax.broadcasted_iota(I32, (n_exp, eid.shape[1]), 0) == eid[k:k + 1, :]
        off = jnp.sum(jnp.where(hot, start_ref[:, 0:1], 0), axis=0, keepdims=True)
        dest_ref[k:k + 1, :] = dest[k:k + 1, :] + off


def _dest_rows(eid, rank, seg_start):
    k, n = eid.shape
    n_exp = seg_start.shape[0]
    tm = _tile(n, 2048)
    start = jnp.broadcast_to(seg_start[:, None], (n_exp, LANES)).astype(I32)
    return pl.pallas_call(
        _dest_kernel,
        out_shape=jax.ShapeDtypeStruct((k, n), I32),
        grid=(n // tm,),
        in_specs=[
            pl.BlockSpec((k, tm), lambda i: (0, i)),
            pl.BlockSpec((k, tm), lambda i: (0, i)),
            pl.BlockSpec((n_exp, LANES), lambda i: (0, 0)),
        ],
        out_specs=pl.BlockSpec((k, tm), lambda i: (0, i)),
        compiler_params=_cparams(("parallel",), 16),
        name="moe_dest",
    )(eid, rank, start)


def _dispatch_kernel(dest_ref, hn_ref, init_ref, xs_ref, sem):
    del init_ref
    k, tm = dest_ref.shape[1], dest_ref.shape[2]

    def issue(t, c):
        for kk in range(k):
            pltpu.make_async_copy(hn_ref.at[pl.ds(t, 1)],
                                  xs_ref.at[pl.ds(dest_ref[0, kk, t], 1)], sem).start()
        return c

    lax.fori_loop(0, tm, issue, 0, unroll=8)
    for kk in range(k):
        pltpu.make_async_copy(hn_ref, xs_ref.at[pl.ds(0, tm)], sem).wait()


def _dispatch(dest, hn, n_rows):
    k, n = dest.shape
    d = hn.shape[1]
    tm = _tile(n, 512)
    dest3 = dest.reshape(k, n // tm, tm).transpose(1, 0, 2)
    return pl.pallas_call(
        _dispatch_kernel,
        out_shape=jax.ShapeDtypeStruct((n_rows, d), hn.dtype),
        grid=(n // tm,),
        in_specs=[
            pl.BlockSpec((1, k, tm), lambda i: (i, 0, 0), memory_space=pltpu.SMEM),
            pl.BlockSpec((tm, d), lambda i: (i, 0)),
            pl.BlockSpec(memory_space=pl.ANY),
        ],
        out_specs=pl.BlockSpec(memory_space=pl.ANY),
        scratch_shapes=[pltpu.SemaphoreType.DMA(())],
        input_output_aliases={2: 0},
        compiler_params=_cparams(("arbitrary",), 24),
        name="moe_dispatch",
    )(dest3, hn, jnp.zeros((n_rows, d), hn.dtype))


def _expert_kernel(ce_ref, nu_ref, xs_ref, wi_ref, wo_ref, ys_ref, *, d_expert):
    @pl.when(pl.program_id(0) < nu_ref[0])
    def _():
        xb = xs_ref[...].astype(BF16)
        gu = jnp.dot(xb, wi_ref[0], preferred_element_type=F32)
        gate = gu[:, :d_expert]
        act = (gate * jax.nn.sigmoid(gate) * gu[:, d_expert:]).astype(BF16)
        ys_ref[...] = jnp.dot(act, wo_ref[0], preferred_element_type=F32)

    @pl.when(pl.program_id(0) >= nu_ref[0])
    def _():
        ys_ref[...] = jnp.zeros_like(ys_ref)


def _experts(xs, chunk_e, n_used, w_in, w_out, chunk_rows):
    n_rows, d = xs.shape
    d_expert = w_out.shape[1]
    n_chunks = n_rows // chunk_rows
    grid_spec = pltpu.PrefetchScalarGridSpec(
        num_scalar_prefetch=2,
        grid=(n_chunks,),
        in_specs=[
            pl.BlockSpec((chunk_rows, d), lambda i, ce, nu: (i, 0)),
            pl.BlockSpec((1, d, 2 * d_expert), lambda i, ce, nu: (ce[i], 0, 0)),
            pl.BlockSpec((1, d_expert, d), lambda i, ce, nu: (ce[i], 0, 0)),
        ],
        out_specs=pl.BlockSpec((chunk_rows, d), lambda i, ce, nu: (i, 0)),
    )
    return pl.pallas_call(
        functools.partial(_expert_kernel, d_expert=d_expert),
        out_shape=jax.ShapeDtypeStruct((n_rows, d), F32),
        grid_spec=grid_spec,
        compiler_params=_cparams(("arbitrary",), 48),
        name="moe_experts",
    )(chunk_e, n_used, xs, w_in, w_out)


def _combine_kernel(dest_ref, next_ref, x_ref, wt_ref, g_ref, ys_ref, o_ref, buf_ref, sem_ref, *,
                    final_norm, n_blocks):
    k, tm = dest_ref.shape[1], dest_ref.shape[2]
    i = pl.program_id(0)

    def issue_block(d_ref, slot):
        def body(t, c):
            for kk in range(k):
                pltpu.make_async_copy(ys_ref.at[pl.ds(d_ref[0, kk, t], 1)],
                                      buf_ref.at[slot, kk, pl.ds(t, 1)], sem_ref.at[slot]).start()
            return c
        lax.fori_loop(0, tm, body, 0, unroll=8)

    @pl.when(i == 0)
    def _():
        issue_block(dest_ref, 0)

    @pl.when(i + 1 < n_blocks)
    def _():
        issue_block(next_ref, (i + 1) % 2)

    slot = i % 2
    for kk in range(k):
        pltpu.make_async_copy(ys_ref.at[pl.ds(0, tm)], buf_ref.at[slot, kk], sem_ref.at[slot]).wait()

    out = x_ref[...]
    for kk in range(k):
        out = out + wt_ref[:, kk:kk + 1] * buf_ref[slot, kk]
    if final_norm:
        out = _rms(out, g_ref[...])
    o_ref[...] = out


def _combine(dest, x2, wts, ys, gain, final_norm):
    k, n = dest.shape
    d = x2.shape[1]
    tm = _tile(n, 256)
    n_blocks = n // tm
    dest3 = dest.reshape(k, n_blocks, tm).transpose(1, 0, 2)
    return pl.pallas_call(
        functools.partial(_combine_kernel, final_norm=final_norm, n_blocks=n_blocks),
        out_shape=jax.ShapeDtypeStruct((n, d), F32),
        grid=(n_blocks,),
        in_specs=[
            pl.BlockSpec((1, k, tm), lambda i: (i, 0, 0), memory_space=pltpu.SMEM),
            pl.BlockSpec((1, k, tm), lambda i: (jnp.minimum(i + 1, n_blocks - 1), 0, 0),
                         memory_space=pltpu.SMEM),
            pl.BlockSpec((tm, d), lambda i: (i, 0)),
            pl.BlockSpec((tm, k), lambda i: (i, 0)),
            pl.BlockSpec((1, d), lambda i: (0, 0)),
            pl.BlockSpec(memory_space=pl.ANY),
        ],
        out_specs=pl.BlockSpec((tm, d), lambda i: (i, 0)),
        scratch_shapes=[pltpu.VMEM((2, k, tm, d), F32), pltpu.SemaphoreType.DMA((2,))],
        compiler_params=_cparams(("arbitrary",), 40),
        name="moe_combine",
    )(dest3, dest3, x2, wts.T, gain.reshape(1, d), ys)


EXPERT_CHUNK_ROWS = 256


def _moe(x2, norm_gain, w_group, b_group, w_router, b_router, w_in, w_out, out_gain, final_norm):
    n, d = x2.shape
    n_exp = w_router.shape[1]
    chunk = EXPERT_CHUNK_ROWS
    hn, eid, wts, rank, counts = _router(x2, norm_gain, w_group, b_group, w_router, b_router)
    counts = counts[:, 0]
    padded = (counts + chunk - 1) // chunk * chunk
    seg_end = jnp.cumsum(padded)
    seg_start = seg_end - padded
    n_rows = (-(-(n * TOP_K) // chunk) + n_exp) * chunk
    n_chunks = n_rows // chunk
    chunk_start = jnp.arange(n_chunks, dtype=I32) * chunk
    n_used = (seg_end[-1:] // chunk).astype(I32)
    live_start = jnp.minimum(chunk_start, jnp.maximum(seg_end[-1] - chunk, 0))
    chunk_e = jnp.sum(seg_end[None, :] <= live_start[:, None], axis=1).astype(I32)
    chunk_e = jnp.minimum(chunk_e, n_exp - 1)
    dest = _dest_rows(eid, rank, seg_start)
    xs = _dispatch(dest, hn, n_rows)
    ys = _experts(xs, chunk_e, n_used, w_in.astype(BF16), w_out.astype(BF16), chunk)
    return _combine(dest, x2, wts, ys, out_gain, final_norm)


def _forget_kernel(x_ref, g_ref, w_ref, b_ref, aux_ref, tri_ref, carry_ref, *, n_heads):
    ts = x_ref.shape[0]

    @pl.when(pl.program_id(1) == 0)
    def _():
        r = lax.broadcasted_iota(I32, (ts, ts), 0)
        c = lax.broadcasted_iota(I32, (ts, ts), 1)
        tri_ref[...] = (c <= r).astype(BF16)
        carry_ref[...] = jnp.zeros_like(carry_ref)

    hn = _rms(x_ref[...], g_ref[...]).astype(BF16)
    f = jnp.dot(hn, w_ref[...], preferred_element_type=F32) + b_ref[...]
    logf = jnp.minimum(f, 0.0) - jnp.log(1.0 + jnp.exp(-jnp.abs(f)))
    l1, l2, l3 = _split3(logf)
    tri = tri_ref[...]
    cum = (jnp.dot(tri, l1, preferred_element_type=F32)
           + jnp.dot(tri, l2, preferred_element_type=F32)
           + jnp.dot(tri, l3, preferred_element_type=F32)) + carry_ref[0:1, :]
    carry_ref[...] = jnp.broadcast_to(cum[ts - 1:ts, :], carry_ref.shape)
    c1, c2, c3 = [c.astype(F32) for c in _split3(cum * (-LOG2E))]
    lane = lax.broadcasted_iota(I32, (ts, LANES), 1)
    for h in range(n_heads):
        col = lambda v: jnp.broadcast_to(v[:, h:h + 1], (ts, LANES))
        aux = jnp.where(lane == 0, col(c1), jnp.where(lane == 1, col(c2),
                        jnp.where(lane == 2, col(c3), 0.0)))
        aux_ref[h] = aux.astype(BF16)


def _forget_aux(x3, gain, w_f, b_f):
    b, s, d = x3.shape
    n_heads = w_f.shape[1]
    assert n_heads <= LANES
    w_p = jnp.pad(w_f, ((0, 0), (0, LANES - n_heads))).astype(BF16)
    b_p = jnp.pad(b_f, (0, LANES - n_heads)).reshape(1, LANES)
    ts = _tile(s, 512)
    return pl.pallas_call(
        functools.partial(_forget_kernel, n_heads=n_heads),
        out_shape=jax.ShapeDtypeStruct((b, n_heads, s, LANES), BF16),
        grid=(b, s // ts),
        in_specs=[
            pl.BlockSpec((None, ts, d), lambda bi, i: (bi, i, 0)),
            pl.BlockSpec((1, d), lambda bi, i: (0, 0)),
            pl.BlockSpec((d, LANES), lambda bi, i: (0, 0)),
            pl.BlockSpec((1, LANES), lambda bi, i: (0, 0)),
        ],
        out_specs=pl.BlockSpec((None, n_heads, ts, LANES), lambda bi, i: (bi, 0, i, 0)),
        scratch_shapes=[pltpu.VMEM((ts, ts), BF16), pltpu.VMEM((SUBLANES, LANES), F32)],
        compiler_params=_cparams(("parallel", "arbitrary"), 32),
        name="forget_cumsum",
    )(x3, gain.reshape(1, d), w_p, b_p)


ATTN_GROUP_BLOCKS = 4
FINITE_LIMIT = 3.0e38


def _attn_kernel(q_ref, k_ref, aux_ref, vt_ref, o_ref, qp_ref, m_ref, l_ref, acc_ref):
    t, hd = q_ref.shape
    i = pl.program_id(2)
    lane = lax.broadcasted_iota(I32, (t, LANES), 1)
    qp_ref[:, 0:hd] = q_ref[...]
    qp_ref[:, hd:hd + LANES] = jnp.where(lane < 3, 1.0, 0.0).astype(BF16)

    def scores(start, size, masked):
        rows = pl.ds(pl.multiple_of(start, t), size)
        kp = jnp.concatenate([k_ref[rows, :], aux_ref[rows, :]], axis=1)
        s_t = _dot_nt(kp, qp_ref[...])
        if masked:
            kr = lax.broadcasted_iota(I32, (size, t), 0)
            qc = lax.broadcasted_iota(I32, (size, t), 1)
            s_t = jnp.where(kr <= qc, s_t, MASK_VALUE)
        return s_t, rows

    def exact_block(start, masked):
        s_t, rows = scores(start, t, masked)
        m_old = m_ref[...]
        m_new = jnp.maximum(m_old, jnp.max(s_t, axis=0, keepdims=True))
        alpha = jnp.exp2(m_old - m_new)
        p = jnp.exp2(s_t - m_new)
        l_ref[...] = alpha * l_ref[...] + jnp.sum(p, axis=0, keepdims=True)
        pv = jnp.dot(vt_ref[:, rows], p.astype(BF16), preferred_element_type=F32)
        acc_ref[...] = alpha * acc_ref[...] + pv
        m_ref[...] = m_new

    def lagged_group(start, size):
        s_t, rows = scores(start, size, False)
        m_old = m_ref[...]
        p = jnp.exp2(s_t - m_old)
        pv = jnp.dot(vt_ref[:, rows], p.astype(BF16), preferred_element_type=F32)
        m_new = jnp.maximum(m_old, jnp.max(s_t, axis=0, keepdims=True))
        alpha = jnp.exp2(m_old - m_new)
        l_ref[...] = (l_ref[...] + jnp.sum(p, axis=0, keepdims=True)) * alpha
        acc_ref[...] = (acc_ref[...] + pv) * alpha
        m_ref[...] = m_new

    def reset():
        m_ref[...] = jnp.full_like(m_ref, MASK_VALUE)
        l_ref[...] = jnp.zeros_like(l_ref)
        acc_ref[...] = jnp.zeros_like(acc_ref)

    def finish():
        out = acc_ref[...] / l_ref[...]
        o_ref[...] = out.T.astype(o_ref.dtype)
        return out

    reset()
    exact_block(i * t, True)
    group = ATTN_GROUP_BLOCKS * t
    n_groups = i // ATTN_GROUP_BLOCKS

    def group_body(g, c):
        lagged_group(g * group, group)
        return c

    def single_body(j, c):
        lagged_group(j * t, t)
        return c

    lax.fori_loop(0, n_groups, group_body, 0)
    lax.fori_loop(n_groups * ATTN_GROUP_BLOCKS, i, single_body, 0)
    out = finish()
    overflowed = jnp.max(jnp.where(jnp.abs(out) < FINITE_LIMIT, 0.0, 1.0)) > 0.0

    @pl.when(overflowed)
    def _():
        reset()

        def exact_body(j, c):
            exact_block(j * t, False)
            return c

        lax.fori_loop(0, i, exact_body, 0)
        exact_block(i * t, True)
        finish()


def _attention(qg3, k3, aux, vt, n_heads):
    b, s, _ = k3.shape
    hd = k3.shape[2] // n_heads
    assert hd == LANES
    t = _tile(s, 512)
    return pl.pallas_call(
        _attn_kernel,
        out_shape=jax.ShapeDtypeStruct((b, s, n_heads * hd), BF16),
        grid=(b, n_heads, s // t),
        in_specs=[
            pl.BlockSpec((None, t, hd), lambda bi, h, i: (bi, i, h)),
            pl.BlockSpec((None, s, hd), lambda bi, h, i: (bi, 0, h)),
            pl.BlockSpec((None, None, s, LANES), lambda bi, h, i: (bi, h, 0, 0)),
            pl.BlockSpec((None, hd, s), lambda bi, h, i: (bi, h, 0)),
        ],
        out_specs=pl.BlockSpec((None, t, hd), lambda bi, h, i: (bi, i, h)),
        scratch_shapes=[
            pltpu.VMEM((t, hd + LANES), BF16),
            pltpu.VMEM((1, t), F32),
            pltpu.VMEM((1, t), F32),
            pltpu.VMEM((hd, t), F32),
        ],
        compiler_params=_cparams(("parallel", "parallel", "arbitrary"), 48),
        name="fox_attention",
    )(qg3, k3, aux, vt)


def _gated_out_kernel(o_ref, gate_ref, x_ref, w_ref, out_ref):
    g = o_ref[...].astype(F32) * jax.nn.sigmoid(gate_ref[...].astype(F32))
    out_ref[...] = x_ref[...] + jnp.dot(g.astype(BF16), w_ref[...], preferred_element_type=F32)


def _gated_out(o2, qg, x2, w_o):
    n, d = x2.shape
    hd_all = o2.shape[1]
    tm = _tile(n, 512)
    return pl.pallas_call(
        _gated_out_kernel,
        out_shape=jax.ShapeDtypeStruct((n, d), F32),
        grid=(n // tm,),
        in_specs=[
            pl.BlockSpec((tm, hd_all), lambda i: (i, 0)),
            pl.BlockSpec((tm, hd_all), lambda i: (i, 1)),
            pl.BlockSpec((tm, d), lambda i: (i, 0)),
            pl.BlockSpec((hd_all, d), lambda i: (0, 0)),
        ],
        out_specs=pl.BlockSpec((tm, d), lambda i: (i, 0)),
        compiler_params=_cparams(("parallel",), 48),
        name="gated_out_proj",
    )(o2, qg, x2, w_o)


def kernel(x, a_norm, a_w_in, a_conv_w, a_conv_b, a_w_rec, a_b_rec, a_w_inp, a_b_inp, a_lambda, a_w_out, kv_norm, kv_w, kv_b_forget, b_norm, b_w_qg, b_w_o, m_norm, m_w_group, m_b_group, m_w_router, m_b_router, m_w_in, m_w_out, final_norm):
    b, s, d = x.shape
    n = b * s
    depth = m_norm.shape[0]
    n_a = a_norm.shape[0]
    n_heads = kv_b_forget.shape[0]
    hd_all = b_w_o.shape[1]
    head_dim = hd_all // n_heads
    x2 = x.reshape(n, d)
    k3 = aux = vt = None
    for layer in range(depth):
        if layer < n_a:
            i = layer
            d_rnn = a_w_out.shape[1]
            proj = _norm_proj(x2, a_norm[i], a_w_in[i].astype(BF16), jnp.ones((2 * d_rnn,), F32),
                              BF16, "rglru_in_proj")
            x2 = _rglru(proj, x2.reshape(b, s, d), a_conv_w[i], a_conv_b[i], a_w_rec[i], a_b_rec[i],
                        a_w_inp[i], a_b_inp[i], a_lambda[i], a_w_out[i]).reshape(n, d)
        else:
            j = layer - n_a
            q_scale = jnp.concatenate([jnp.full((hd_all,), head_dim ** -0.5 * LOG2E, F32),
                                       jnp.ones((hd_all,), F32)])
            qg = _norm_proj(x2, b_norm[j], b_w_qg[j].astype(BF16), q_scale, BF16, "fox_qg_proj")
            o = _attention(qg.reshape(b, s, 2 * hd_all), k3, aux, vt, n_heads)
            x2 = _gated_out(o.reshape(n, hd_all), qg, x2, b_w_o[j].astype(BF16))
        last = layer == depth - 1
        x2 = _moe(x2, m_norm[layer], m_w_group[layer], m_b_group[layer], m_w_router[layer],
                  m_b_router[layer], m_w_in[layer], m_w_out[layer], final_norm, last)
        if layer == n_a - 1:
            x3 = x2.reshape(b, s, d)
            k3 = _norm_proj(x2, kv_norm, kv_w[:, :hd_all].astype(BF16), jnp.ones((hd_all,), F32),
                            BF16, "shared_k_proj").reshape(b, s, hd_all)
            vt = _norm_proj_t(x3, kv_norm, kv_w[:, hd_all:2 * hd_all].T.astype(BF16), BF16,
                              "shared_vt_proj")
            aux = _forget_aux(x3, kv_norm, kv_w[:, 2 * hd_all:], kv_b_forget)
    if depth == 0:
        x2 = _rms(x2, final_norm)
    return x2.reshape(b, s, d)
```

```python
import functools
import math

import jax
import jax.numpy as jnp
from jax import lax
from jax.experimental import pallas as pl
from jax.experimental.pallas import tpu as pltpu

F32 = jnp.float32
BF16 = jnp.bfloat16
I32 = jnp.int32

EPS = 1e-6
LRU_C = 8.0
TOP_K = 2
LOG2E = 1.4426950408889634
MASK_VALUE = -1e30

V7X_VMEM_BYTES = 64 * 1024 * 1024
SUBLANES = 8
LANES = 128
MIB = 1024 * 1024


def _cparams(semantics, vmem_mib):
    assert vmem_mib * MIB < V7X_VMEM_BYTES
    return pltpu.CompilerParams(dimension_semantics=semantics, vmem_limit_bytes=vmem_mib * MIB)


def _tile(dim, pref):
    t = min(dim, pref)
    assert dim % t == 0, (dim, pref)
    return t


def _rms(x, gain):
    return x * lax.rsqrt(jnp.mean(x * x, axis=-1, keepdims=True) + EPS) * gain


def _norm_proj_kernel(x_ref, g_ref, w_ref, s_ref, o_ref, hn_ref):
    @pl.when(pl.program_id(1) == 0)
    def _():
        hn_ref[...] = _rms(x_ref[...], g_ref[...]).astype(BF16)

    acc = jnp.dot(hn_ref[...], w_ref[...], preferred_element_type=F32)
    o_ref[...] = (acc * s_ref[...]).astype(o_ref.dtype)


def _norm_proj(x, gain, w, col_scale, out_dtype, name):
    n, d = x.shape
    n_out = w.shape[1]
    tm = _tile(n, 1024)
    tn = _tile(n_out, 1024)
    return pl.pallas_call(
        _norm_proj_kernel,
        out_shape=jax.ShapeDtypeStruct((n, n_out), out_dtype),
        grid=(n // tm, n_out // tn),
        in_specs=[
            pl.BlockSpec((tm, d), lambda i, j: (i, 0)),
            pl.BlockSpec((1, d), lambda i, j: (0, 0)),
            pl.BlockSpec((d, tn), lambda i, j: (0, j)),
            pl.BlockSpec((1, tn), lambda i, j: (0, j)),
        ],
        out_specs=pl.BlockSpec((tm, tn), lambda i, j: (i, j)),
        scratch_shapes=[pltpu.VMEM((tm, d), BF16)],
        compiler_params=_cparams(("parallel", "arbitrary"), 48),
        name=name,
    )(x, gain.reshape(1, d), w, col_scale.reshape(1, n_out))


def _norm_proj_t_kernel(x_ref, g_ref, wt_ref, o_ref, hn_ref):
    @pl.when(pl.program_id(2) == 0)
    def _():
        hn_ref[...] = _rms(x_ref[0], g_ref[...]).astype(BF16)

    acc = lax.dot_general(wt_ref[...], hn_ref[...], (((1,), (1,)), ((), ())),
                          preferred_element_type=F32)
    o_ref[0] = acc.astype(o_ref.dtype)


def _norm_proj_t(x3, gain, w_t, out_dtype, name):
    b, s, d = x3.shape
    n_out = w_t.shape[0]
    tm = _tile(s, 1024)
    tn = _tile(n_out, 1024)
    return pl.pallas_call(
        _norm_proj_t_kernel,
        out_shape=jax.ShapeDtypeStruct((b, n_out, s), out_dtype),
        grid=(b, s // tm, n_out // tn),
        in_specs=[
            pl.BlockSpec((1, tm, d), lambda bi, i, j: (bi, i, 0)),
            pl.BlockSpec((1, d), lambda bi, i, j: (0, 0)),
            pl.BlockSpec((tn, d), lambda bi, i, j: (j, 0)),
        ],
        out_specs=pl.BlockSpec((1, tn, tm), lambda bi, i, j: (bi, j, i)),
        scratch_shapes=[pltpu.VMEM((tm, d), BF16)],
        compiler_params=_cparams(("parallel", "parallel", "arbitrary"), 48),
        name=name,
    )(x3, gain.reshape(1, d), w_t)


SCAN_STRIP = 512


def _rglru_kernel(u_ref, y_ref, x_ref, cw_ref, cb_ref, wr_ref, br_ref, wi_ref, bi_ref,
                  lam_ref, wo_ref, o_ref, ubuf_ref, a_ref, b_ref, h_ref, *, conv_width):
    ts, d = u_ref.shape
    nb, blk, _ = wr_ref.shape
    halo = SUBLANES

    @pl.when(pl.program_id(1) == 0)
    def _():
        ubuf_ref[0:halo, :] = jnp.zeros((halo, d), F32)
        h_ref[...] = jnp.zeros_like(h_ref)

    u = u_ref[...].astype(F32)
    ubuf_ref[halo:halo + ts, :] = u
    uc = u * cw_ref[conv_width - 1:conv_width, :] + cb_ref[...]
    for j in range(conv_width - 1):
        shift = conv_width - 1 - j
        uc = uc + ubuf_ref[halo - shift:halo - shift + ts, :] * cw_ref[j:j + 1, :]
    ubuf_ref[0:halo, :] = u[ts - halo:, :]

    ub = uc.astype(BF16)
    r_parts, i_parts = [], []
    for n in range(nb):
        ubn = ub[:, n * blk:(n + 1) * blk]
        r_parts.append(jnp.dot(ubn, wr_ref[n], preferred_element_type=F32))
        i_parts.append(jnp.dot(ubn, wi_ref[n], preferred_element_type=F32))
    r = jax.nn.sigmoid(jnp.concatenate(r_parts, axis=1) + br_ref[...])
    gi = jax.nn.sigmoid(jnp.concatenate(i_parts, axis=1) + bi_ref[...])
    lam = lam_ref[...]
    sp = jnp.maximum(-lam, 0.0) + jnp.log(1.0 + jnp.exp(-jnp.abs(lam)))
    log_a = (-LRU_C * r) * sp
    a_ref[...] = jnp.exp(log_a)
    th = jnp.tanh(log_a)
    b_ref[...] = jnp.sqrt(-2.0 * th / (1.0 - th)) * (gi * uc)

    row = lax.broadcasted_iota(I32, (SUBLANES, SCAN_STRIP), 0)
    strip = min(SCAN_STRIP, d)
    for c in range(d // strip):
        cols = pl.ds(c * strip, strip)

        def tile_step(t, h):
            rows = pl.ds(pl.multiple_of(t * SUBLANES, SUBLANES), SUBLANES)
            a = a_ref[rows, cols]
            bb = b_ref[rows, cols]
            for sh in (1, 2, 4):
                keep = row[:, :strip] >= sh
                a_prev = jnp.where(keep, pltpu.roll(a, sh, 0), 1.0)
                b_prev = jnp.where(keep, pltpu.roll(bb, sh, 0), 0.0)
                bb = a * b_prev + bb
                a = a * a_prev
            hs = a * h + bb
            b_ref[rows, cols] = hs
            return jnp.broadcast_to(hs[SUBLANES - 1:SUBLANES, :], (SUBLANES, strip))

        h_ref[:, cols] = lax.fori_loop(0, ts // SUBLANES, tile_step, h_ref[:, cols], unroll=2)

    y = y_ref[...].astype(F32)
    gelu = 0.5 * y * (1.0 + jnp.tanh(math.sqrt(2.0 / math.pi) * (y + 0.044715 * (y * y * y))))
    g = (b_ref[...] * gelu).astype(BF16)
    o_ref[...] = x_ref[...] + jnp.dot(g, wo_ref[...], preferred_element_type=F32)


def _rglru(proj, x3, conv_w, conv_b, w_rec, b_rec, w_inp, b_inp, lam, w_out):
    b, s, d = x3.shape
    d_rnn = w_out.shape[0]
    nb, blk, _ = w_rec.shape
    width = conv_w.shape[0]
    assert width - 1 <= SUBLANES
    ts = _tile(s, 256)
    proj3 = proj.reshape(b, s, 2 * d_rnn)
    row = lambda v: v.reshape(1, -1)
    const2 = lambda bi, i: (0, 0)
    const3 = lambda bi, i: (0, 0, 0)
    return pl.pallas_call(
        functools.partial(_rglru_kernel, conv_width=width),
        out_shape=jax.ShapeDtypeStruct((b, s, d), F32),
        grid=(b, s // ts),
        in_specs=[
            pl.BlockSpec((None, ts, d_rnn), lambda bi, i: (bi, i, 0)),
            pl.BlockSpec((None, ts, d_rnn), lambda bi, i: (bi, i, 1)),
            pl.BlockSpec((None, ts, d), lambda bi, i: (bi, i, 0)),
            pl.BlockSpec((width, d_rnn), const2),
            pl.BlockSpec((1, d_rnn), const2),
            pl.BlockSpec((nb, blk, blk), const3),
            pl.BlockSpec((1, d_rnn), const2),
            pl.BlockSpec((nb, blk, blk), const3),
            pl.BlockSpec((1, d_rnn), const2),
            pl.BlockSpec((1, d_rnn), const2),
            pl.BlockSpec((d_rnn, d), const2),
        ],
        out_specs=pl.BlockSpec((None, ts, d), lambda bi, i: (bi, i, 0)),
        scratch_shapes=[
            pltpu.VMEM((SUBLANES + ts, d_rnn), F32),
            pltpu.VMEM((ts, d_rnn), F32),
            pltpu.VMEM((ts, d_rnn), F32),
            pltpu.VMEM((SUBLANES, d_rnn), F32),
        ],
        compiler_params=_cparams(("parallel", "arbitrary"), 56),
        name="rglru",
    )(proj3, proj3, x3, conv_w, row(conv_b), w_rec.astype(BF16), row(b_rec),
      w_inp.astype(BF16), row(b_inp), row(lam), w_out.astype(BF16))


def _split3(v):
    v1 = v.astype(BF16)
    r1 = v - v1.astype(F32)
    v2 = r1.astype(BF16)
    v3 = (r1 - v2.astype(F32)).astype(BF16)
    return v1, v2, v3


def _dot_nt(a, b):
    return lax.dot_general(a, b, (((1,), (1,)), ((), ())), preferred_element_type=F32)


def _first_argmax(v, n):
    idx = lax.broadcasted_iota(I32, v.shape, 0)
    vmax = jnp.max(v, axis=0, keepdims=True)
    amax = jnp.min(jnp.where(v == vmax, idx, n), axis=0, keepdims=True)
    return amax, vmax


def _router_kernel(x_ref, g_ref, w1_ref, w2_ref, w3_ref, bias_ref, hn_ref, eid_ref, wt_ref,
                   rank_ref, cnt_ref, tri_ref, carry_ref, *, n_groups, per_group):
    tm = x_ref.shape[0]
    n_exp = n_groups * per_group

    @pl.when(pl.program_id(0) == 0)
    def _():
        r = lax.broadcasted_iota(I32, (tm, tm), 0)
        c = lax.broadcasted_iota(I32, (tm, tm), 1)
        tri_ref[...] = (r < c).astype(BF16)
        carry_ref[...] = jnp.zeros_like(carry_ref)

    hn = _rms(x_ref[...], g_ref[...])
    hn_ref[...] = hn
    h1, h2, h3 = _split3(hn)
    w1, w2, w3 = w1_ref[...], w2_ref[...], w3_ref[...]
    logits = (_dot_nt(w1, h1) + (_dot_nt(w1, h2) + _dot_nt(w2, h1))
              + (_dot_nt(w1, h3) + _dot_nt(w2, h2) + _dot_nt(w3, h1)))
    logits = logits + bias_ref[:, 0:1]

    gl = logits[0:n_groups, :]
    g_idx, g_max = _first_argmax(gl, n_groups)
    gp_top = 1.0 / jnp.sum(jnp.exp(gl - g_max), axis=0, keepdims=True)
    el = jnp.zeros((per_group, tm), F32)
    for g in range(n_groups):
        lo = n_groups + g * per_group
        el = jnp.where(g_idx == g, logits[lo:lo + per_group, :], el)
    e_max = jnp.max(el, axis=0, keepdims=True)
    ex = jnp.exp(el - e_max)
    ep = ex / jnp.sum(ex, axis=0, keepdims=True)
    e1, p1 = _first_argmax(ep, per_group)
    sub = lax.broadcasted_iota(I32, ep.shape, 0)
    e2, p2 = _first_argmax(jnp.where(sub == e1, -1.0, ep), per_group)
    denom = p1 + p2
    eid1 = g_idx * per_group + e1
    eid2 = g_idx * per_group + e2
    eid_ref[...] = jnp.concatenate([eid1, eid2], axis=0)
    wt_ref[...] = jnp.concatenate([gp_top * p1 / denom, gp_top * p2 / denom], axis=0)

    e_iota = lax.broadcasted_iota(I32, (n_exp, tm), 0)
    hot1 = e_iota == eid1
    hot2 = e_iota == eid2
    chosen = jnp.logical_or(hot1, hot2)
    before = jnp.dot(chosen.astype(BF16), tri_ref[...], preferred_element_type=F32)
    base = (before + carry_ref[:, 0:1]).astype(I32)
    rank1 = jnp.sum(jnp.where(hot1, base, 0), axis=0, keepdims=True)
    rank2 = jnp.sum(jnp.where(hot2, base, 0), axis=0, keepdims=True)
    rank_ref[...] = jnp.concatenate([rank1, rank2], axis=0)
    carry_ref[...] = carry_ref[...] + jnp.sum(chosen.astype(F32), axis=1, keepdims=True)
    cnt_ref[...] = carry_ref[...].astype(I32)


def _router(x2, gain, w_group, b_group, w_router, b_router):
    n, d = x2.shape
    n_groups = w_group.shape[1]
    n_exp = w_router.shape[1]
    per_group = n_exp // n_groups
    assert per_group == SUBLANES and n_groups == SUBLANES
    rows = n_groups + n_exp
    rows_p = -(-rows // LANES) * LANES
    w_t = jnp.concatenate([w_group, w_router], axis=1).T
    w_t = jnp.pad(w_t, ((0, rows_p - rows), (0, 0)))
    w1, w2, w3 = _split3(w_t)
    bias = jnp.pad(jnp.concatenate([b_group, b_router]), (0, rows_p - rows))
    bias = jnp.broadcast_to(bias[:, None], (rows_p, LANES))
    tm = _tile(n, 512)
    const = lambda i: (0, 0)
    return pl.pallas_call(
        functools.partial(_router_kernel, n_groups=n_groups, per_group=per_group),
        out_shape=(
            jax.ShapeDtypeStruct((n, d), F32),
            jax.ShapeDtypeStruct((TOP_K, n), I32),
            jax.ShapeDtypeStruct((TOP_K, n), F32),
            jax.ShapeDtypeStruct((TOP_K, n), I32),
            jax.ShapeDtypeStruct((n_exp, LANES), I32),
        ),
        grid=(n // tm,),
        in_specs=[
            pl.BlockSpec((tm, d), lambda i: (i, 0)),
            pl.BlockSpec((1, d), const),
            pl.BlockSpec((rows_p, d), const),
            pl.BlockSpec((rows_p, d), const),
            pl.BlockSpec((rows_p, d), const),
            pl.BlockSpec((rows_p, LANES), const),
        ],
        out_specs=(
            pl.BlockSpec((tm, d), lambda i: (i, 0)),
            pl.BlockSpec((TOP_K, tm), lambda i: (0, i)),
            pl.BlockSpec((TOP_K, tm), lambda i: (0, i)),
            pl.BlockSpec((TOP_K, tm), lambda i: (0, i)),
            pl.BlockSpec((n_exp, LANES), const),
        ),
        scratch_shapes=[pltpu.VMEM((tm, tm), BF16), pltpu.VMEM((n_exp, LANES), F32)],
        compiler_params=_cparams(("arbitrary",), 40),
        name="moe_router",
    )(x2, gain.reshape(1, d), w1, w2, w3, bias)


def _dest_kernel(eid_ref, rank_ref, start_ref, dest_ref):
    eid = eid_ref[...]
    n_exp = start_ref.shape[0]
    dest = rank_ref[...]
    for k in range(eid.shape[0]):
        hot = lax.broadcasted_iota(I32, (n_exp, eid.shape[1]), 0) == eid[k:k + 1, :]
        off = jnp.sum(jnp.where(hot, start_ref[:, 0:1], 0), axis=0, keepdims=True)
        dest_ref[k:k + 1, :] = dest[k:k + 1, :] + off


def _dest_rows(eid, rank, seg_start):
    k, n = eid.shape
    n_exp = seg_start.shape[0]
    tm = _tile(n, 2048)
    start = jnp.broadcast_to(seg_start[:, None], (n_exp, LANES)).astype(I32)
    return pl.pallas_call(
        _dest_kernel,
        out_shape=jax.ShapeDtypeStruct((k, n), I32),
        grid=(n // tm,),
        in_specs=[
            pl.BlockSpec((k, tm), lambda i: (0, i)),
            pl.BlockSpec((k, tm), lambda i: (0, i)),
            pl.BlockSpec((n_exp, LANES), lambda i: (0, 0)),
        ],
        out_specs=pl.BlockSpec((k, tm), lambda i: (0, i)),
        compiler_params=_cparams(("parallel",), 16),
        name="moe_dest",
    )(eid, rank, start)


def _dispatch_kernel(nv_ref, dest_ref, hn_ref, xs_ref, zero_ref, sem, zsem, *, chunk_rows):
    k, tm = dest_ref.shape[1], dest_ref.shape[2]
    n_chunks = nv_ref.shape[0]

    @pl.when(pl.program_id(0) == 0)
    def _():
        zero_ref[...] = jnp.zeros_like(zero_ref)

        def zero_copy(c):
            rows = pl.ds(pl.multiple_of(c * chunk_rows, chunk_rows), chunk_rows)
            return pltpu.make_async_copy(zero_ref, xs_ref.at[rows], zsem)

        def start(c, carry):
            @pl.when(nv_ref[c] < chunk_rows)
            def _():
                zero_copy(c).start()
            return carry

        def finish(c, carry):
            @pl.when(nv_ref[c] < chunk_rows)
            def _():
                zero_copy(c).wait()
            return carry

        lax.fori_loop(0, n_chunks, start, 0)
        lax.fori_loop(0, n_chunks, finish, 0)


    def issue(t, c):
        for kk in range(k):
            pltpu.make_async_copy(hn_ref.at[pl.ds(t, 1)],
                                  xs_ref.at[pl.ds(dest_ref[0, kk, t], 1)], sem).start()
        return c

    lax.fori_loop(0, tm, issue, 0, unroll=8)
    for kk in range(k):
        pltpu.make_async_copy(hn_ref, xs_ref.at[pl.ds(0, tm)], sem).wait()


def _dispatch(dest, hn, n_valid, n_rows, chunk_rows):
    k, n = dest.shape
    d = hn.shape[1]
    tm = _tile(n, 512)
    dest3 = dest.reshape(k, n // tm, tm).transpose(1, 0, 2)
    grid_spec = pltpu.PrefetchScalarGridSpec(
        num_scalar_prefetch=1,
        grid=(n // tm,),
        in_specs=[
            pl.BlockSpec((1, k, tm), lambda i, nv: (i, 0, 0), memory_space=pltpu.SMEM),
            pl.BlockSpec((tm, d), lambda i, nv: (i, 0)),
        ],
        out_specs=pl.BlockSpec(memory_space=pl.ANY),
        scratch_shapes=[pltpu.VMEM((chunk_rows, d), hn.dtype), pltpu.SemaphoreType.DMA(()),
                        pltpu.SemaphoreType.DMA(())],
    )
    return pl.pallas_call(
        functools.partial(_dispatch_kernel, chunk_rows=chunk_rows),
        out_shape=jax.ShapeDtypeStruct((n_rows, d), hn.dtype),
        grid_spec=grid_spec,
        compiler_params=_cparams(("arbitrary",), 24),
        name="moe_dispatch",
    )(n_valid, dest3, hn)


def _expert_kernel(ce_ref, nu_ref, fresh_ref, xs_ref, wi_ref, wo_ref, ys_ref,
                   wib_ref, wob_ref, *, d_expert):
    i = pl.program_id(0)

    @pl.when(i < nu_ref[0])
    def _():
        @pl.when(fresh_ref[i] == 1)
        def _():
            wib_ref[...] = wi_ref[...].astype(BF16)
            wob_ref[...] = wo_ref[...].astype(BF16)

        gu = jnp.dot(xs_ref[...].astype(BF16), wib_ref[...], preferred_element_type=F32)
        gate = gu[:, :d_expert]
        act = (gate * jax.nn.sigmoid(gate) * gu[:, d_expert:]).astype(BF16)
        ys_ref[...] = jnp.dot(act, wob_ref[...], preferred_element_type=F32)

    @pl.when(i >= nu_ref[0])
    def _():
        ys_ref[...] = jnp.zeros_like(ys_ref)


def _experts(xs, chunk_e, n_used, fresh, w_in_all, w_out_all, layer, chunk_rows):
    n_rows, d = xs.shape
    d_expert = w_out_all.shape[2]
    n_chunks = n_rows // chunk_rows
    xs_map = lambda i, ce, nu, fr: (jnp.minimum(i, jnp.maximum(nu[0] - 1, 0)), 0)
    w_map = lambda i, ce, nu, fr: (layer, ce[i], 0, 0)
    grid_spec = pltpu.PrefetchScalarGridSpec(
        num_scalar_prefetch=3,
        grid=(n_chunks,),
        in_specs=[
            pl.BlockSpec((chunk_rows, d), xs_map),
            pl.BlockSpec((None, None, d, 2 * d_expert), w_map),
            pl.BlockSpec((None, None, d_expert, d), w_map),
        ],
        out_specs=pl.BlockSpec((chunk_rows, d), lambda i, ce, nu, fr: (i, 0)),
        scratch_shapes=[pltpu.VMEM((d, 2 * d_expert), BF16), pltpu.VMEM((d_expert, d), BF16)],
    )
    return pl.pallas_call(
        functools.partial(_expert_kernel, d_expert=d_expert),
        out_shape=jax.ShapeDtypeStruct((n_rows, d), F32),
        grid_spec=grid_spec,
        compiler_params=_cparams(("arbitrary",), 56),
        name="moe_experts",
    )(chunk_e, n_used, fresh, xs, w_in_all, w_out_all)


def _combine_kernel(dest_ref, next_ref, x_ref, wt_ref, g_ref, ys_ref, o_ref, buf_ref, sem_ref, *,
                    final_norm, n_blocks):
    k, tm = dest_ref.shape[1], dest_ref.shape[2]
    i = pl.program_id(0)

    def issue_block(d_ref, slot):
        def body(t, c):
            for kk in range(k):
                pltpu.make_async_copy(ys_ref.at[pl.ds(d_ref[0, kk, t], 1)],
                                      buf_ref.at[slot, kk, pl.ds(t, 1)], sem_ref.at[slot]).start()
            return c
        lax.fori_loop(0, tm, body, 0, unroll=8)

    @pl.when(i == 0)
    def _():
        issue_block(dest_ref, 0)

    @pl.when(i + 1 < n_blocks)
    def _():
        issue_block(next_ref, (i + 1) % 2)

    slot = i % 2
    for kk in range(k):
        pltpu.make_async_copy(ys_ref.at[pl.ds(0, tm)], buf_ref.at[slot, kk], sem_ref.at[slot]).wait()

    out = x_ref[...]
    for kk in range(k):
        out = out + wt_ref[:, kk:kk + 1] * buf_ref[slot, kk]
    if final_norm:
        out = _rms(out, g_ref[...])
    o_ref[...] = out


def _combine(dest, x2, wts, ys, gain, final_norm):
    k, n = dest.shape
    d = x2.shape[1]
    tm = _tile(n, 256)
    n_blocks = n // tm
    dest3 = dest.reshape(k, n_blocks, tm).transpose(1, 0, 2)
    return pl.pallas_call(
        functools.partial(_combine_kernel, final_norm=final_norm, n_blocks=n_blocks),
        out_shape=jax.ShapeDtypeStruct((n, d), F32),
        grid=(n_blocks,),
        in_specs=[
            pl.BlockSpec((1, k, tm), lambda i: (i, 0, 0), memory_space=pltpu.SMEM),
            pl.BlockSpec((1, k, tm), lambda i: (jnp.minimum(i + 1, n_blocks - 1), 0, 0),
                         memory_space=pltpu.SMEM),
            pl.BlockSpec((tm, d), lambda i: (i, 0)),
            pl.BlockSpec((tm, k), lambda i: (i, 0)),
            pl.BlockSpec((1, d), lambda i: (0, 0)),
            pl.BlockSpec(memory_space=pl.ANY),
        ],
        out_specs=pl.BlockSpec((tm, d), lambda i: (i, 0)),
        scratch_shapes=[pltpu.VMEM((2, k, tm, d), F32), pltpu.SemaphoreType.DMA((2,))],
        compiler_params=_cparams(("arbitrary",), 40),
        name="moe_combine",
    )(dest3, dest3, x2, wts.T, gain.reshape(1, d), ys)


EXPERT_CHUNK_ROWS = 256


def _moe(x2, norm_gain, w_group, b_group, w_router, b_router, w_in_all, w_out_all, layer, out_gain,
         final_norm):
    n, d = x2.shape
    n_exp = w_router.shape[1]
    chunk = EXPERT_CHUNK_ROWS
    hn, eid, wts, rank, counts = _router(x2, norm_gain, w_group, b_group, w_router, b_router)
    counts = counts[:, 0]
    padded = (counts + chunk - 1) // chunk * chunk
    seg_end = jnp.cumsum(padded)
    seg_start = seg_end - padded
    n_rows = (-(-(n * TOP_K) // chunk) + n_exp) * chunk
    n_chunks = n_rows // chunk
    chunk_start = jnp.arange(n_chunks, dtype=I32) * chunk
    n_used = (seg_end[-1:] // chunk).astype(I32)
    live_start = jnp.minimum(chunk_start, jnp.maximum(seg_end[-1] - chunk, 0))
    chunk_e = jnp.sum(seg_end[None, :] <= live_start[:, None], axis=1).astype(I32)
    chunk_e = jnp.minimum(chunk_e, n_exp - 1)
    n_valid = jnp.clip(counts[chunk_e] - (chunk_start - seg_start[chunk_e]), 0, chunk).astype(I32)
    fresh = jnp.concatenate([jnp.ones((1,), I32), (chunk_e[1:] != chunk_e[:-1]).astype(I32)])
    dest = _dest_rows(eid, rank, seg_start)
    xs = _dispatch(dest, hn, n_valid, n_rows, chunk)
    ys = _experts(xs, chunk_e, n_used, fresh, w_in_all, w_out_all, layer, chunk)
    return _combine(dest, x2, wts, ys, out_gain, final_norm)


def _forget_kernel(x_ref, g_ref, w_ref, b_ref, aux_ref, tri_ref, carry_ref, *, n_heads):
    ts = x_ref.shape[0]

    @pl.when(pl.program_id(1) == 0)
    def _():
        r = lax.broadcasted_iota(I32, (ts, ts), 0)
        c = lax.broadcasted_iota(I32, (ts, ts), 1)
        tri_ref[...] = (c <= r).astype(BF16)
        carry_ref[...] = jnp.zeros_like(carry_ref)

    hn = _rms(x_ref[...], g_ref[...]).astype(BF16)
    f = jnp.dot(hn, w_ref[...], preferred_element_type=F32) + b_ref[...]
    logf = jnp.minimum(f, 0.0) - jnp.log(1.0 + jnp.exp(-jnp.abs(f)))
    l1, l2, l3 = _split3(logf)
    tri = tri_ref[...]
    cum = (jnp.dot(tri, l1, preferred_element_type=F32)
           + jnp.dot(tri, l2, preferred_element_type=F32)
           + jnp.dot(tri, l3, preferred_element_type=F32)) + carry_ref[0:1, :]
    carry_ref[...] = jnp.broadcast_to(cum[ts - 1:ts, :], carry_ref.shape)
    c1, c2, c3 = [c.astype(F32) for c in _split3(cum * (-LOG2E))]
    lane = lax.broadcasted_iota(I32, (ts, LANES), 1)
    for h in range(n_heads):
        col = lambda v: jnp.broadcast_to(v[:, h:h + 1], (ts, LANES))
        aux = jnp.where(lane == 0, col(c1), jnp.where(lane == 1, col(c2),
                        jnp.where(lane == 2, col(c3), 0.0)))
        aux_ref[h] = aux.astype(BF16)


def _forget_aux(x3, gain, w_f, b_f):
    b, s, d = x3.shape
    n_heads = w_f.shape[1]
    assert n_heads <= LANES
    w_p = jnp.pad(w_f, ((0, 0), (0, LANES - n_heads))).astype(BF16)
    b_p = jnp.pad(b_f, (0, LANES - n_heads)).reshape(1, LANES)
    ts = _tile(s, 512)
    return pl.pallas_call(
        functools.partial(_forget_kernel, n_heads=n_heads),
        out_shape=jax.ShapeDtypeStruct((b, n_heads, s, LANES), BF16),
        grid=(b, s // ts),
        in_specs=[
            pl.BlockSpec((None, ts, d), lambda bi, i: (bi, i, 0)),
            pl.BlockSpec((1, d), lambda bi, i: (0, 0)),
            pl.BlockSpec((d, LANES), lambda bi, i: (0, 0)),
            pl.BlockSpec((1, LANES), lambda bi, i: (0, 0)),
        ],
        out_specs=pl.BlockSpec((None, n_heads, ts, LANES), lambda bi, i: (bi, 0, i, 0)),
        scratch_shapes=[pltpu.VMEM((ts, ts), BF16), pltpu.VMEM((SUBLANES, LANES), F32)],
        compiler_params=_cparams(("parallel", "arbitrary"), 32),
        name="forget_cumsum",
    )(x3, gain.reshape(1, d), w_p, b_p)


ATTN_GROUP_BLOCKS = 4
FINITE_LIMIT = 3.0e38


def _attn_kernel(q_ref, k_ref, aux_ref, vt_ref, o_ref, qp_ref, m_ref, l_ref, acc_ref):
    t, hd = q_ref.shape
    i = pl.program_id(2)
    lane = lax.broadcasted_iota(I32, (t, LANES), 1)
    qp_ref[:, 0:hd] = q_ref[...]
    qp_ref[:, hd:hd + LANES] = jnp.where(lane < 3, 1.0, 0.0).astype(BF16)

    def scores(start, size, masked):
        rows = pl.ds(pl.multiple_of(start, t), size)
        kp = jnp.concatenate([k_ref[rows, :], aux_ref[rows, :]], axis=1)
        s_t = _dot_nt(kp, qp_ref[...])
        if masked:
            kr = lax.broadcasted_iota(I32, (size, t), 0)
            qc = lax.broadcasted_iota(I32, (size, t), 1)
            s_t = jnp.where(kr <= qc, s_t, MASK_VALUE)
        return s_t, rows

    def exact_block(start, masked):
        s_t, rows = scores(start, t, masked)
        m_old = m_ref[...]
        m_new = jnp.maximum(m_old, jnp.max(s_t, axis=0, keepdims=True))
        alpha = jnp.exp2(m_old - m_new)
        p = jnp.exp2(s_t - m_new)
        l_ref[...] = alpha * l_ref[...] + jnp.sum(p, axis=0, keepdims=True)
        pv = jnp.dot(vt_ref[:, rows], p.astype(BF16), preferred_element_type=F32)
        acc_ref[...] = alpha * acc_ref[...] + pv
        m_ref[...] = m_new

    def lagged_group(start, size):
        s_t, rows = scores(start, size, False)
        m_old = m_ref[...]
        p = jnp.exp2(s_t - m_old)
        pv = jnp.dot(vt_ref[:, rows], p.astype(BF16), preferred_element_type=F32)
        m_new = jnp.maximum(m_old, jnp.max(s_t, axis=0, keepdims=True))
        alpha = jnp.exp2(m_old - m_new)
        l_ref[...] = (l_ref[...] + jnp.sum(p, axis=0, keepdims=True)) * alpha
        acc_ref[...] = (acc_ref[...] + pv) * alpha
        m_ref[...] = m_new

    def reset():
        m_ref[...] = jnp.full_like(m_ref, MASK_VALUE)
        l_ref[...] = jnp.zeros_like(l_ref)
        acc_ref[...] = jnp.zeros_like(acc_ref)

    def finish():
        out = acc_ref[...] / l_ref[...]
        o_ref[...] = out.T.astype(o_ref.dtype)
        return out

    reset()
    exact_block(i * t, True)
    group = ATTN_GROUP_BLOCKS * t
    n_groups = i // ATTN_GROUP_BLOCKS

    def group_body(g, c):
        lagged_group(g * group, group)
        return c

    lax.fori_loop(0, n_groups, group_body, 0)
    rest = n_groups * ATTN_GROUP_BLOCKS
    pair = i & 2

    @pl.when(pair != 0)
    def _():
        lagged_group(rest * t, 2 * t)

    @pl.when((i & 1) != 0)
    def _():
        lagged_group((rest + pair) * t, t)

    out = finish()
    overflowed = jnp.max(jnp.where(jnp.abs(out) < FINITE_LIMIT, 0.0, 1.0)) > 0.0

    @pl.when(overflowed)
    def _():
        reset()

        def exact_body(j, c):
            exact_block(j * t, False)
            return c

        lax.fori_loop(0, i, exact_body, 0)
        exact_block(i * t, True)
        finish()


def _attention(qg3, k3, aux, vt, n_heads):
    b, s, _ = k3.shape
    hd = k3.shape[2] // n_heads
    assert hd == LANES
    t = _tile(s, 512)
    return pl.pallas_call(
        _attn_kernel,
        out_shape=jax.ShapeDtypeStruct((b, s, n_heads * hd), BF16),
        grid=(b, n_heads, s // t),
        in_specs=[
            pl.BlockSpec((None, t, hd), lambda bi, h, i: (bi, i, h)),
            pl.BlockSpec((None, s, hd), lambda bi, h, i: (bi, 0, h)),
            pl.BlockSpec((None, None, s, LANES), lambda bi, h, i: (bi, h, 0, 0)),
            pl.BlockSpec((None, hd, s), lambda bi, h, i: (bi, h, 0)),
        ],
        out_specs=pl.BlockSpec((None, t, hd), lambda bi, h, i: (bi, i, h)),
        scratch_shapes=[
            pltpu.VMEM((t, hd + LANES), BF16),
            pltpu.VMEM((1, t), F32),
            pltpu.VMEM((1, t), F32),
            pltpu.VMEM((hd, t), F32),
        ],
        compiler_params=_cparams(("parallel", "parallel", "arbitrary"), 48),
        name="fox_attention",
    )(qg3, k3, aux, vt)


def _gated_out_kernel(o_ref, gate_ref, x_ref, w_ref, out_ref):
    g = o_ref[...].astype(F32) * jax.nn.sigmoid(gate_ref[...].astype(F32))
    out_ref[...] = x_ref[...] + jnp.dot(g.astype(BF16), w_ref[...], preferred_element_type=F32)


def _gated_out(o2, qg, x2, w_o):
    n, d = x2.shape
    hd_all = o2.shape[1]
    tm = _tile(n, 512)
    return pl.pallas_call(
        _gated_out_kernel,
        out_shape=jax.ShapeDtypeStruct((n, d), F32),
        grid=(n // tm,),
        in_specs=[
            pl.BlockSpec((tm, hd_all), lambda i: (i, 0)),
            pl.BlockSpec((tm, hd_all), lambda i: (i, 1)),
            pl.BlockSpec((tm, d), lambda i: (i, 0)),
            pl.BlockSpec((hd_all, d), lambda i: (0, 0)),
        ],
        out_specs=pl.BlockSpec((tm, d), lambda i: (i, 0)),
        compiler_params=_cparams(("parallel",), 48),
        name="gated_out_proj",
    )(o2, qg, x2, w_o)


def kernel(x, a_norm, a_w_in, a_conv_w, a_conv_b, a_w_rec, a_b_rec, a_w_inp, a_b_inp, a_lambda, a_w_out, kv_norm, kv_w, kv_b_forget, b_norm, b_w_qg, b_w_o, m_norm, m_w_group, m_b_group, m_w_router, m_b_router, m_w_in, m_w_out, final_norm):
    b, s, d = x.shape
    n = b * s
    depth = m_norm.shape[0]
    n_a = a_norm.shape[0]
    n_heads = kv_b_forget.shape[0]
    hd_all = b_w_o.shape[1]
    head_dim = hd_all // n_heads
    x2 = x.reshape(n, d)
    k3 = aux = vt = None
    for layer in range(depth):
        if layer < n_a:
            i = layer
            d_rnn = a_w_out.shape[1]
            proj = _norm_proj(x2, a_norm[i], a_w_in[i].astype(BF16), jnp.ones((2 * d_rnn,), F32),
                              BF16, "rglru_in_proj")
            x2 = _rglru(proj, x2.reshape(b, s, d), a_conv_w[i], a_conv_b[i], a_w_rec[i], a_b_rec[i],
                        a_w_inp[i], a_b_inp[i], a_lambda[i], a_w_out[i]).reshape(n, d)
        else:
            j = layer - n_a
            q_scale = jnp.concatenate([jnp.full((hd_all,), head_dim ** -0.5 * LOG2E, F32),
                                       jnp.ones((hd_all,), F32)])
            qg = _norm_proj(x2, b_norm[j], b_w_qg[j].astype(BF16), q_scale, BF16, "fox_qg_proj")
            o = _attention(qg.reshape(b, s, 2 * hd_all), k3, aux, vt, n_heads)
            x2 = _gated_out(o.reshape(n, hd_all), qg, x2, b_w_o[j].astype(BF16))
        last = layer == depth - 1
        x2 = _moe(x2, m_norm[layer], m_w_group[layer], m_b_group[layer], m_w_router[layer],
                  m_b_router[layer], m_w_in, m_w_out, layer, final_norm, last)
        if layer == n_a - 1:
            x3 = x2.reshape(b, s, d)
            k3 = _norm_proj(x2, kv_norm, kv_w[:, :hd_all].astype(BF16), jnp.ones((hd_all,), F32),
                            BF16, "shared_k_proj").reshape(b, s, hd_all)
            vt = _norm_proj_t(x3, kv_norm, kv_w[:, hd_all:2 * hd_all].T.astype(BF16), BF16,
                              "shared_vt_proj")
            aux = _forget_aux(x3, kv_norm, kv_w[:, 2 * hd_all:], kv_b_forget)
    if depth == 0:
        x2 = _rms(x2, final_norm)
    return x2.reshape(b, s, d)
```

```python
import functools
import math

import jax
import jax.numpy as jnp
from jax import lax
from jax.experimental import pallas as pl
from jax.experimental.pallas import tpu as pltpu

F32 = jnp.float32
BF16 = jnp.bfloat16
I32 = jnp.int32

EPS = 1e-6
LRU_C = 8.0
TOP_K = 2
LOG2E = 1.4426950408889634
MASK_VALUE = -1e30

V7X_VMEM_BYTES = 64 * 1024 * 1024
SUBLANES = 8
LANES = 128
MIB = 1024 * 1024


def _cparams(semantics, vmem_mib):
    assert vmem_mib * MIB < V7X_VMEM_BYTES
    return pltpu.CompilerParams(dimension_semantics=semantics, vmem_limit_bytes=vmem_mib * MIB)


def _tile(dim, pref):
    t = min(dim, pref)
    assert dim % t == 0, (dim, pref)
    return t


def _rms(x, gain):
    return x * lax.rsqrt(jnp.mean(x * x, axis=-1, keepdims=True) + EPS) * gain


def _norm_proj_kernel(x_ref, g_ref, w_ref, s_ref, o_ref, hn_ref):
    @pl.when(pl.program_id(1) == 0)
    def _():
        hn_ref[...] = _rms(x_ref[...], g_ref[...]).astype(BF16)

    acc = jnp.dot(hn_ref[...], w_ref[...], preferred_element_type=F32)
    o_ref[...] = (acc * s_ref[...]).astype(o_ref.dtype)


def _norm_proj(x, gain, w, col_scale, out_dtype, name):
    n, d = x.shape
    n_out = w.shape[1]
    tm = _tile(n, 1024)
    tn = _tile(n_out, 1024)
    return pl.pallas_call(
        _norm_proj_kernel,
        out_shape=jax.ShapeDtypeStruct((n, n_out), out_dtype),
        grid=(n // tm, n_out // tn),
        in_specs=[
            pl.BlockSpec((tm, d), lambda i, j: (i, 0)),
            pl.BlockSpec((1, d), lambda i, j: (0, 0)),
            pl.BlockSpec((d, tn), lambda i, j: (0, j)),
            pl.BlockSpec((1, tn), lambda i, j: (0, j)),
        ],
        out_specs=pl.BlockSpec((tm, tn), lambda i, j: (i, j)),
        scratch_shapes=[pltpu.VMEM((tm, d), BF16)],
        compiler_params=_cparams(("parallel", "arbitrary"), 48),
        name=name,
    )(x, gain.reshape(1, d), w, col_scale.reshape(1, n_out))


def _norm_proj_t_kernel(x_ref, g_ref, wt_ref, o_ref, hn_ref):
    @pl.when(pl.program_id(2) == 0)
    def _():
        hn_ref[...] = _rms(x_ref[0], g_ref[...]).astype(BF16)

    acc = lax.dot_general(wt_ref[...], hn_ref[...], (((1,), (1,)), ((), ())),
                          preferred_element_type=F32)
    o_ref[0] = acc.astype(o_ref.dtype)


def _norm_proj_t(x3, gain, w_t, out_dtype, name):
    b, s, d = x3.shape
    n_out = w_t.shape[0]
    tm = _tile(s, 1024)
    tn = _tile(n_out, 1024)
    return pl.pallas_call(
        _norm_proj_t_kernel,
        out_shape=jax.ShapeDtypeStruct((b, n_out, s), out_dtype),
        grid=(b, s // tm, n_out // tn),
        in_specs=[
            pl.BlockSpec((1, tm, d), lambda bi, i, j: (bi, i, 0)),
            pl.BlockSpec((1, d), lambda bi, i, j: (0, 0)),
            pl.BlockSpec((tn, d), lambda bi, i, j: (j, 0)),
        ],
        out_specs=pl.BlockSpec((1, tn, tm), lambda bi, i, j: (bi, j, i)),
        scratch_shapes=[pltpu.VMEM((tm, d), BF16)],
        compiler_params=_cparams(("parallel", "parallel", "arbitrary"), 48),
        name=name,
    )(x3, gain.reshape(1, d), w_t)


SCAN_STRIP = 512


def _rglru_kernel(u_ref, y_ref, x_ref, cw_ref, cb_ref, wr_ref, br_ref, wi_ref, bi_ref,
                  lam_ref, wo_ref, o_ref, ubuf_ref, a_ref, b_ref, h_ref, *, conv_width):
    ts, d = u_ref.shape
    nb, blk, _ = wr_ref.shape
    halo = SUBLANES

    @pl.when(pl.program_id(1) == 0)
    def _():
        ubuf_ref[0:halo, :] = jnp.zeros((halo, d), F32)
        h_ref[...] = jnp.zeros_like(h_ref)

    u = u_ref[...].astype(F32)
    ubuf_ref[halo:halo + ts, :] = u
    uc = u * cw_ref[conv_width - 1:conv_width, :] + cb_ref[...]
    for j in range(conv_width - 1):
        shift = conv_width - 1 - j
        uc = uc + ubuf_ref[halo - shift:halo - shift + ts, :] * cw_ref[j:j + 1, :]
    ubuf_ref[0:halo, :] = u[ts - halo:, :]

    ub = uc.astype(BF16)
    r_parts, i_parts = [], []
    for n in range(nb):
        ubn = ub[:, n * blk:(n + 1) * blk]
        r_parts.append(jnp.dot(ubn, wr_ref[n], preferred_element_type=F32))
        i_parts.append(jnp.dot(ubn, wi_ref[n], preferred_element_type=F32))
    r = jax.nn.sigmoid(jnp.concatenate(r_parts, axis=1) + br_ref[...])
    gi = jax.nn.sigmoid(jnp.concatenate(i_parts, axis=1) + bi_ref[...])
    lam = lam_ref[...]
    sp = jnp.maximum(-lam, 0.0) + jnp.log(1.0 + jnp.exp(-jnp.abs(lam)))
    log_a = (-LRU_C * r) * sp
    a_ref[...] = jnp.exp(log_a)
    th = jnp.tanh(log_a)
    b_ref[...] = jnp.sqrt(-2.0 * th / (1.0 - th)) * (gi * uc)

    row = lax.broadcasted_iota(I32, (SUBLANES, SCAN_STRIP), 0)
    strip = min(SCAN_STRIP, d)
    for c in range(d // strip):
        cols = pl.ds(c * strip, strip)

        def tile_step(t, h):
            rows = pl.ds(pl.multiple_of(t * SUBLANES, SUBLANES), SUBLANES)
            a = a_ref[rows, cols]
            bb = b_ref[rows, cols]
            for sh in (1, 2, 4):
                keep = row[:, :strip] >= sh
                a_prev = jnp.where(keep, pltpu.roll(a, sh, 0), 1.0)
                b_prev = jnp.where(keep, pltpu.roll(bb, sh, 0), 0.0)
                bb = a * b_prev + bb
                a = a * a_prev
            hs = a * h + bb
            b_ref[rows, cols] = hs
            return jnp.broadcast_to(hs[SUBLANES - 1:SUBLANES, :], (SUBLANES, strip))

        h_ref[:, cols] = lax.fori_loop(0, ts // SUBLANES, tile_step, h_ref[:, cols], unroll=2)

    y = y_ref[...].astype(F32)
    gelu = 0.5 * y * (1.0 + jnp.tanh(math.sqrt(2.0 / math.pi) * (y + 0.044715 * (y * y * y))))
    g = (b_ref[...] * gelu).astype(BF16)
    o_ref[...] = x_ref[...] + jnp.dot(g, wo_ref[...], preferred_element_type=F32)


def _rglru(proj, x3, conv_w, conv_b, w_rec, b_rec, w_inp, b_inp, lam, w_out):
    b, s, d = x3.shape
    d_rnn = w_out.shape[0]
    nb, blk, _ = w_rec.shape
    width = conv_w.shape[0]
    assert width - 1 <= SUBLANES
    ts = _tile(s, 256)
    proj3 = proj.reshape(b, s, 2 * d_rnn)
    row = lambda v: v.reshape(1, -1)
    const2 = lambda bi, i: (0, 0)
    const3 = lambda bi, i: (0, 0, 0)
    return pl.pallas_call(
        functools.partial(_rglru_kernel, conv_width=width),
        out_shape=jax.ShapeDtypeStruct((b, s, d), F32),
        grid=(b, s // ts),
        in_specs=[
            pl.BlockSpec((None, ts, d_rnn), lambda bi, i: (bi, i, 0)),
            pl.BlockSpec((None, ts, d_rnn), lambda bi, i: (bi, i, 1)),
            pl.BlockSpec((None, ts, d), lambda bi, i: (bi, i, 0)),
            pl.BlockSpec((width, d_rnn), const2),
            pl.BlockSpec((1, d_rnn), const2),
            pl.BlockSpec((nb, blk, blk), const3),
            pl.BlockSpec((1, d_rnn), const2),
            pl.BlockSpec((nb, blk, blk), const3),
            pl.BlockSpec((1, d_rnn), const2),
            pl.BlockSpec((1, d_rnn), const2),
            pl.BlockSpec((d_rnn, d), const2),
        ],
        out_specs=pl.BlockSpec((None, ts, d), lambda bi, i: (bi, i, 0)),
        scratch_shapes=[
            pltpu.VMEM((SUBLANES + ts, d_rnn), F32),
            pltpu.VMEM((ts, d_rnn), F32),
            pltpu.VMEM((ts, d_rnn), F32),
            pltpu.VMEM((SUBLANES, d_rnn), F32),
        ],
        compiler_params=_cparams(("parallel", "arbitrary"), 56),
        name="rglru",
    )(proj3, proj3, x3, conv_w, row(conv_b), w_rec.astype(BF16), row(b_rec),
      w_inp.astype(BF16), row(b_inp), row(lam), w_out.astype(BF16))


def _split3(v):
    v1 = v.astype(BF16)
    r1 = v - v1.astype(F32)
    v2 = r1.astype(BF16)
    v3 = (r1 - v2.astype(F32)).astype(BF16)
    return v1, v2, v3


def _dot_nt(a, b):
    return lax.dot_general(a, b, (((1,), (1,)), ((), ())), preferred_element_type=F32)


def _first_argmax(v, n):
    idx = lax.broadcasted_iota(I32, v.shape, 0)
    vmax = jnp.max(v, axis=0, keepdims=True)
    amax = jnp.min(jnp.where(v == vmax, idx, n), axis=0, keepdims=True)
    return amax, vmax


def _router_kernel(x_ref, g_ref, w1_ref, w2_ref, w3_ref, bias_ref, hn_ref, eid_ref, wt_ref,
                   rank_ref, cnt_ref, tri_ref, carry_ref, *, n_groups, per_group):
    tm = x_ref.shape[0]
    n_exp = n_groups * per_group

    @pl.when(pl.program_id(0) == 0)
    def _():
        r = lax.broadcasted_iota(I32, (tm, tm), 0)
        c = lax.broadcasted_iota(I32, (tm, tm), 1)
        tri_ref[...] = (r < c).astype(BF16)
        carry_ref[...] = jnp.zeros_like(carry_ref)

    hn = _rms(x_ref[...], g_ref[...])
    hn_ref[...] = hn
    h1, h2, h3 = _split3(hn)
    w1, w2, w3 = w1_ref[...], w2_ref[...], w3_ref[...]
    logits = (_dot_nt(w1, h1) + (_dot_nt(w1, h2) + _dot_nt(w2, h1))
              + (_dot_nt(w1, h3) + _dot_nt(w2, h2) + _dot_nt(w3, h1)))
    logits = logits + bias_ref[:, 0:1]

    gl = logits[0:n_groups, :]
    g_idx, g_max = _first_argmax(gl, n_groups)
    gp_top = 1.0 / jnp.sum(jnp.exp(gl - g_max), axis=0, keepdims=True)
    el = jnp.zeros((per_group, tm), F32)
    for g in range(n_groups):
        lo = n_groups + g * per_group
        el = jnp.where(g_idx == g, logits[lo:lo + per_group, :], el)
    e_max = jnp.max(el, axis=0, keepdims=True)
    ex = jnp.exp(el - e_max)
    ep = ex / jnp.sum(ex, axis=0, keepdims=True)
    e1, p1 = _first_argmax(ep, per_group)
    sub = lax.broadcasted_iota(I32, ep.shape, 0)
    e2, p2 = _first_argmax(jnp.where(sub == e1, -1.0, ep), per_group)
    denom = p1 + p2
    eid1 = g_idx * per_group + e1
    eid2 = g_idx * per_group + e2
    eid_ref[...] = jnp.concatenate([eid1, eid2], axis=0)
    wt_ref[...] = jnp.concatenate([gp_top * p1 / denom, gp_top * p2 / denom], axis=0)

    e_iota = lax.broadcasted_iota(I32, (n_exp, tm), 0)
    hot1 = e_iota == eid1
    hot2 = e_iota == eid2
    chosen = jnp.logical_or(hot1, hot2)
    before = jnp.dot(chosen.astype(BF16), tri_ref[...], preferred_element_type=F32)
    base = (before + carry_ref[:, 0:1]).astype(I32)
    rank1 = jnp.sum(jnp.where(hot1, base, 0), axis=0, keepdims=True)
    rank2 = jnp.sum(jnp.where(hot2, base, 0), axis=0, keepdims=True)
    rank_ref[...] = jnp.concatenate([rank1, rank2], axis=0)
    carry_ref[...] = carry_ref[...] + jnp.sum(chosen.astype(F32), axis=1, keepdims=True)
    cnt_ref[...] = carry_ref[...].astype(I32)


def _router(x2, gain, w_group, b_group, w_router, b_router):
    n, d = x2.shape
    n_groups = w_group.shape[1]
    n_exp = w_router.shape[1]
    per_group = n_exp // n_groups
    assert per_group == SUBLANES and n_groups == SUBLANES
    rows = n_groups + n_exp
    rows_p = -(-rows // LANES) * LANES
    w_t = jnp.concatenate([w_group, w_router], axis=1).T
    w_t = jnp.pad(w_t, ((0, rows_p - rows), (0, 0)))
    w1, w2, w3 = _split3(w_t)
    bias = jnp.pad(jnp.concatenate([b_group, b_router]), (0, rows_p - rows))
    bias = jnp.broadcast_to(bias[:, None], (rows_p, LANES))
    tm = _tile(n, 512)
    const = lambda i: (0, 0)
    return pl.pallas_call(
        functools.partial(_router_kernel, n_groups=n_groups, per_group=per_group),
        out_shape=(
            jax.ShapeDtypeStruct((n, d), F32),
            jax.ShapeDtypeStruct((TOP_K, n), I32),
            jax.ShapeDtypeStruct((TOP_K, n), F32),
            jax.ShapeDtypeStruct((TOP_K, n), I32),
            jax.ShapeDtypeStruct((n_exp, LANES), I32),
        ),
        grid=(n // tm,),
        in_specs=[
            pl.BlockSpec((tm, d), lambda i: (i, 0)),
            pl.BlockSpec((1, d), const),
            pl.BlockSpec((rows_p, d), const),
            pl.BlockSpec((rows_p, d), const),
            pl.BlockSpec((rows_p, d), const),
            pl.BlockSpec((rows_p, LANES), const),
        ],
        out_specs=(
            pl.BlockSpec((tm, d), lambda i: (i, 0)),
            pl.BlockSpec((TOP_K, tm), lambda i: (0, i)),
            pl.BlockSpec((TOP_K, tm), lambda i: (0, i)),
            pl.BlockSpec((TOP_K, tm), lambda i: (0, i)),
            pl.BlockSpec((n_exp, LANES), const),
        ),
        scratch_shapes=[pltpu.VMEM((tm, tm), BF16), pltpu.VMEM((n_exp, LANES), F32)],
        compiler_params=_cparams(("arbitrary",), 40),
        name="moe_router",
    )(x2, gain.reshape(1, d), w1, w2, w3, bias)


def _dest_kernel(eid_ref, rank_ref, start_ref, dest_ref):
    eid = eid_ref[...]
    n_exp = start_ref.shape[0]
    dest = rank_ref[...]
    for k in range(eid.shape[0]):
        hot = lax.broadcasted_iota(I32, (n_exp, eid.shape[1]), 0) == eid[k:k + 1, :]
        off = jnp.sum(jnp.where(hot, start_ref[:, 0:1], 0), axis=0, keepdims=True)
        dest_ref[k:k + 1, :] = dest[k:k + 1, :] + off


def _dest_rows(eid, rank, seg_start):
    k, n = eid.shape
    n_exp = seg_start.shape[0]
    tm = _tile(n, 2048)
    start = jnp.broadcast_to(seg_start[:, None], (n_exp, LANES)).astype(I32)
    return pl.pallas_call(
        _dest_kernel,
        out_shape=jax.ShapeDtypeStruct((k, n), I32),
        grid=(n // tm,),
        in_specs=[
            pl.BlockSpec((k, tm), lambda i: (0, i)),
            pl.BlockSpec((k, tm), lambda i: (0, i)),
            pl.BlockSpec((n_exp, LANES), lambda i: (0, 0)),
        ],
        out_specs=pl.BlockSpec((k, tm), lambda i: (0, i)),
        compiler_params=_cparams(("parallel",), 16),
        name="moe_dest",
    )(eid, rank, start)


def _dispatch_kernel(nv_ref, dest_ref, hn_ref, xs_ref, zero_ref, sem, zsem, *, chunk_rows):
    k, tm = dest_ref.shape[1], dest_ref.shape[2]
    n_chunks = nv_ref.shape[0]

    @pl.when(pl.program_id(0) == 0)
    def _():
        zero_ref[...] = jnp.zeros_like(zero_ref)

        def zero_copy(c):
            rows = pl.ds(pl.multiple_of(c * chunk_rows, chunk_rows), chunk_rows)
            return pltpu.make_async_copy(zero_ref, xs_ref.at[rows], zsem)

        def start(c, carry):
            @pl.when(nv_ref[c] < chunk_rows)
            def _():
                zero_copy(c).start()
            return carry

        def finish(c, carry):
            @pl.when(nv_ref[c] < chunk_rows)
            def _():
                zero_copy(c).wait()
            return carry

        lax.fori_loop(0, n_chunks, start, 0)
        lax.fori_loop(0, n_chunks, finish, 0)


    def issue(t, c):
        for kk in range(k):
            pltpu.make_async_copy(hn_ref.at[pl.ds(t, 1)],
                                  xs_ref.at[pl.ds(dest_ref[0, kk, t], 1)], sem).start()
        return c

    lax.fori_loop(0, tm, issue, 0, unroll=8)
    for kk in range(k):
        pltpu.make_async_copy(hn_ref, xs_ref.at[pl.ds(0, tm)], sem).wait()


def _dispatch(dest, hn, n_valid, n_rows, chunk_rows):
    k, n = dest.shape
    d = hn.shape[1]
    tm = _tile(n, 512)
    dest3 = dest.reshape(k, n // tm, tm).transpose(1, 0, 2)
    grid_spec = pltpu.PrefetchScalarGridSpec(
        num_scalar_prefetch=1,
        grid=(n // tm,),
        in_specs=[
            pl.BlockSpec((1, k, tm), lambda i, nv: (i, 0, 0), memory_space=pltpu.SMEM),
            pl.BlockSpec((tm, d), lambda i, nv: (i, 0)),
        ],
        out_specs=pl.BlockSpec(memory_space=pl.ANY),
        scratch_shapes=[pltpu.VMEM((chunk_rows, d), hn.dtype), pltpu.SemaphoreType.DMA(()),
                        pltpu.SemaphoreType.DMA(())],
    )
    return pl.pallas_call(
        functools.partial(_dispatch_kernel, chunk_rows=chunk_rows),
        out_shape=jax.ShapeDtypeStruct((n_rows, d), hn.dtype),
        grid_spec=grid_spec,
        compiler_params=_cparams(("arbitrary",), 24),
        name="moe_dispatch",
    )(n_valid, dest3, hn)


def _expert_kernel(ce_ref, nu_ref, fresh_ref, xs_ref, wi_ref, wo_ref, ys_ref,
                   wib_ref, wob_ref, *, d_expert):
    i = pl.program_id(0)

    @pl.when(i < nu_ref[0])
    def _():
        @pl.when(fresh_ref[i] == 1)
        def _():
            wib_ref[...] = wi_ref[...].astype(BF16)
            wob_ref[...] = wo_ref[...].astype(BF16)

        gu = jnp.dot(xs_ref[...].astype(BF16), wib_ref[...], preferred_element_type=F32)
        gate = gu[:, :d_expert]
        act = (gate * jax.nn.sigmoid(gate) * gu[:, d_expert:]).astype(BF16)
        ys_ref[...] = jnp.dot(act, wob_ref[...], preferred_element_type=F32)

    @pl.when(i >= nu_ref[0])
    def _():
        ys_ref[...] = jnp.zeros_like(ys_ref)


def _experts(xs, chunk_e, n_used, fresh, w_in_all, w_out_all, layer, chunk_rows):
    n_rows, d = xs.shape
    d_expert = w_out_all.shape[2]
    n_chunks = n_rows // chunk_rows
    xs_map = lambda i, ce, nu, fr: (jnp.minimum(i, jnp.maximum(nu[0] - 1, 0)), 0)
    w_map = lambda i, ce, nu, fr: (layer, ce[i], 0, 0)
    grid_spec = pltpu.PrefetchScalarGridSpec(
        num_scalar_prefetch=3,
        grid=(n_chunks,),
        in_specs=[
            pl.BlockSpec((chunk_rows, d), xs_map),
            pl.BlockSpec((None, None, d, 2 * d_expert), w_map),
            pl.BlockSpec((None, None, d_expert, d), w_map),
        ],
        out_specs=pl.BlockSpec((chunk_rows, d), lambda i, ce, nu, fr: (i, 0)),
        scratch_shapes=[pltpu.VMEM((d, 2 * d_expert), BF16), pltpu.VMEM((d_expert, d), BF16)],
    )
    return pl.pallas_call(
        functools.partial(_expert_kernel, d_expert=d_expert),
        out_shape=jax.ShapeDtypeStruct((n_rows, d), F32),
        grid_spec=grid_spec,
        compiler_params=_cparams(("arbitrary",), 56),
        name="moe_experts",
    )(chunk_e, n_used, fresh, xs, w_in_all, w_out_all)


def _combine_kernel(dest_ref, next_ref, x_ref, wt_ref, g_ref, ys_ref, o_ref, buf_ref, sem_ref, *,
                    final_norm, n_blocks):
    k, tm = dest_ref.shape[1], dest_ref.shape[2]
    i = pl.program_id(0)

    def issue_block(d_ref, slot):
        def body(t, c):
            for kk in range(k):
                pltpu.make_async_copy(ys_ref.at[pl.ds(d_ref[0, kk, t], 1)],
                                      buf_ref.at[slot, kk, pl.ds(t, 1)], sem_ref.at[slot]).start()
            return c
        lax.fori_loop(0, tm, body, 0, unroll=8)

    @pl.when(i == 0)
    def _():
        issue_block(dest_ref, 0)

    @pl.when(i + 1 < n_blocks)
    def _():
        issue_block(next_ref, (i + 1) % 2)

    slot = i % 2
    for kk in range(k):
        pltpu.make_async_copy(ys_ref.at[pl.ds(0, tm)], buf_ref.at[slot, kk], sem_ref.at[slot]).wait()

    out = x_ref[...]
    for kk in range(k):
        out = out + wt_ref[:, kk:kk + 1] * buf_ref[slot, kk]
    if final_norm:
        out = _rms(out, g_ref[...])
    o_ref[...] = out


def _combine(dest, x2, wts, ys, gain, final_norm):
    k, n = dest.shape
    d = x2.shape[1]
    tm = _tile(n, 256)
    n_blocks = n // tm
    dest3 = dest.reshape(k, n_blocks, tm).transpose(1, 0, 2)
    return pl.pallas_call(
        functools.partial(_combine_kernel, final_norm=final_norm, n_blocks=n_blocks),
        out_shape=jax.ShapeDtypeStruct((n, d), F32),
        grid=(n_blocks,),
        in_specs=[
            pl.BlockSpec((1, k, tm), lambda i: (i, 0, 0), memory_space=pltpu.SMEM),
            pl.BlockSpec((1, k, tm), lambda i: (jnp.minimum(i + 1, n_blocks - 1), 0, 0),
                         memory_space=pltpu.SMEM),
            pl.BlockSpec((tm, d), lambda i: (i, 0)),
            pl.BlockSpec((tm, k), lambda i: (i, 0)),
            pl.BlockSpec((1, d), lambda i: (0, 0)),
            pl.BlockSpec(memory_space=pl.ANY),
        ],
        out_specs=pl.BlockSpec((tm, d), lambda i: (i, 0)),
        scratch_shapes=[pltpu.VMEM((2, k, tm, d), F32), pltpu.SemaphoreType.DMA((2,))],
        compiler_params=_cparams(("arbitrary",), 40),
        name="moe_combine",
    )(dest3, dest3, x2, wts.T, gain.reshape(1, d), ys)


EXPERT_CHUNK_ROWS = 256


def _moe(x2, norm_gain, w_group, b_group, w_router, b_router, w_in_all, w_out_all, layer, out_gain,
         final_norm):
    n, d = x2.shape
    n_exp = w_router.shape[1]
    chunk = EXPERT_CHUNK_ROWS
    hn, eid, wts, rank, counts = _router(x2, norm_gain, w_group, b_group, w_router, b_router)
    counts = counts[:, 0]
    padded = (counts + chunk - 1) // chunk * chunk
    seg_end = jnp.cumsum(padded)
    seg_start = seg_end - padded
    n_rows = (-(-(n * TOP_K) // chunk) + n_exp) * chunk
    n_chunks = n_rows // chunk
    chunk_start = jnp.arange(n_chunks, dtype=I32) * chunk
    n_used = (seg_end[-1:] // chunk).astype(I32)
    live_start = jnp.minimum(chunk_start, jnp.maximum(seg_end[-1] - chunk, 0))
    chunk_e = jnp.sum(seg_end[None, :] <= live_start[:, None], axis=1).astype(I32)
    chunk_e = jnp.minimum(chunk_e, n_exp - 1)
    n_valid = jnp.clip(counts[chunk_e] - (chunk_start - seg_start[chunk_e]), 0, chunk).astype(I32)
    fresh = jnp.concatenate([jnp.ones((1,), I32), (chunk_e[1:] != chunk_e[:-1]).astype(I32)])
    dest = _dest_rows(eid, rank, seg_start)
    xs = _dispatch(dest, hn, n_valid, n_rows, chunk)
    ys = _experts(xs, chunk_e, n_used, fresh, w_in_all, w_out_all, layer, chunk)
    return _combine(dest, x2, wts, ys, out_gain, final_norm)


def _forget_kernel(x_ref, g_ref, w_ref, b_ref, aux_ref, end_ref, tri_ref, carry_ref, *, n_heads):
    ts = x_ref.shape[0]

    @pl.when(pl.program_id(1) == 0)
    def _():
        r = lax.broadcasted_iota(I32, (ts, ts), 0)
        c = lax.broadcasted_iota(I32, (ts, ts), 1)
        tri_ref[...] = (c <= r).astype(BF16)
        carry_ref[...] = jnp.zeros_like(carry_ref)

    hn = _rms(x_ref[...], g_ref[...]).astype(BF16)
    f = jnp.dot(hn, w_ref[...], preferred_element_type=F32) + b_ref[...]
    logf = jnp.minimum(f, 0.0) - jnp.log(1.0 + jnp.exp(-jnp.abs(f)))
    l1, l2, l3 = _split3(logf)
    tri = tri_ref[...]
    cum = (jnp.dot(tri, l1, preferred_element_type=F32)
           + jnp.dot(tri, l2, preferred_element_type=F32)
           + jnp.dot(tri, l3, preferred_element_type=F32)) + carry_ref[0:1, :]
    carry_ref[...] = jnp.broadcast_to(cum[ts - 1:ts, :], carry_ref.shape)
    end_ref[...] = carry_ref[...] * (-LOG2E)
    c1, c2, c3 = [c.astype(F32) for c in _split3(cum * (-LOG2E))]
    lane = lax.broadcasted_iota(I32, (ts, LANES), 1)
    for h in range(n_heads):
        col = lambda v: jnp.broadcast_to(v[:, h:h + 1], (ts, LANES))
        aux = jnp.where(lane == 0, col(c1), jnp.where(lane == 1, col(c2),
                        jnp.where(lane == 2, col(c3), 0.0)))
        aux_ref[h] = aux.astype(BF16)


def _forget_aux(x3, gain, w_f, b_f):
    b, s, d = x3.shape
    n_heads = w_f.shape[1]
    assert n_heads <= LANES
    w_p = jnp.pad(w_f, ((0, 0), (0, LANES - n_heads))).astype(BF16)
    b_p = jnp.pad(b_f, (0, LANES - n_heads)).reshape(1, LANES)
    ts = _tile(s, ATTN_BLOCK)
    return pl.pallas_call(
        functools.partial(_forget_kernel, n_heads=n_heads),
        out_shape=(jax.ShapeDtypeStruct((b, n_heads, s, LANES), BF16),
                   jax.ShapeDtypeStruct((b, s // ts, SUBLANES, LANES), F32)),
        grid=(b, s // ts),
        in_specs=[
            pl.BlockSpec((None, ts, d), lambda bi, i: (bi, i, 0)),
            pl.BlockSpec((1, d), lambda bi, i: (0, 0)),
            pl.BlockSpec((d, LANES), lambda bi, i: (0, 0)),
            pl.BlockSpec((1, LANES), lambda bi, i: (0, 0)),
        ],
        out_specs=(pl.BlockSpec((None, n_heads, ts, LANES), lambda bi, i: (bi, 0, i, 0)),
                   pl.BlockSpec((None, None, SUBLANES, LANES), lambda bi, i: (bi, i, 0, 0))),
        scratch_shapes=[pltpu.VMEM((ts, ts), BF16), pltpu.VMEM((SUBLANES, LANES), F32)],
        compiler_params=_cparams(("parallel", "arbitrary"), 32),
        name="forget_cumsum",
    )(x3, gain.reshape(1, d), w_p, b_p)


ATTN_BLOCK = 512
ATTN_GROUP_BLOCKS = 4
FINITE_LIMIT = 3.0e38
SKIP_MARGIN = 160.0


def _attn_kernel(end_ref, q_ref, k_ref, aux_ref, vt_ref, o_ref, qp_ref, m_ref, l_ref, acc_ref,
                 kmax_ref):
    t, hd = q_ref.shape
    i = pl.program_id(2)
    lane = lax.broadcasted_iota(I32, (t, LANES), 1)
    qp_ref[:, 0:hd] = q_ref[...]
    qp_ref[:, hd:hd + LANES] = jnp.where(lane < 3, 1.0, 0.0).astype(BF16)

    def max_row_norm(x):
        xf = x.astype(F32)
        return jnp.sum(xf * xf, axis=1, keepdims=True)

    @pl.when(i == 0)
    def _():
        def body(c, best):
            rows = pl.ds(pl.multiple_of(c * t, t), t)
            return jnp.maximum(best, max_row_norm(k_ref[rows, :]))
        best = lax.fori_loop(0, k_ref.shape[0] // t, body, jnp.zeros((t, 1), F32))
        kmax_ref[0] = jnp.sqrt(jnp.max(best))

    reach = jnp.sqrt(jnp.max(max_row_norm(q_ref[...]))) * kmax_ref[0]

    def alive(block):
        bias_top = end_ref[0, jnp.maximum(block, 0)]
        return jnp.logical_and(block >= 0,
                               reach + bias_top - jnp.min(m_ref[...]) > -SKIP_MARGIN)

    def scores(start, size, masked):
        rows = pl.ds(pl.multiple_of(start, t), size)
        kp = jnp.concatenate([k_ref[rows, :], aux_ref[rows, :]], axis=1)
        s_t = _dot_nt(kp, qp_ref[...])
        if masked:
            kr = lax.broadcasted_iota(I32, (size, t), 0)
            qc = lax.broadcasted_iota(I32, (size, t), 1)
            s_t = jnp.where(kr <= qc, s_t, MASK_VALUE)
        return s_t, rows

    def exact_block(start, masked):
        s_t, rows = scores(start, t, masked)
        m_old = m_ref[...]
        m_new = jnp.maximum(m_old, jnp.max(s_t, axis=0, keepdims=True))
        alpha = jnp.exp2(m_old - m_new)
        p = jnp.exp2(s_t - m_new)
        l_ref[...] = alpha * l_ref[...] + jnp.sum(p, axis=0, keepdims=True)
        pv = jnp.dot(vt_ref[:, rows], p.astype(BF16), preferred_element_type=F32)
        acc_ref[...] = alpha * acc_ref[...] + pv
        m_ref[...] = m_new

    def lagged_group(start, size):
        s_t, rows = scores(start, size, False)
        m_old = m_ref[...]
        p = jnp.exp2(s_t - m_old)
        pv = jnp.dot(vt_ref[:, rows], p.astype(BF16), preferred_element_type=F32)
        m_new = jnp.maximum(m_old, jnp.max(s_t, axis=0, keepdims=True))
        alpha = jnp.exp2(m_old - m_new)
        l_ref[...] = (l_ref[...] + jnp.sum(p, axis=0, keepdims=True)) * alpha
        acc_ref[...] = (acc_ref[...] + pv) * alpha
        m_ref[...] = m_new

    def reset():
        m_ref[...] = jnp.full_like(m_ref, MASK_VALUE)
        l_ref[...] = jnp.zeros_like(l_ref)
        acc_ref[...] = jnp.zeros_like(acc_ref)

    def finish():
        out = acc_ref[...] / l_ref[...]
        o_ref[...] = out.T.astype(o_ref.dtype)
        return out

    reset()
    exact_block(i * t, True)
    gb = ATTN_GROUP_BLOCKS
    n_groups = i // gb
    rest = i - n_groups * gb

    def group_cond(c):
        g, live = c
        return jnp.logical_and(g < n_groups, live)

    def group_body(c):
        g, _ = c
        first = i - (g + 1) * gb
        lagged_group(first * t, gb * t)
        return g + 1, alive(first - 1)

    _, live = lax.while_loop(group_cond, group_body, (0, alive(i - 1)))

    @pl.when(jnp.logical_and((rest & 2) != 0, live))
    def _():
        lagged_group((rest - 2) * t, 2 * t)

    @pl.when(jnp.logical_and((rest & 1) != 0, jnp.logical_and(live, alive(0))))
    def _():
        lagged_group(0, t)

    out = finish()
    overflowed = jnp.max(jnp.where(jnp.abs(out) < FINITE_LIMIT, 0.0, 1.0)) > 0.0

    @pl.when(overflowed)
    def _():
        reset()

        def exact_body(j, c):
            exact_block(j * t, False)
            return c

        lax.fori_loop(0, i, exact_body, 0)
        exact_block(i * t, True)
        finish()


def _attention(qg3, k3, aux, vt, block_end, n_heads):
    b, s, _ = k3.shape
    hd = k3.shape[2] // n_heads
    assert hd == LANES
    t = _tile(s, ATTN_BLOCK)
    end_tab = block_end[:, :, 0, :n_heads].transpose(0, 2, 1).reshape(b, n_heads, 1, s // t)
    return pl.pallas_call(
        _attn_kernel,
        out_shape=jax.ShapeDtypeStruct((b, s, n_heads * hd), BF16),
        grid=(b, n_heads, s // t),
        in_specs=[
            pl.BlockSpec((None, None, 1, s // t), lambda bi, h, i: (bi, h, 0, 0),
                         memory_space=pltpu.SMEM),
            pl.BlockSpec((None, t, hd), lambda bi, h, i: (bi, i, h)),
            pl.BlockSpec((None, s, hd), lambda bi, h, i: (bi, 0, h)),
            pl.BlockSpec((None, None, s, LANES), lambda bi, h, i: (bi, h, 0, 0)),
            pl.BlockSpec((None, hd, s), lambda bi, h, i: (bi, h, 0)),
        ],
        out_specs=pl.BlockSpec((None, t, hd), lambda bi, h, i: (bi, i, h)),
        scratch_shapes=[
            pltpu.VMEM((t, hd + LANES), BF16),
            pltpu.VMEM((1, t), F32),
            pltpu.VMEM((1, t), F32),
            pltpu.VMEM((hd, t), F32),
            pltpu.SMEM((1,), F32),
        ],
        compiler_params=_cparams(("parallel", "parallel", "arbitrary"), 48),
        name="fox_attention",
    )(end_tab, qg3, k3, aux, vt)


def _gated_out_kernel(o_ref, gate_ref, x_ref, w_ref, out_ref):
    g = o_ref[...].astype(F32) * jax.nn.sigmoid(gate_ref[...].astype(F32))
    out_ref[...] = x_ref[...] + jnp.dot(g.astype(BF16), w_ref[...], preferred_element_type=F32)


def _gated_out(o2, qg, x2, w_o):
    n, d = x2.shape
    hd_all = o2.shape[1]
    tm = _tile(n, 512)
    return pl.pallas_call(
        _gated_out_kernel,
        out_shape=jax.ShapeDtypeStruct((n, d), F32),
        grid=(n // tm,),
        in_specs=[
            pl.BlockSpec((tm, hd_all), lambda i: (i, 0)),
            pl.BlockSpec((tm, hd_all), lambda i: (i, 1)),
            pl.BlockSpec((tm, d), lambda i: (i, 0)),
            pl.BlockSpec((hd_all, d), lambda i: (0, 0)),
        ],
        out_specs=pl.BlockSpec((tm, d), lambda i: (i, 0)),
        compiler_params=_cparams(("parallel",), 48),
        name="gated_out_proj",
    )(o2, qg, x2, w_o)


def kernel(x, a_norm, a_w_in, a_conv_w, a_conv_b, a_w_rec, a_b_rec, a_w_inp, a_b_inp, a_lambda, a_w_out, kv_norm, kv_w, kv_b_forget, b_norm, b_w_qg, b_w_o, m_norm, m_w_group, m_b_group, m_w_router, m_b_router, m_w_in, m_w_out, final_norm):
    b, s, d = x.shape
    n = b * s
    depth = m_norm.shape[0]
    n_a = a_norm.shape[0]
    n_heads = kv_b_forget.shape[0]
    hd_all = b_w_o.shape[1]
    head_dim = hd_all // n_heads
    x2 = x.reshape(n, d)
    k3 = aux = vt = block_end = None
    for layer in range(depth):
        if layer < n_a:
            i = layer
            d_rnn = a_w_out.shape[1]
            proj = _norm_proj(x2, a_norm[i], a_w_in[i].astype(BF16), jnp.ones((2 * d_rnn,), F32),
                              BF16, "rglru_in_proj")
            x2 = _rglru(proj, x2.reshape(b, s, d), a_conv_w[i], a_conv_b[i], a_w_rec[i], a_b_rec[i],
                        a_w_inp[i], a_b_inp[i], a_lambda[i], a_w_out[i]).reshape(n, d)
        else:
            j = layer - n_a
            q_scale = jnp.concatenate([jnp.full((hd_all,), head_dim ** -0.5 * LOG2E, F32),
                                       jnp.ones((hd_all,), F32)])
            qg = _norm_proj(x2, b_norm[j], b_w_qg[j].astype(BF16), q_scale, BF16, "fox_qg_proj")
            o = _attention(qg.reshape(b, s, 2 * hd_all), k3, aux, vt, block_end, n_heads)
            x2 = _gated_out(o.reshape(n, hd_all), qg, x2, b_w_o[j].astype(BF16))
        last = layer == depth - 1
        x2 = _moe(x2, m_norm[layer], m_w_group[layer], m_b_group[layer], m_w_router[layer],
                  m_b_router[layer], m_w_in, m_w_out, layer, final_norm, last)
        if layer == n_a - 1:
            x3 = x2.reshape(b, s, d)
            k3 = _norm_proj(x2, kv_norm, kv_w[:, :hd_all].astype(BF16), jnp.ones((hd_all,), F32),
                            BF16, "shared_k_proj").reshape(b, s, hd_all)
            vt = _norm_proj_t(x3, kv_norm, kv_w[:, hd_all:2 * hd_all].T.astype(BF16), BF16,
                              "shared_vt_proj")
            aux, block_end = _forget_aux(x3, kv_norm, kv_w[:, 2 * hd_all:], kv_b_forget)
    if depth == 0:
        x2 = _rms(x2, final_norm)
    return x2.reshape(b, s, d)
```

```python
import functools
import math

import jax
import jax.numpy as jnp
from jax import lax
from jax.experimental import pallas as pl
from jax.experimental.pallas import tpu as pltpu

F32 = jnp.float32
BF16 = jnp.bfloat16
I32 = jnp.int32

EPS = 1e-6
LRU_C = 8.0
TOP_K = 2
LOG2E = 1.4426950408889634
MASK_VALUE = -1e30

V7X_VMEM_BYTES = 64 * 1024 * 1024
SUBLANES = 8
LANES = 128
MIB = 1024 * 1024


def _cparams(semantics, vmem_mib):
    assert vmem_mib * MIB < V7X_VMEM_BYTES
    return pltpu.CompilerParams(dimension_semantics=semantics, vmem_limit_bytes=vmem_mib * MIB)


def _tile(dim, pref):
    t = min(dim, pref)
    assert dim % t == 0, (dim, pref)
    return t


def _rms(x, gain):
    return x * lax.rsqrt(jnp.mean(x * x, axis=-1, keepdims=True) + EPS) * gain


def _norm_proj_kernel(x_ref, g_ref, w_ref, s_ref, o_ref, hn_ref):
    @pl.when(pl.program_id(1) == 0)
    def _():
        hn_ref[...] = _rms(x_ref[...], g_ref[...]).astype(BF16)

    acc = jnp.dot(hn_ref[...], w_ref[...], preferred_element_type=F32)
    o_ref[...] = (acc * s_ref[...]).astype(o_ref.dtype)


def _norm_proj(x, gain, w, col_scale, out_dtype, name):
    n, d = x.shape
    n_out = w.shape[1]
    tm = _tile(n, 1024)
    tn = _tile(n_out, 1024)
    return pl.pallas_call(
        _norm_proj_kernel,
        out_shape=jax.ShapeDtypeStruct((n, n_out), out_dtype),
        grid=(n // tm, n_out // tn),
        in_specs=[
            pl.BlockSpec((tm, d), lambda i, j: (i, 0)),
            pl.BlockSpec((1, d), lambda i, j: (0, 0)),
            pl.BlockSpec((d, tn), lambda i, j: (0, j)),
            pl.BlockSpec((1, tn), lambda i, j: (0, j)),
        ],
        out_specs=pl.BlockSpec((tm, tn), lambda i, j: (i, j)),
        scratch_shapes=[pltpu.VMEM((tm, d), BF16)],
        compiler_params=_cparams(("parallel", "arbitrary"), 48),
        name=name,
    )(x, gain.reshape(1, d), w, col_scale.reshape(1, n_out))


def _norm_proj_t_kernel(x_ref, g_ref, wt_ref, o_ref, hn_ref):
    @pl.when(pl.program_id(2) == 0)
    def _():
        hn_ref[...] = _rms(x_ref[0], g_ref[...]).astype(BF16)

    acc = lax.dot_general(wt_ref[...], hn_ref[...], (((1,), (1,)), ((), ())),
                          preferred_element_type=F32)
    o_ref[0] = acc.astype(o_ref.dtype)


def _norm_proj_t(x3, gain, w_t, out_dtype, name):
    b, s, d = x3.shape
    n_out = w_t.shape[0]
    tm = _tile(s, 1024)
    tn = _tile(n_out, 1024)
    return pl.pallas_call(
        _norm_proj_t_kernel,
        out_shape=jax.ShapeDtypeStruct((b, n_out, s), out_dtype),
        grid=(b, s // tm, n_out // tn),
        in_specs=[
            pl.BlockSpec((1, tm, d), lambda bi, i, j: (bi, i, 0)),
            pl.BlockSpec((1, d), lambda bi, i, j: (0, 0)),
            pl.BlockSpec((tn, d), lambda bi, i, j: (j, 0)),
        ],
        out_specs=pl.BlockSpec((1, tn, tm), lambda bi, i, j: (bi, j, i)),
        scratch_shapes=[pltpu.VMEM((tm, d), BF16)],
        compiler_params=_cparams(("parallel", "parallel", "arbitrary"), 48),
        name=name,
    )(x3, gain.reshape(1, d), w_t)


SCAN_STRIP = 512


def _rglru_kernel(u_ref, y_ref, x_ref, cw_ref, cb_ref, wr_ref, br_ref, wi_ref, bi_ref,
                  lam_ref, wo_ref, o_ref, ubuf_ref, a_ref, b_ref, h_ref, *, conv_width):
    ts, d = u_ref.shape
    nb, blk, _ = wr_ref.shape
    halo = SUBLANES

    @pl.when(pl.program_id(1) == 0)
    def _():
        ubuf_ref[0:halo, :] = jnp.zeros((halo, d), F32)
        h_ref[...] = jnp.zeros_like(h_ref)

    u = u_ref[...].astype(F32)
    ubuf_ref[halo:halo + ts, :] = u
    uc = u * cw_ref[conv_width - 1:conv_width, :] + cb_ref[...]
    for j in range(conv_width - 1):
        shift = conv_width - 1 - j
        uc = uc + ubuf_ref[halo - shift:halo - shift + ts, :] * cw_ref[j:j + 1, :]
    ubuf_ref[0:halo, :] = u[ts - halo:, :]

    ub = uc.astype(BF16)
    r_parts, i_parts = [], []
    for n in range(nb):
        ubn = ub[:, n * blk:(n + 1) * blk]
        r_parts.append(jnp.dot(ubn, wr_ref[n], preferred_element_type=F32))
        i_parts.append(jnp.dot(ubn, wi_ref[n], preferred_element_type=F32))
    r = jax.nn.sigmoid(jnp.concatenate(r_parts, axis=1) + br_ref[...])
    gi = jax.nn.sigmoid(jnp.concatenate(i_parts, axis=1) + bi_ref[...])
    lam = lam_ref[...]
    sp = jnp.maximum(-lam, 0.0) + jnp.log(1.0 + jnp.exp(-jnp.abs(lam)))
    log_a = (-LRU_C * r) * sp
    a_ref[...] = jnp.exp(log_a)
    th = jnp.tanh(log_a)
    b_ref[...] = jnp.sqrt(-2.0 * th / (1.0 - th)) * (gi * uc)

    row = lax.broadcasted_iota(I32, (SUBLANES, SCAN_STRIP), 0)
    strip = min(SCAN_STRIP, d)
    for c in range(d // strip):
        cols = pl.ds(c * strip, strip)

        def tile_step(t, h):
            rows = pl.ds(pl.multiple_of(t * SUBLANES, SUBLANES), SUBLANES)
            a = a_ref[rows, cols]
            bb = b_ref[rows, cols]
            for sh in (1, 2, 4):
                keep = row[:, :strip] >= sh
                a_prev = jnp.where(keep, pltpu.roll(a, sh, 0), 1.0)
                b_prev = jnp.where(keep, pltpu.roll(bb, sh, 0), 0.0)
                bb = a * b_prev + bb
                a = a * a_prev
            hs = a * h + bb
            b_ref[rows, cols] = hs
            return jnp.broadcast_to(hs[SUBLANES - 1:SUBLANES, :], (SUBLANES, strip))

        h_ref[:, cols] = lax.fori_loop(0, ts // SUBLANES, tile_step, h_ref[:, cols], unroll=2)

    y = y_ref[...].astype(F32)
    gelu = 0.5 * y * (1.0 + jnp.tanh(math.sqrt(2.0 / math.pi) * (y + 0.044715 * (y * y * y))))
    g = (b_ref[...] * gelu).astype(BF16)
    o_ref[...] = x_ref[...] + jnp.dot(g, wo_ref[...], preferred_element_type=F32)


def _rglru(proj, x3, conv_w, conv_b, w_rec, b_rec, w_inp, b_inp, lam, w_out):
    b, s, d = x3.shape
    d_rnn = w_out.shape[0]
    nb, blk, _ = w_rec.shape
    width = conv_w.shape[0]
    assert width - 1 <= SUBLANES
    ts = _tile(s, 256)
    proj3 = proj.reshape(b, s, 2 * d_rnn)
    row = lambda v: v.reshape(1, -1)
    const2 = lambda bi, i: (0, 0)
    const3 = lambda bi, i: (0, 0, 0)
    return pl.pallas_call(
        functools.partial(_rglru_kernel, conv_width=width),
        out_shape=jax.ShapeDtypeStruct((b, s, d), F32),
        grid=(b, s // ts),
        in_specs=[
            pl.BlockSpec((None, ts, d_rnn), lambda bi, i: (bi, i, 0)),
            pl.BlockSpec((None, ts, d_rnn), lambda bi, i: (bi, i, 1)),
            pl.BlockSpec((None, ts, d), lambda bi, i: (bi, i, 0)),
            pl.BlockSpec((width, d_rnn), const2),
            pl.BlockSpec((1, d_rnn), const2),
            pl.BlockSpec((nb, blk, blk), const3),
            pl.BlockSpec((1, d_rnn), const2),
            pl.BlockSpec((nb, blk, blk), const3),
            pl.BlockSpec((1, d_rnn), const2),
            pl.BlockSpec((1, d_rnn), const2),
            pl.BlockSpec((d_rnn, d), const2),
        ],
        out_specs=pl.BlockSpec((None, ts, d), lambda bi, i: (bi, i, 0)),
        scratch_shapes=[
            pltpu.VMEM((SUBLANES + ts, d_rnn), F32),
            pltpu.VMEM((ts, d_rnn), F32),
            pltpu.VMEM((ts, d_rnn), F32),
            pltpu.VMEM((SUBLANES, d_rnn), F32),
        ],
        compiler_params=_cparams(("parallel", "arbitrary"), 56),
        name="rglru",
    )(proj3, proj3, x3, conv_w, row(conv_b), w_rec.astype(BF16), row(b_rec),
      w_inp.astype(BF16), row(b_inp), row(lam), w_out.astype(BF16))


def _split3(v):
    v1 = v.astype(BF16)
    r1 = v - v1.astype(F32)
    v2 = r1.astype(BF16)
    v3 = (r1 - v2.astype(F32)).astype(BF16)
    return v1, v2, v3


def _dot_nt(a, b):
    return lax.dot_general(a, b, (((1,), (1,)), ((), ())), preferred_element_type=F32)


def _first_argmax(v, n):
    idx = lax.broadcasted_iota(I32, v.shape, 0)
    vmax = jnp.max(v, axis=0, keepdims=True)
    amax = jnp.min(jnp.where(v == vmax, idx, n), axis=0, keepdims=True)
    return amax, vmax


def _router_kernel(x_ref, g_ref, w1_ref, w2_ref, w3_ref, bias_ref, hn_ref, eid_ref, wt_ref,
                   rank_ref, cnt_ref, tri_ref, carry_ref, *, n_groups, per_group):
    tm = x_ref.shape[0]
    n_exp = n_groups * per_group

    @pl.when(pl.program_id(0) == 0)
    def _():
        r = lax.broadcasted_iota(I32, (tm, tm), 0)
        c = lax.broadcasted_iota(I32, (tm, tm), 1)
        tri_ref[...] = (r < c).astype(BF16)
        carry_ref[...] = jnp.zeros_like(carry_ref)

    hn = _rms(x_ref[...], g_ref[...])
    hn_ref[...] = hn
    h1, h2, h3 = _split3(hn)
    w1, w2, w3 = w1_ref[...], w2_ref[...], w3_ref[...]
    logits = (_dot_nt(w1, h1) + (_dot_nt(w1, h2) + _dot_nt(w2, h1))
              + (_dot_nt(w1, h3) + _dot_nt(w2, h2) + _dot_nt(w3, h1)))
    logits = logits + bias_ref[:, 0:1]

    gl = logits[0:n_groups, :]
    g_idx, g_max = _first_argmax(gl, n_groups)
    gp_top = 1.0 / jnp.sum(jnp.exp(gl - g_max), axis=0, keepdims=True)
    el = jnp.zeros((per_group, tm), F32)
    for g in range(n_groups):
        lo = n_groups + g * per_group
        el = jnp.where(g_idx == g, logits[lo:lo + per_group, :], el)
    e_max = jnp.max(el, axis=0, keepdims=True)
    ex = jnp.exp(el - e_max)
    ep = ex / jnp.sum(ex, axis=0, keepdims=True)
    e1, p1 = _first_argmax(ep, per_group)
    sub = lax.broadcasted_iota(I32, ep.shape, 0)
    e2, p2 = _first_argmax(jnp.where(sub == e1, -1.0, ep), per_group)
    denom = p1 + p2
    eid1 = g_idx * per_group + e1
    eid2 = g_idx * per_group + e2
    eid_ref[...] = jnp.concatenate([eid1, eid2], axis=0)
    wt_ref[...] = jnp.concatenate([gp_top * p1 / denom, gp_top * p2 / denom], axis=0)

    e_iota = lax.broadcasted_iota(I32, (n_exp, tm), 0)
    hot1 = e_iota == eid1
    hot2 = e_iota == eid2
    chosen = jnp.logical_or(hot1, hot2)
    before = jnp.dot(chosen.astype(BF16), tri_ref[...], preferred_element_type=F32)
    base = (before + carry_ref[:, 0:1]).astype(I32)
    rank1 = jnp.sum(jnp.where(hot1, base, 0), axis=0, keepdims=True)
    rank2 = jnp.sum(jnp.where(hot2, base, 0), axis=0, keepdims=True)
    rank_ref[...] = jnp.concatenate([rank1, rank2], axis=0)
    carry_ref[...] = carry_ref[...] + jnp.sum(chosen.astype(F32), axis=1, keepdims=True)
    cnt_ref[...] = carry_ref[...].astype(I32)


def _router(x2, gain, w_group, b_group, w_router, b_router):
    n, d = x2.shape
    n_groups = w_group.shape[1]
    n_exp = w_router.shape[1]
    per_group = n_exp // n_groups
    assert per_group == SUBLANES and n_groups == SUBLANES
    rows = n_groups + n_exp
    rows_p = -(-rows // LANES) * LANES
    w_t = jnp.concatenate([w_group, w_router], axis=1).T
    w_t = jnp.pad(w_t, ((0, rows_p - rows), (0, 0)))
    w1, w2, w3 = _split3(w_t)
    bias = jnp.pad(jnp.concatenate([b_group, b_router]), (0, rows_p - rows))
    bias = jnp.broadcast_to(bias[:, None], (rows_p, LANES))
    tm = _tile(n, 512)
    const = lambda i: (0, 0)
    return pl.pallas_call(
        functools.partial(_router_kernel, n_groups=n_groups, per_group=per_group),
        out_shape=(
            jax.ShapeDtypeStruct((n, d), F32),
            jax.ShapeDtypeStruct((TOP_K, n), I32),
            jax.ShapeDtypeStruct((TOP_K, n), F32),
            jax.ShapeDtypeStruct((TOP_K, n), I32),
            jax.ShapeDtypeStruct((n_exp, LANES), I32),
        ),
        grid=(n // tm,),
        in_specs=[
            pl.BlockSpec((tm, d), lambda i: (i, 0)),
            pl.BlockSpec((1, d), const),
            pl.BlockSpec((rows_p, d), const),
            pl.BlockSpec((rows_p, d), const),
            pl.BlockSpec((rows_p, d), const),
            pl.BlockSpec((rows_p, LANES), const),
        ],
        out_specs=(
            pl.BlockSpec((tm, d), lambda i: (i, 0)),
            pl.BlockSpec((TOP_K, tm), lambda i: (0, i)),
            pl.BlockSpec((TOP_K, tm), lambda i: (0, i)),
            pl.BlockSpec((TOP_K, tm), lambda i: (0, i)),
            pl.BlockSpec((n_exp, LANES), const),
        ),
        scratch_shapes=[pltpu.VMEM((tm, tm), BF16), pltpu.VMEM((n_exp, LANES), F32)],
        compiler_params=_cparams(("arbitrary",), 40),
        name="moe_router",
    )(x2, gain.reshape(1, d), w1, w2, w3, bias)


def _dest_kernel(eid_ref, rank_ref, start_ref, dest_ref):
    eid = eid_ref[...]
    n_exp = start_ref.shape[0]
    dest = rank_ref[...]
    for k in range(eid.shape[0]):
        hot = lax.broadcasted_iota(I32, (n_exp, eid.shape[1]), 0) == eid[k:k + 1, :]
        off = jnp.sum(jnp.where(hot, start_ref[:, 0:1], 0), axis=0, keepdims=True)
        dest_ref[k:k + 1, :] = dest[k:k + 1, :] + off


def _dest_rows(eid, rank, seg_start):
    k, n = eid.shape
    n_exp = seg_start.shape[0]
    tm = _tile(n, 2048)
    start = jnp.broadcast_to(seg_start[:, None], (n_exp, LANES)).astype(I32)
    return pl.pallas_call(
        _dest_kernel,
        out_shape=jax.ShapeDtypeStruct((k, n), I32),
        grid=(n // tm,),
        in_specs=[
            pl.BlockSpec((k, tm), lambda i: (0, i)),
            pl.BlockSpec((k, tm), lambda i: (0, i)),
            pl.BlockSpec((n_exp, LANES), lambda i: (0, 0)),
        ],
        out_specs=pl.BlockSpec((k, tm), lambda i: (0, i)),
        compiler_params=_cparams(("parallel",), 16),
        name="moe_dest",
    )(eid, rank, start)


def _dispatch_kernel(nv_ref, dest_ref, hn_ref, xs_ref, zero_ref, sem, zsem, *, chunk_rows):
    k, tm = dest_ref.shape[1], dest_ref.shape[2]
    n_chunks = nv_ref.shape[0]

    @pl.when(pl.program_id(0) == 0)
    def _():
        zero_ref[...] = jnp.zeros_like(zero_ref)

        def zero_copy(c):
            rows = pl.ds(pl.multiple_of(c * chunk_rows, chunk_rows), chunk_rows)
            return pltpu.make_async_copy(zero_ref, xs_ref.at[rows], zsem)

        def start(c, carry):
            @pl.when(nv_ref[c] < chunk_rows)
            def _():
                zero_copy(c).start()
            return carry

        def finish(c, carry):
            @pl.when(nv_ref[c] < chunk_rows)
            def _():
                zero_copy(c).wait()
            return carry

        lax.fori_loop(0, n_chunks, start, 0)
        lax.fori_loop(0, n_chunks, finish, 0)


    def issue(t, c):
        for kk in range(k):
            pltpu.make_async_copy(hn_ref.at[pl.ds(t, 1)],
                                  xs_ref.at[pl.ds(dest_ref[0, kk, t], 1)], sem).start()
        return c

    lax.fori_loop(0, tm, issue, 0, unroll=8)
    for kk in range(k):
        pltpu.make_async_copy(hn_ref, xs_ref.at[pl.ds(0, tm)], sem).wait()


def _dispatch(dest, hn, n_valid, n_rows, chunk_rows):
    k, n = dest.shape
    d = hn.shape[1]
    tm = _tile(n, 512)
    dest3 = dest.reshape(k, n // tm, tm).transpose(1, 0, 2)
    grid_spec = pltpu.PrefetchScalarGridSpec(
        num_scalar_prefetch=1,
        grid=(n // tm,),
        in_specs=[
            pl.BlockSpec((1, k, tm), lambda i, nv: (i, 0, 0), memory_space=pltpu.SMEM),
            pl.BlockSpec((tm, d), lambda i, nv: (i, 0)),
        ],
        out_specs=pl.BlockSpec(memory_space=pl.ANY),
        scratch_shapes=[pltpu.VMEM((chunk_rows, d), hn.dtype), pltpu.SemaphoreType.DMA(()),
                        pltpu.SemaphoreType.DMA(())],
    )
    return pl.pallas_call(
        functools.partial(_dispatch_kernel, chunk_rows=chunk_rows),
        out_shape=jax.ShapeDtypeStruct((n_rows, d), hn.dtype),
        grid_spec=grid_spec,
        compiler_params=_cparams(("arbitrary",), 24),
        name="moe_dispatch",
    )(n_valid, dest3, hn)


def _expert_kernel(ce_ref, nu_ref, fresh_ref, xs_ref, wi_ref, wo_ref, ys_ref,
                   wib_ref, wob_ref, *, d_expert):
    i = pl.program_id(0)

    @pl.when(i < nu_ref[0])
    def _():
        @pl.when(fresh_ref[i] == 1)
        def _():
            wib_ref[...] = wi_ref[...].astype(BF16)
            wob_ref[...] = wo_ref[...].astype(BF16)

        gu = jnp.dot(xs_ref[...].astype(BF16), wib_ref[...], preferred_element_type=F32)
        gate = gu[:, :d_expert]
        act = (gate * jax.nn.sigmoid(gate) * gu[:, d_expert:]).astype(BF16)
        ys_ref[...] = jnp.dot(act, wob_ref[...], preferred_element_type=F32)

    @pl.when(i >= nu_ref[0])
    def _():
        ys_ref[...] = jnp.zeros_like(ys_ref)


def _experts(xs, chunk_e, n_used, fresh, w_in_all, w_out_all, layer, chunk_rows):
    n_rows, d = xs.shape
    d_expert = w_out_all.shape[2]
    n_chunks = n_rows // chunk_rows
    xs_map = lambda i, ce, nu, fr: (jnp.minimum(i, jnp.maximum(nu[0] - 1, 0)), 0)
    w_map = lambda i, ce, nu, fr: (layer, ce[i], 0, 0)
    grid_spec = pltpu.PrefetchScalarGridSpec(
        num_scalar_prefetch=3,
        grid=(n_chunks,),
        in_specs=[
            pl.BlockSpec((chunk_rows, d), xs_map),
            pl.BlockSpec((None, None, d, 2 * d_expert), w_map),
            pl.BlockSpec((None, None, d_expert, d), w_map),
        ],
        out_specs=pl.BlockSpec((chunk_rows, d), lambda i, ce, nu, fr: (i, 0)),
        scratch_shapes=[pltpu.VMEM((d, 2 * d_expert), BF16), pltpu.VMEM((d_expert, d), BF16)],
    )
    return pl.pallas_call(
        functools.partial(_expert_kernel, d_expert=d_expert),
        out_shape=jax.ShapeDtypeStruct((n_rows, d), F32),
        grid_spec=grid_spec,
        compiler_params=_cparams(("arbitrary",), 56),
        name="moe_experts",
    )(chunk_e, n_used, fresh, xs, w_in_all, w_out_all)


def _combine_kernel(dest_ref, next_ref, x_ref, wt_ref, g_ref, ys_ref, o_ref, buf_ref, sem_ref, *,
                    final_norm, n_blocks):
    k, tm = dest_ref.shape[1], dest_ref.shape[2]
    i = pl.program_id(0)

    def issue_block(d_ref, slot):
        def body(t, c):
            for kk in range(k):
                pltpu.make_async_copy(ys_ref.at[pl.ds(d_ref[0, kk, t], 1)],
                                      buf_ref.at[slot, kk, pl.ds(t, 1)], sem_ref.at[slot]).start()
            return c
        lax.fori_loop(0, tm, body, 0, unroll=8)

    @pl.when(i == 0)
    def _():
        issue_block(dest_ref, 0)

    @pl.when(i + 1 < n_blocks)
    def _():
        issue_block(next_ref, (i + 1) % 2)

    slot = i % 2
    for kk in range(k):
        pltpu.make_async_copy(ys_ref.at[pl.ds(0, tm)], buf_ref.at[slot, kk], sem_ref.at[slot]).wait()

    out = x_ref[...]
    for kk in range(k):
        out = out + wt_ref[:, kk:kk + 1] * buf_ref[slot, kk]
    if final_norm:
        out = _rms(out, g_ref[...])
    o_ref[...] = out


def _combine(dest, x2, wts, ys, gain, final_norm):
    k, n = dest.shape
    d = x2.shape[1]
    tm = _tile(n, 256)
    n_blocks = n // tm
    dest3 = dest.reshape(k, n_blocks, tm).transpose(1, 0, 2)
    return pl.pallas_call(
        functools.partial(_combine_kernel, final_norm=final_norm, n_blocks=n_blocks),
        out_shape=jax.ShapeDtypeStruct((n, d), F32),
        grid=(n_blocks,),
        in_specs=[
            pl.BlockSpec((1, k, tm), lambda i: (i, 0, 0), memory_space=pltpu.SMEM),
            pl.BlockSpec((1, k, tm), lambda i: (jnp.minimum(i + 1, n_blocks - 1), 0, 0),
                         memory_space=pltpu.SMEM),
            pl.BlockSpec((tm, d), lambda i: (i, 0)),
            pl.BlockSpec((tm, k), lambda i: (i, 0)),
            pl.BlockSpec((1, d), lambda i: (0, 0)),
            pl.BlockSpec(memory_space=pl.ANY),
        ],
        out_specs=pl.BlockSpec((tm, d), lambda i: (i, 0)),
        scratch_shapes=[pltpu.VMEM((2, k, tm, d), F32), pltpu.SemaphoreType.DMA((2,))],
        compiler_params=_cparams(("arbitrary",), 40),
        name="moe_combine",
    )(dest3, dest3, x2, wts.T, gain.reshape(1, d), ys)


EXPERT_CHUNK_ROWS = 256


def _moe(x2, norm_gain, w_group, b_group, w_router, b_router, w_in_all, w_out_all, layer, out_gain,
         final_norm):
    n, d = x2.shape
    n_exp = w_router.shape[1]
    chunk = EXPERT_CHUNK_ROWS
    hn, eid, wts, rank, counts = _router(x2, norm_gain, w_group, b_group, w_router, b_router)
    counts = counts[:, 0]
    padded = (counts + chunk - 1) // chunk * chunk
    seg_end = jnp.cumsum(padded)
    seg_start = seg_end - padded
    n_rows = (-(-(n * TOP_K) // chunk) + n_exp) * chunk
    n_chunks = n_rows // chunk
    chunk_start = jnp.arange(n_chunks, dtype=I32) * chunk
    n_used = (seg_end[-1:] // chunk).astype(I32)
    live_start = jnp.minimum(chunk_start, jnp.maximum(seg_end[-1] - chunk, 0))
    chunk_e = jnp.sum(seg_end[None, :] <= live_start[:, None], axis=1).astype(I32)
    chunk_e = jnp.minimum(chunk_e, n_exp - 1)
    n_valid = jnp.clip(counts[chunk_e] - (chunk_start - seg_start[chunk_e]), 0, chunk).astype(I32)
    fresh = jnp.concatenate([jnp.ones((1,), I32), (chunk_e[1:] != chunk_e[:-1]).astype(I32)])
    dest = _dest_rows(eid, rank, seg_start)
    xs = _dispatch(dest, hn, n_valid, n_rows, chunk)
    ys = _experts(xs, chunk_e, n_used, fresh, w_in_all, w_out_all, layer, chunk)
    return _combine(dest, x2, wts, ys, out_gain, final_norm)


def _forget_kernel(x_ref, g_ref, w_ref, b_ref, aux_ref, end_ref, tri_ref, carry_ref, *, n_heads):
    ts = x_ref.shape[0]

    @pl.when(pl.program_id(1) == 0)
    def _():
        r = lax.broadcasted_iota(I32, (ts, ts), 0)
        c = lax.broadcasted_iota(I32, (ts, ts), 1)
        tri_ref[...] = (c <= r).astype(BF16)
        carry_ref[...] = jnp.zeros_like(carry_ref)

    hn = _rms(x_ref[...], g_ref[...]).astype(BF16)
    f = jnp.dot(hn, w_ref[...], preferred_element_type=F32) + b_ref[...]
    logf = jnp.minimum(f, 0.0) - jnp.log(1.0 + jnp.exp(-jnp.abs(f)))
    l1, l2, l3 = _split3(logf)
    tri = tri_ref[...]
    cum = (jnp.dot(tri, l1, preferred_element_type=F32)
           + jnp.dot(tri, l2, preferred_element_type=F32)
           + jnp.dot(tri, l3, preferred_element_type=F32)) + carry_ref[0:1, :]
    carry_ref[...] = jnp.broadcast_to(cum[ts - 1:ts, :], carry_ref.shape)
    end_ref[...] = carry_ref[...] * (-LOG2E)
    c1, c2, c3 = [c.astype(F32) for c in _split3(cum * (-LOG2E))]
    lane = lax.broadcasted_iota(I32, (ts, LANES), 1)
    for h in range(n_heads):
        col = lambda v: jnp.broadcast_to(v[:, h:h + 1], (ts, LANES))
        aux = jnp.where(lane == 0, col(c1), jnp.where(lane == 1, col(c2),
                        jnp.where(lane == 2, col(c3), 0.0)))
        aux_ref[h] = aux.astype(BF16)


def _forget_aux(x3, gain, w_f, b_f):
    b, s, d = x3.shape
    n_heads = w_f.shape[1]
    assert n_heads <= LANES
    w_p = jnp.pad(w_f, ((0, 0), (0, LANES - n_heads))).astype(BF16)
    b_p = jnp.pad(b_f, (0, LANES - n_heads)).reshape(1, LANES)
    ts = _tile(s, ATTN_BLOCK)
    return pl.pallas_call(
        functools.partial(_forget_kernel, n_heads=n_heads),
        out_shape=(jax.ShapeDtypeStruct((b, n_heads, s, LANES), BF16),
                   jax.ShapeDtypeStruct((b, s // ts, SUBLANES, LANES), F32)),
        grid=(b, s // ts),
        in_specs=[
            pl.BlockSpec((None, ts, d), lambda bi, i: (bi, i, 0)),
            pl.BlockSpec((1, d), lambda bi, i: (0, 0)),
            pl.BlockSpec((d, LANES), lambda bi, i: (0, 0)),
            pl.BlockSpec((1, LANES), lambda bi, i: (0, 0)),
        ],
        out_specs=(pl.BlockSpec((None, n_heads, ts, LANES), lambda bi, i: (bi, 0, i, 0)),
                   pl.BlockSpec((None, None, SUBLANES, LANES), lambda bi, i: (bi, i, 0, 0))),
        scratch_shapes=[pltpu.VMEM((ts, ts), BF16), pltpu.VMEM((SUBLANES, LANES), F32)],
        compiler_params=_cparams(("parallel", "arbitrary"), 32),
        name="forget_cumsum",
    )(x3, gain.reshape(1, d), w_p, b_p)


ATTN_BLOCK = 512
ATTN_GROUP_BLOCKS = 8
FINITE_LIMIT = 3.0e38
SKIP_MARGIN = 160.0


def _attn_kernel(end_ref, q_ref, k_ref, aux_ref, vt_ref, o_ref, qp_ref, m_ref, l_ref, acc_ref,
                 kmax_ref):
    t, hd = q_ref.shape
    i = pl.program_id(2)
    lane = lax.broadcasted_iota(I32, (t, LANES), 1)
    qp_ref[:, 0:hd] = q_ref[...]
    qp_ref[:, hd:hd + LANES] = jnp.where(lane < 3, 1.0, 0.0).astype(BF16)

    def max_row_norm(x):
        xf = x.astype(F32)
        return jnp.sum(xf * xf, axis=1, keepdims=True)

    @pl.when(i == 0)
    def _():
        def body(c, best):
            rows = pl.ds(pl.multiple_of(c * t, t), t)
            return jnp.maximum(best, max_row_norm(k_ref[rows, :]))
        best = lax.fori_loop(0, k_ref.shape[0] // t, body, jnp.zeros((t, 1), F32))
        kmax_ref[0] = jnp.sqrt(jnp.max(best))

    reach = jnp.sqrt(jnp.max(max_row_norm(q_ref[...]))) * kmax_ref[0]

    def alive(block):
        bias_top = end_ref[0, jnp.maximum(block, 0)]
        return jnp.logical_and(block >= 0,
                               reach + bias_top - jnp.min(m_ref[...]) > -SKIP_MARGIN)

    def scores(start, size, masked):
        rows = pl.ds(pl.multiple_of(start, t), size)
        kp = jnp.concatenate([k_ref[rows, :], aux_ref[rows, :]], axis=1)
        s_t = _dot_nt(kp, qp_ref[...])
        if masked:
            kr = lax.broadcasted_iota(I32, (size, t), 0)
            qc = lax.broadcasted_iota(I32, (size, t), 1)
            s_t = jnp.where(kr <= qc, s_t, MASK_VALUE)
        return s_t, rows

    def exact_block(start, masked):
        s_t, rows = scores(start, t, masked)
        m_old = m_ref[...]
        m_new = jnp.maximum(m_old, jnp.max(s_t, axis=0, keepdims=True))
        alpha = jnp.exp2(m_old - m_new)
        p = jnp.exp2(s_t - m_new)
        l_ref[...] = alpha * l_ref[...] + jnp.sum(p, axis=0, keepdims=True)
        pv = jnp.dot(vt_ref[:, rows], p.astype(BF16), preferred_element_type=F32)
        acc_ref[...] = alpha * acc_ref[...] + pv
        m_ref[...] = m_new

    def lagged_group(start, size):
        s_t, rows = scores(start, size, False)
        m_old = m_ref[...]
        p = jnp.exp2(s_t - m_old)
        pv = jnp.dot(vt_ref[:, rows], p.astype(BF16), preferred_element_type=F32)
        m_new = jnp.maximum(m_old, jnp.max(s_t, axis=0, keepdims=True))
        alpha = jnp.exp2(m_old - m_new)
        l_ref[...] = (l_ref[...] + jnp.sum(p, axis=0, keepdims=True)) * alpha
        acc_ref[...] = (acc_ref[...] + pv) * alpha
        m_ref[...] = m_new

    def reset():
        m_ref[...] = jnp.full_like(m_ref, MASK_VALUE)
        l_ref[...] = jnp.zeros_like(l_ref)
        acc_ref[...] = jnp.zeros_like(acc_ref)

    def finish():
        out = acc_ref[...] / l_ref[...]
        o_ref[...] = out.T.astype(o_ref.dtype)
        return out

    gb = ATTN_GROUP_BLOCKS
    n_groups = i // gb
    rest = i - n_groups * gb

    def diagonal_with_group():
        s_t, rows = scores((i - gb) * t, (gb + 1) * t, False)
        kr = lax.broadcasted_iota(I32, (t, t), 0)
        qc = lax.broadcasted_iota(I32, (t, t), 1)
        diag = jnp.where(kr <= qc, s_t[gb * t:, :], MASK_VALUE)
        older = s_t[:gb * t, :]
        m_diag = jnp.max(diag, axis=0, keepdims=True)
        p = jnp.exp2(jnp.concatenate([older, diag], axis=0) - m_diag)
        pv = jnp.dot(vt_ref[:, rows], p.astype(BF16), preferred_element_type=F32)
        m_new = jnp.maximum(m_diag, jnp.max(older, axis=0, keepdims=True))
        alpha = jnp.exp2(m_diag - m_new)
        l_ref[...] = jnp.sum(p, axis=0, keepdims=True) * alpha
        acc_ref[...] = pv * alpha
        m_ref[...] = m_new

    reset()
    fused = i >= gb

    @pl.when(fused)
    def _():
        diagonal_with_group()

    @pl.when(jnp.logical_not(fused))
    def _():
        exact_block(i * t, True)

    first_group = jnp.where(fused, 1, 0)

    def group_cond(c):
        g, live = c
        return jnp.logical_and(g < n_groups, live)

    def group_body(c):
        g, _ = c
        first = i - (g + 1) * gb
        lagged_group(first * t, gb * t)
        return g + 1, alive(first - 1)

    _, live = lax.while_loop(group_cond, group_body,
                             (first_group, alive(i - 1 - first_group * gb)))

    size = gb // 2
    while size >= 1:
        top = rest & (2 * size - 1)
        use = jnp.logical_and((rest & size) != 0, jnp.logical_and(live, alive(top - 1)))

        @pl.when(use)
        def _(top=top, size=size):
            lagged_group((top - size) * t, size * t)

        size //= 2

    out = finish()
    overflowed = jnp.max(jnp.where(jnp.abs(out) < FINITE_LIMIT, 0.0, 1.0)) > 0.0

    @pl.when(overflowed)
    def _():
        reset()

        def exact_body(j, c):
            exact_block(j * t, False)
            return c

        lax.fori_loop(0, i, exact_body, 0)
        exact_block(i * t, True)
        finish()


def _attention(qg3, k3, aux, vt, block_end, n_heads):
    b, s, _ = k3.shape
    hd = k3.shape[2] // n_heads
    assert hd == LANES
    t = _tile(s, ATTN_BLOCK)
    end_tab = block_end[:, :, 0, :n_heads].transpose(0, 2, 1).reshape(b, n_heads, 1, s // t)
    return pl.pallas_call(
        _attn_kernel,
        out_shape=jax.ShapeDtypeStruct((b, s, n_heads * hd), BF16),
        grid=(b, n_heads, s // t),
        in_specs=[
            pl.BlockSpec((None, None, 1, s // t), lambda bi, h, i: (bi, h, 0, 0),
                         memory_space=pltpu.SMEM),
            pl.BlockSpec((None, t, hd), lambda bi, h, i: (bi, i, h)),
            pl.BlockSpec((None, s, hd), lambda bi, h, i: (bi, 0, h)),
            pl.BlockSpec((None, None, s, LANES), lambda bi, h, i: (bi, h, 0, 0)),
            pl.BlockSpec((None, hd, s), lambda bi, h, i: (bi, h, 0)),
        ],
        out_specs=pl.BlockSpec((None, t, hd), lambda bi, h, i: (bi, i, h)),
        scratch_shapes=[
            pltpu.VMEM((t, hd + LANES), BF16),
            pltpu.VMEM((1, t), F32),
            pltpu.VMEM((1, t), F32),
            pltpu.VMEM((hd, t), F32),
            pltpu.SMEM((1,), F32),
        ],
        compiler_params=_cparams(("parallel", "parallel", "arbitrary"), 56),
        name="fox_attention",
    )(end_tab, qg3, k3, aux, vt)


def _gated_out_kernel(o_ref, gate_ref, x_ref, w_ref, out_ref):
    g = o_ref[...].astype(F32) * jax.nn.sigmoid(gate_ref[...].astype(F32))
    out_ref[...] = x_ref[...] + jnp.dot(g.astype(BF16), w_ref[...], preferred_element_type=F32)


def _gated_out(o2, qg, x2, w_o):
    n, d = x2.shape
    hd_all = o2.shape[1]
    tm = _tile(n, 512)
    return pl.pallas_call(
        _gated_out_kernel,
        out_shape=jax.ShapeDtypeStruct((n, d), F32),
        grid=(n // tm,),
        in_specs=[
            pl.BlockSpec((tm, hd_all), lambda i: (i, 0)),
            pl.BlockSpec((tm, hd_all), lambda i: (i, 1)),
            pl.BlockSpec((tm, d), lambda i: (i, 0)),
            pl.BlockSpec((hd_all, d), lambda i: (0, 0)),
        ],
        out_specs=pl.BlockSpec((tm, d), lambda i: (i, 0)),
        compiler_params=_cparams(("parallel",), 48),
        name="gated_out_proj",
    )(o2, qg, x2, w_o)


def kernel(x, a_norm, a_w_in, a_conv_w, a_conv_b, a_w_rec, a_b_rec, a_w_inp, a_b_inp, a_lambda, a_w_out, kv_norm, kv_w, kv_b_forget, b_norm, b_w_qg, b_w_o, m_norm, m_w_group, m_b_group, m_w_router, m_b_router, m_w_in, m_w_out, final_norm):
    b, s, d = x.shape
    n = b * s
    depth = m_norm.shape[0]
    n_a = a_norm.shape[0]
    n_heads = kv_b_forget.shape[0]
    hd_all = b_w_o.shape[1]
    head_dim = hd_all // n_heads
    x2 = x.reshape(n, d)
    k3 = aux = vt = block_end = None
    for layer in range(depth):
        if layer < n_a:
            i = layer
            d_rnn = a_w_out.shape[1]
            proj = _norm_proj(x2, a_norm[i], a_w_in[i].astype(BF16), jnp.ones((2 * d_rnn,), F32),
                              BF16, "rglru_in_proj")
            x2 = _rglru(proj, x2.reshape(b, s, d), a_conv_w[i], a_conv_b[i], a_w_rec[i], a_b_rec[i],
                        a_w_inp[i], a_b_inp[i], a_lambda[i], a_w_out[i]).reshape(n, d)
        else:
            j = layer - n_a
            q_scale = jnp.concatenate([jnp.full((hd_all,), head_dim ** -0.5 * LOG2E, F32),
                                       jnp.ones((hd_all,), F32)])
            qg = _norm_proj(x2, b_norm[j], b_w_qg[j].astype(BF16), q_scale, BF16, "fox_qg_proj")
            o = _attention(qg.reshape(b, s, 2 * hd_all), k3, aux, vt, block_end, n_heads)
            x2 = _gated_out(o.reshape(n, hd_all), qg, x2, b_w_o[j].astype(BF16))
        last = layer == depth - 1
        x2 = _moe(x2, m_norm[layer], m_w_group[layer], m_b_group[layer], m_w_router[layer],
                  m_b_router[layer], m_w_in, m_w_out, layer, final_norm, last)
        if layer == n_a - 1:
            x3 = x2.reshape(b, s, d)
            k3 = _norm_proj(x2, kv_norm, kv_w[:, :hd_all].astype(BF16), jnp.ones((hd_all,), F32),
                            BF16, "shared_k_proj").reshape(b, s, hd_all)
            vt = _norm_proj_t(x3, kv_norm, kv_w[:, hd_all:2 * hd_all].T.astype(BF16), BF16,
                              "shared_vt_proj")
            aux, block_end = _forget_aux(x3, kv_norm, kv_w[:, 2 * hd_all:], kv_b_forget)
    if depth == 0:
        x2 = _rms(x2, final_norm)
    return x2.reshape(b, s, d)
```

```python
import functools
import math

import jax
import jax.numpy as jnp
from jax import lax
from jax.experimental import pallas as pl
from jax.experimental.pallas import tpu as pltpu

F32 = jnp.float32
BF16 = jnp.bfloat16
I32 = jnp.int32

EPS = 1e-6
LRU_C = 8.0
TOP_K = 2
LOG2E = 1.4426950408889634
MASK_VALUE = -1e30

V7X_VMEM_BYTES = 64 * 1024 * 1024
SUBLANES = 8
LANES = 128
MIB = 1024 * 1024


def _cparams(semantics, vmem_mib):
    assert vmem_mib * MIB < V7X_VMEM_BYTES
    return pltpu.CompilerParams(dimension_semantics=semantics, vmem_limit_bytes=vmem_mib * MIB)


def _tile(dim, pref):
    t = min(dim, pref)
    assert dim % t == 0, (dim, pref)
    return t


def _rms(x, gain):
    return x * lax.rsqrt(jnp.mean(x * x, axis=-1, keepdims=True) + EPS) * gain


def _norm_proj_kernel(x_ref, g_ref, w_ref, s_ref, o_ref, hn_ref):
    @pl.when(pl.program_id(1) == 0)
    def _():
        hn_ref[...] = _rms(x_ref[...], g_ref[...]).astype(BF16)

    acc = jnp.dot(hn_ref[...], w_ref[...], preferred_element_type=F32)
    o_ref[...] = (acc * s_ref[...]).astype(o_ref.dtype)


def _norm_proj(x, gain, w, col_scale, out_dtype, name):
    n, d = x.shape
    n_out = w.shape[1]
    tm = _tile(n, 1024)
    tn = _tile(n_out, 1024)
    return pl.pallas_call(
        _norm_proj_kernel,
        out_shape=jax.ShapeDtypeStruct((n, n_out), out_dtype),
        grid=(n // tm, n_out // tn),
        in_specs=[
            pl.BlockSpec((tm, d), lambda i, j: (i, 0)),
            pl.BlockSpec((1, d), lambda i, j: (0, 0)),
            pl.BlockSpec((d, tn), lambda i, j: (0, j)),
            pl.BlockSpec((1, tn), lambda i, j: (0, j)),
        ],
        out_specs=pl.BlockSpec((tm, tn), lambda i, j: (i, j)),
        scratch_shapes=[pltpu.VMEM((tm, d), BF16)],
        compiler_params=_cparams(("parallel", "arbitrary"), 48),
        name=name,
    )(x, gain.reshape(1, d), w, col_scale.reshape(1, n_out))


def _norm_proj_t_kernel(x_ref, g_ref, wt_ref, o_ref, hn_ref):
    @pl.when(pl.program_id(2) == 0)
    def _():
        hn_ref[...] = _rms(x_ref[0], g_ref[...]).astype(BF16)

    acc = lax.dot_general(wt_ref[...], hn_ref[...], (((1,), (1,)), ((), ())),
                          preferred_element_type=F32)
    o_ref[0] = acc.astype(o_ref.dtype)


def _norm_proj_t(x3, gain, w_t, out_dtype, name):
    b, s, d = x3.shape
    n_out = w_t.shape[0]
    tm = _tile(s, 1024)
    tn = _tile(n_out, 1024)
    return pl.pallas_call(
        _norm_proj_t_kernel,
        out_shape=jax.ShapeDtypeStruct((b, n_out, s), out_dtype),
        grid=(b, s // tm, n_out // tn),
        in_specs=[
            pl.BlockSpec((1, tm, d), lambda bi, i, j: (bi, i, 0)),
            pl.BlockSpec((1, d), lambda bi, i, j: (0, 0)),
            pl.BlockSpec((tn, d), lambda bi, i, j: (j, 0)),
        ],
        out_specs=pl.BlockSpec((1, tn, tm), lambda bi, i, j: (bi, j, i)),
        scratch_shapes=[pltpu.VMEM((tm, d), BF16)],
        compiler_params=_cparams(("parallel", "parallel", "arbitrary"), 48),
        name=name,
    )(x3, gain.reshape(1, d), w_t)


SCAN_STRIP = 512


def _rglru_kernel(u_ref, y_ref, x_ref, cw_ref, cb_ref, wr_ref, br_ref, wi_ref, bi_ref,
                  lam_ref, wo_ref, o_ref, ubuf_ref, a_ref, b_ref, h_ref, *, conv_width):
    ts, d = u_ref.shape
    nb, blk, _ = wr_ref.shape
    halo = SUBLANES

    @pl.when(pl.program_id(1) == 0)
    def _():
        ubuf_ref[0:halo, :] = jnp.zeros((halo, d), F32)
        h_ref[...] = jnp.zeros_like(h_ref)

    u = u_ref[...].astype(F32)
    ubuf_ref[halo:halo + ts, :] = u
    uc = u * cw_ref[conv_width - 1:conv_width, :] + cb_ref[...]
    for j in range(conv_width - 1):
        shift = conv_width - 1 - j
        uc = uc + ubuf_ref[halo - shift:halo - shift + ts, :] * cw_ref[j:j + 1, :]
    ubuf_ref[0:halo, :] = u[ts - halo:, :]

    ub = uc.astype(BF16)
    r_parts, i_parts = [], []
    for n in range(nb):
        ubn = ub[:, n * blk:(n + 1) * blk]
        r_parts.append(jnp.dot(ubn, wr_ref[n], preferred_element_type=F32))
        i_parts.append(jnp.dot(ubn, wi_ref[n], preferred_element_type=F32))
    r = jax.nn.sigmoid(jnp.concatenate(r_parts, axis=1) + br_ref[...])
    gi = jax.nn.sigmoid(jnp.concatenate(i_parts, axis=1) + bi_ref[...])
    lam = lam_ref[...]
    sp = jnp.maximum(-lam, 0.0) + jnp.log(1.0 + jnp.exp(-jnp.abs(lam)))
    log_a = (-LRU_C * r) * sp
    a_ref[...] = jnp.exp(log_a)
    th = jnp.tanh(log_a)
    b_ref[...] = jnp.sqrt(-2.0 * th / (1.0 - th)) * (gi * uc)

    row = lax.broadcasted_iota(I32, (SUBLANES, SCAN_STRIP), 0)
    strip = min(SCAN_STRIP, d)
    for c in range(d // strip):
        cols = pl.ds(c * strip, strip)

        def tile_step(t, h):
            rows = pl.ds(pl.multiple_of(t * SUBLANES, SUBLANES), SUBLANES)
            a = a_ref[rows, cols]
            bb = b_ref[rows, cols]
            for sh in (1, 2, 4):
                keep = row[:, :strip] >= sh
                a_prev = jnp.where(keep, pltpu.roll(a, sh, 0), 1.0)
                b_prev = jnp.where(keep, pltpu.roll(bb, sh, 0), 0.0)
                bb = a * b_prev + bb
                a = a * a_prev
            hs = a * h + bb
            b_ref[rows, cols] = hs
            return jnp.broadcast_to(hs[SUBLANES - 1:SUBLANES, :], (SUBLANES, strip))

        h_ref[:, cols] = lax.fori_loop(0, ts // SUBLANES, tile_step, h_ref[:, cols], unroll=2)

    y = y_ref[...].astype(F32)
    gelu = 0.5 * y * (1.0 + jnp.tanh(math.sqrt(2.0 / math.pi) * (y + 0.044715 * (y * y * y))))
    g = (b_ref[...] * gelu).astype(BF16)
    o_ref[...] = x_ref[...] + jnp.dot(g, wo_ref[...], preferred_element_type=F32)


def _rglru(proj, x3, conv_w, conv_b, w_rec, b_rec, w_inp, b_inp, lam, w_out):
    b, s, d = x3.shape
    d_rnn = w_out.shape[0]
    nb, blk, _ = w_rec.shape
    width = conv_w.shape[0]
    assert width - 1 <= SUBLANES
    ts = _tile(s, 256)
    proj3 = proj.reshape(b, s, 2 * d_rnn)
    row = lambda v: v.reshape(1, -1)
    const2 = lambda bi, i: (0, 0)
    const3 = lambda bi, i: (0, 0, 0)
    return pl.pallas_call(
        functools.partial(_rglru_kernel, conv_width=width),
        out_shape=jax.ShapeDtypeStruct((b, s, d), F32),
        grid=(b, s // ts),
        in_specs=[
            pl.BlockSpec((None, ts, d_rnn), lambda bi, i: (bi, i, 0)),
            pl.BlockSpec((None, ts, d_rnn), lambda bi, i: (bi, i, 1)),
            pl.BlockSpec((None, ts, d), lambda bi, i: (bi, i, 0)),
            pl.BlockSpec((width, d_rnn), const2),
            pl.BlockSpec((1, d_rnn), const2),
            pl.BlockSpec((nb, blk, blk), const3),
            pl.BlockSpec((1, d_rnn), const2),
            pl.BlockSpec((nb, blk, blk), const3),
            pl.BlockSpec((1, d_rnn), const2),
            pl.BlockSpec((1, d_rnn), const2),
            pl.BlockSpec((d_rnn, d), const2),
        ],
        out_specs=pl.BlockSpec((None, ts, d), lambda bi, i: (bi, i, 0)),
        scratch_shapes=[
            pltpu.VMEM((SUBLANES + ts, d_rnn), F32),
            pltpu.VMEM((ts, d_rnn), F32),
            pltpu.VMEM((ts, d_rnn), F32),
            pltpu.VMEM((SUBLANES, d_rnn), F32),
        ],
        compiler_params=_cparams(("parallel", "arbitrary"), 56),
        name="rglru",
    )(proj3, proj3, x3, conv_w, row(conv_b), w_rec.astype(BF16), row(b_rec),
      w_inp.astype(BF16), row(b_inp), row(lam), w_out.astype(BF16))


def _split3(v):
    v1 = v.astype(BF16)
    r1 = v - v1.astype(F32)
    v2 = r1.astype(BF16)
    v3 = (r1 - v2.astype(F32)).astype(BF16)
    return v1, v2, v3


def _dot_nt(a, b):
    return lax.dot_general(a, b, (((1,), (1,)), ((), ())), preferred_element_type=F32)


def _first_argmax(v, n):
    idx = lax.broadcasted_iota(I32, v.shape, 0)
    vmax = jnp.max(v, axis=0, keepdims=True)
    amax = jnp.min(jnp.where(v == vmax, idx, n), axis=0, keepdims=True)
    return amax, vmax


def _router_kernel(x_ref, g_ref, w1_ref, w2_ref, w3_ref, bias_ref, hn_ref, eid_ref, wt_ref,
                   rank_ref, cnt_ref, tri_ref, carry_ref, *, n_groups, per_group):
    tm = x_ref.shape[0]
    n_exp = n_groups * per_group

    @pl.when(pl.program_id(0) == 0)
    def _():
        r = lax.broadcasted_iota(I32, (tm, tm), 0)
        c = lax.broadcasted_iota(I32, (tm, tm), 1)
        tri_ref[...] = (r < c).astype(BF16)
        carry_ref[...] = jnp.zeros_like(carry_ref)

    hn = _rms(x_ref[...], g_ref[...])
    hn_ref[...] = hn
    h1, h2, h3 = _split3(hn)
    w1, w2, w3 = w1_ref[...], w2_ref[...], w3_ref[...]
    logits = (_dot_nt(w1, h1) + (_dot_nt(w1, h2) + _dot_nt(w2, h1))
              + (_dot_nt(w1, h3) + _dot_nt(w2, h2) + _dot_nt(w3, h1)))
    logits = logits + bias_ref[:, 0:1]

    gl = logits[0:n_groups, :]
    g_idx, g_max = _first_argmax(gl, n_groups)
    gp_top = 1.0 / jnp.sum(jnp.exp(gl - g_max), axis=0, keepdims=True)
    el = jnp.zeros((per_group, tm), F32)
    for g in range(n_groups):
        lo = n_groups + g * per_group
        el = jnp.where(g_idx == g, logits[lo:lo + per_group, :], el)
    e_max = jnp.max(el, axis=0, keepdims=True)
    ex = jnp.exp(el - e_max)
    ep = ex / jnp.sum(ex, axis=0, keepdims=True)
    e1, p1 = _first_argmax(ep, per_group)
    sub = lax.broadcasted_iota(I32, ep.shape, 0)
    e2, p2 = _first_argmax(jnp.where(sub == e1, -1.0, ep), per_group)
    denom = p1 + p2
    eid1 = g_idx * per_group + e1
    eid2 = g_idx * per_group + e2
    eid_ref[...] = jnp.concatenate([eid1, eid2], axis=0)
    wt_ref[...] = jnp.concatenate([gp_top * p1 / denom, gp_top * p2 / denom], axis=0)

    e_iota = lax.broadcasted_iota(I32, (n_exp, tm), 0)
    hot1 = e_iota == eid1
    hot2 = e_iota == eid2
    chosen = jnp.logical_or(hot1, hot2)
    before = jnp.dot(chosen.astype(BF16), tri_ref[...], preferred_element_type=F32)
    base = (before + carry_ref[:, 0:1]).astype(I32)
    rank1 = jnp.sum(jnp.where(hot1, base, 0), axis=0, keepdims=True)
    rank2 = jnp.sum(jnp.where(hot2, base, 0), axis=0, keepdims=True)
    rank_ref[...] = jnp.concatenate([rank1, rank2], axis=0)
    carry_ref[...] = carry_ref[...] + jnp.sum(chosen.astype(F32), axis=1, keepdims=True)
    cnt_ref[...] = carry_ref[...].astype(I32)


def _router(x2, gain, w_group, b_group, w_router, b_router):
    n, d = x2.shape
    n_groups = w_group.shape[1]
    n_exp = w_router.shape[1]
    per_group = n_exp // n_groups
    assert per_group == SUBLANES and n_groups == SUBLANES
    rows = n_groups + n_exp
    rows_p = -(-rows // LANES) * LANES
    w_t = jnp.concatenate([w_group, w_router], axis=1).T
    w_t = jnp.pad(w_t, ((0, rows_p - rows), (0, 0)))
    w1, w2, w3 = _split3(w_t)
    bias = jnp.pad(jnp.concatenate([b_group, b_router]), (0, rows_p - rows))
    bias = jnp.broadcast_to(bias[:, None], (rows_p, LANES))
    tm = _tile(n, 512)
    const = lambda i: (0, 0)
    return pl.pallas_call(
        functools.partial(_router_kernel, n_groups=n_groups, per_group=per_group),
        out_shape=(
            jax.ShapeDtypeStruct((n, d), F32),
            jax.ShapeDtypeStruct((TOP_K, n), I32),
            jax.ShapeDtypeStruct((TOP_K, n), F32),
            jax.ShapeDtypeStruct((TOP_K, n), I32),
            jax.ShapeDtypeStruct((n_exp, LANES), I32),
        ),
        grid=(n // tm,),
        in_specs=[
            pl.BlockSpec((tm, d), lambda i: (i, 0)),
            pl.BlockSpec((1, d), const),
            pl.BlockSpec((rows_p, d), const),
            pl.BlockSpec((rows_p, d), const),
            pl.BlockSpec((rows_p, d), const),
            pl.BlockSpec((rows_p, LANES), const),
        ],
        out_specs=(
            pl.BlockSpec((tm, d), lambda i: (i, 0)),
            pl.BlockSpec((TOP_K, tm), lambda i: (0, i)),
            pl.BlockSpec((TOP_K, tm), lambda i: (0, i)),
            pl.BlockSpec((TOP_K, tm), lambda i: (0, i)),
            pl.BlockSpec((n_exp, LANES), const),
        ),
        scratch_shapes=[pltpu.VMEM((tm, tm), BF16), pltpu.VMEM((n_exp, LANES), F32)],
        compiler_params=_cparams(("arbitrary",), 40),
        name="moe_router",
    )(x2, gain.reshape(1, d), w1, w2, w3, bias)


def _dest_kernel(eid_ref, rank_ref, start_ref, dest_ref):
    eid = eid_ref[...]
    n_exp = start_ref.shape[0]
    dest = rank_ref[...]
    for k in range(eid.shape[0]):
        hot = lax.broadcasted_iota(I32, (n_exp, eid.shape[1]), 0) == eid[k:k + 1, :]
        off = jnp.sum(jnp.where(hot, start_ref[:, 0:1], 0), axis=0, keepdims=True)
        dest_ref[k:k + 1, :] = dest[k:k + 1, :] + off


def _dest_rows(eid, rank, seg_start):
    k, n = eid.shape
    n_exp = seg_start.shape[0]
    tm = _tile(n, 2048)
    start = jnp.broadcast_to(seg_start[:, None], (n_exp, LANES)).astype(I32)
    return pl.pallas_call(
        _dest_kernel,
        out_shape=jax.ShapeDtypeStruct((k, n), I32),
        grid=(n // tm,),
        in_specs=[
            pl.BlockSpec((k, tm), lambda i: (0, i)),
            pl.BlockSpec((k, tm), lambda i: (0, i)),
            pl.BlockSpec((n_exp, LANES), lambda i: (0, 0)),
        ],
        out_specs=pl.BlockSpec((k, tm), lambda i: (0, i)),
        compiler_params=_cparams(("parallel",), 16),
        name="moe_dest",
    )(eid, rank, start)


def _dispatch_kernel(nv_ref, dest_ref, hn_ref, xs_ref, zero_ref, sem, zsem, *, chunk_rows):
    k, tm = dest_ref.shape[1], dest_ref.shape[2]
    n_chunks = nv_ref.shape[0]

    @pl.when(pl.program_id(0) == 0)
    def _():
        zero_ref[...] = jnp.zeros_like(zero_ref)

        def zero_copy(c):
            rows = pl.ds(pl.multiple_of(c * chunk_rows, chunk_rows), chunk_rows)
            return pltpu.make_async_copy(zero_ref, xs_ref.at[rows], zsem)

        def start(c, carry):
            @pl.when(nv_ref[c] < chunk_rows)
            def _():
                zero_copy(c).start()
            return carry

        def finish(c, carry):
            @pl.when(nv_ref[c] < chunk_rows)
            def _():
                zero_copy(c).wait()
            return carry

        lax.fori_loop(0, n_chunks, start, 0)
        lax.fori_loop(0, n_chunks, finish, 0)


    def issue(t, c):
        for kk in range(k):
            pltpu.make_async_copy(hn_ref.at[pl.ds(t, 1)],
                                  xs_ref.at[pl.ds(dest_ref[0, kk, t], 1)], sem).start()
        return c

    lax.fori_loop(0, tm, issue, 0, unroll=8)
    for kk in range(k):
        pltpu.make_async_copy(hn_ref, xs_ref.at[pl.ds(0, tm)], sem).wait()


def _dispatch(dest, hn, n_valid, n_rows, chunk_rows):
    k, n = dest.shape
    d = hn.shape[1]
    tm = _tile(n, 512)
    dest3 = dest.reshape(k, n // tm, tm).transpose(1, 0, 2)
    grid_spec = pltpu.PrefetchScalarGridSpec(
        num_scalar_prefetch=1,
        grid=(n // tm,),
        in_specs=[
            pl.BlockSpec((1, k, tm), lambda i, nv: (i, 0, 0), memory_space=pltpu.SMEM),
            pl.BlockSpec((tm, d), lambda i, nv: (i, 0)),
        ],
        out_specs=pl.BlockSpec(memory_space=pl.ANY),
        scratch_shapes=[pltpu.VMEM((chunk_rows, d), hn.dtype), pltpu.SemaphoreType.DMA(()),
                        pltpu.SemaphoreType.DMA(())],
    )
    return pl.pallas_call(
        functools.partial(_dispatch_kernel, chunk_rows=chunk_rows),
        out_shape=jax.ShapeDtypeStruct((n_rows, d), hn.dtype),
        grid_spec=grid_spec,
        compiler_params=_cparams(("arbitrary",), 24),
        name="moe_dispatch",
    )(n_valid, dest3, hn)


def _expert_kernel(ce_ref, nu_ref, fresh_ref, xs_ref, wi_ref, wo_ref, ys_ref,
                   wib_ref, wob_ref, *, d_expert):
    i = pl.program_id(0)

    @pl.when(i < nu_ref[0])
    def _():
        @pl.when(fresh_ref[i] == 1)
        def _():
            wib_ref[...] = wi_ref[...].astype(BF16)
            wob_ref[...] = wo_ref[...].astype(BF16)

        gu = jnp.dot(xs_ref[...].astype(BF16), wib_ref[...], preferred_element_type=F32)
        gate = gu[:, :d_expert]
        act = (gate * jax.nn.sigmoid(gate) * gu[:, d_expert:]).astype(BF16)
        ys_ref[...] = jnp.dot(act, wob_ref[...], preferred_element_type=F32)

    @pl.when(i >= nu_ref[0])
    def _():
        ys_ref[...] = jnp.zeros_like(ys_ref)


def _experts(xs, chunk_e, n_used, fresh, w_in_all, w_out_all, layer, chunk_rows):
    n_rows, d = xs.shape
    d_expert = w_out_all.shape[2]
    n_chunks = n_rows // chunk_rows
    xs_map = lambda i, ce, nu, fr: (jnp.minimum(i, jnp.maximum(nu[0] - 1, 0)), 0)
    w_map = lambda i, ce, nu, fr: (layer, ce[i], 0, 0)
    grid_spec = pltpu.PrefetchScalarGridSpec(
        num_scalar_prefetch=3,
        grid=(n_chunks,),
        in_specs=[
            pl.BlockSpec((chunk_rows, d), xs_map),
            pl.BlockSpec((None, None, d, 2 * d_expert), w_map),
            pl.BlockSpec((None, None, d_expert, d), w_map),
        ],
        out_specs=pl.BlockSpec((chunk_rows, d), lambda i, ce, nu, fr: (i, 0)),
        scratch_shapes=[pltpu.VMEM((d, 2 * d_expert), BF16), pltpu.VMEM((d_expert, d), BF16)],
    )
    return pl.pallas_call(
        functools.partial(_expert_kernel, d_expert=d_expert),
        out_shape=jax.ShapeDtypeStruct((n_rows, d), F32),
        grid_spec=grid_spec,
        compiler_params=_cparams(("arbitrary",), 56),
        name="moe_experts",
    )(chunk_e, n_used, fresh, xs, w_in_all, w_out_all)


def _combine_kernel(dest_ref, next_ref, x_ref, wt_ref, g_ref, ys_ref, o_ref, buf_ref, sem_ref, *,
                    final_norm, n_blocks):
    k, tm = dest_ref.shape[1], dest_ref.shape[2]
    i = pl.program_id(0)

    def issue_block(d_ref, slot):
        def body(t, c):
            for kk in range(k):
                pltpu.make_async_copy(ys_ref.at[pl.ds(d_ref[0, kk, t], 1)],
                                      buf_ref.at[slot, kk, pl.ds(t, 1)], sem_ref.at[slot]).start()
            return c
        lax.fori_loop(0, tm, body, 0, unroll=8)

    @pl.when(i == 0)
    def _():
        issue_block(dest_ref, 0)

    @pl.when(i + 1 < n_blocks)
    def _():
        issue_block(next_ref, (i + 1) % 2)

    slot = i % 2
    for kk in range(k):
        pltpu.make_async_copy(ys_ref.at[pl.ds(0, tm)], buf_ref.at[slot, kk], sem_ref.at[slot]).wait()

    out = x_ref[...]
    for kk in range(k):
        out = out + wt_ref[:, kk:kk + 1] * buf_ref[slot, kk]
    if final_norm:
        out = _rms(out, g_ref[...])
    o_ref[...] = out


def _combine(dest, x2, wts, ys, gain, final_norm):
    k, n = dest.shape
    d = x2.shape[1]
    tm = _tile(n, 256)
    n_blocks = n // tm
    dest3 = dest.reshape(k, n_blocks, tm).transpose(1, 0, 2)
    return pl.pallas_call(
        functools.partial(_combine_kernel, final_norm=final_norm, n_blocks=n_blocks),
        out_shape=jax.ShapeDtypeStruct((n, d), F32),
        grid=(n_blocks,),
        in_specs=[
            pl.BlockSpec((1, k, tm), lambda i: (i, 0, 0), memory_space=pltpu.SMEM),
            pl.BlockSpec((1, k, tm), lambda i: (jnp.minimum(i + 1, n_blocks - 1), 0, 0),
                         memory_space=pltpu.SMEM),
            pl.BlockSpec((tm, d), lambda i: (i, 0)),
            pl.BlockSpec((tm, k), lambda i: (i, 0)),
            pl.BlockSpec((1, d), lambda i: (0, 0)),
            pl.BlockSpec(memory_space=pl.ANY),
        ],
        out_specs=pl.BlockSpec((tm, d), lambda i: (i, 0)),
        scratch_shapes=[pltpu.VMEM((2, k, tm, d), F32), pltpu.SemaphoreType.DMA((2,))],
        compiler_params=_cparams(("arbitrary",), 40),
        name="moe_combine",
    )(dest3, dest3, x2, wts.T, gain.reshape(1, d), ys)


EXPERT_CHUNK_ROWS = 256


def _moe(x2, norm_gain, w_group, b_group, w_router, b_router, w_in_all, w_out_all, layer, out_gain,
         final_norm):
    n, d = x2.shape
    n_exp = w_router.shape[1]
    chunk = EXPERT_CHUNK_ROWS
    hn, eid, wts, rank, counts = _router(x2, norm_gain, w_group, b_group, w_router, b_router)
    counts = counts[:, 0]
    padded = (counts + chunk - 1) // chunk * chunk
    seg_end = jnp.cumsum(padded)
    seg_start = seg_end - padded
    n_rows = (-(-(n * TOP_K) // chunk) + n_exp) * chunk
    n_chunks = n_rows // chunk
    chunk_start = jnp.arange(n_chunks, dtype=I32) * chunk
    n_used = (seg_end[-1:] // chunk).astype(I32)
    live_start = jnp.minimum(chunk_start, jnp.maximum(seg_end[-1] - chunk, 0))
    chunk_e = jnp.sum(seg_end[None, :] <= live_start[:, None], axis=1).astype(I32)
    chunk_e = jnp.minimum(chunk_e, n_exp - 1)
    n_valid = jnp.clip(counts[chunk_e] - (chunk_start - seg_start[chunk_e]), 0, chunk).astype(I32)
    fresh = jnp.concatenate([jnp.ones((1,), I32), (chunk_e[1:] != chunk_e[:-1]).astype(I32)])
    dest = _dest_rows(eid, rank, seg_start)
    xs = _dispatch(dest, hn, n_valid, n_rows, chunk)
    ys = _experts(xs, chunk_e, n_used, fresh, w_in_all, w_out_all, layer, chunk)
    return _combine(dest, x2, wts, ys, out_gain, final_norm)


def _forget_kernel(x_ref, g_ref, w_ref, b_ref, aux_ref, end_ref, tri_ref, carry_ref, *, n_heads):
    ts = x_ref.shape[0]

    @pl.when(pl.program_id(1) == 0)
    def _():
        r = lax.broadcasted_iota(I32, (ts, ts), 0)
        c = lax.broadcasted_iota(I32, (ts, ts), 1)
        tri_ref[...] = (c <= r).astype(BF16)
        carry_ref[...] = jnp.zeros_like(carry_ref)

    hn = _rms(x_ref[...], g_ref[...]).astype(BF16)
    f = jnp.dot(hn, w_ref[...], preferred_element_type=F32) + b_ref[...]
    logf = jnp.minimum(f, 0.0) - jnp.log(1.0 + jnp.exp(-jnp.abs(f)))
    l1, l2, l3 = _split3(logf)
    tri = tri_ref[...]
    cum = (jnp.dot(tri, l1, preferred_element_type=F32)
           + jnp.dot(tri, l2, preferred_element_type=F32)
           + jnp.dot(tri, l3, preferred_element_type=F32)) + carry_ref[0:1, :]
    carry_ref[...] = jnp.broadcast_to(cum[ts - 1:ts, :], carry_ref.shape)
    end_ref[...] = carry_ref[...] * (-LOG2E)
    c1, c2, c3 = [c.astype(F32) for c in _split3(cum * (-LOG2E))]
    lane = lax.broadcasted_iota(I32, (ts, LANES), 1)
    for h in range(n_heads):
        col = lambda v: jnp.broadcast_to(v[:, h:h + 1], (ts, LANES))
        aux = jnp.where(lane == 0, col(c1), jnp.where(lane == 1, col(c2),
                        jnp.where(lane == 2, col(c3), 0.0)))
        aux_ref[h] = aux.astype(BF16)


def _forget_aux(x3, gain, w_f, b_f):
    b, s, d = x3.shape
    n_heads = w_f.shape[1]
    assert n_heads <= LANES
    w_p = jnp.pad(w_f, ((0, 0), (0, LANES - n_heads))).astype(BF16)
    b_p = jnp.pad(b_f, (0, LANES - n_heads)).reshape(1, LANES)
    ts = _tile(s, ATTN_BLOCK)
    return pl.pallas_call(
        functools.partial(_forget_kernel, n_heads=n_heads),
        out_shape=(jax.ShapeDtypeStruct((b, n_heads, s, LANES), BF16),
                   jax.ShapeDtypeStruct((b, s // ts, SUBLANES, LANES), F32)),
        grid=(b, s // ts),
        in_specs=[
            pl.BlockSpec((None, ts, d), lambda bi, i: (bi, i, 0)),
            pl.BlockSpec((1, d), lambda bi, i: (0, 0)),
            pl.BlockSpec((d, LANES), lambda bi, i: (0, 0)),
            pl.BlockSpec((1, LANES), lambda bi, i: (0, 0)),
        ],
        out_specs=(pl.BlockSpec((None, n_heads, ts, LANES), lambda bi, i: (bi, 0, i, 0)),
                   pl.BlockSpec((None, None, SUBLANES, LANES), lambda bi, i: (bi, i, 0, 0))),
        scratch_shapes=[pltpu.VMEM((ts, ts), BF16), pltpu.VMEM((SUBLANES, LANES), F32)],
        compiler_params=_cparams(("parallel", "arbitrary"), 32),
        name="forget_cumsum",
    )(x3, gain.reshape(1, d), w_p, b_p)


ATTN_BLOCK = 512
ATTN_GROUP_BLOCKS = 4
FINITE_LIMIT = 3.0e38
SKIP_MARGIN = 160.0


def _attn_kernel(end_ref, q_ref, k_ref, aux_ref, vt_ref, o_ref, qp_ref, m_ref, l_ref, acc_ref,
                 kmax_ref):
    t, hd = q_ref.shape
    i = pl.program_id(2)
    lane = lax.broadcasted_iota(I32, (t, LANES), 1)
    qp_ref[:, 0:hd] = q_ref[...]
    qp_ref[:, hd:hd + LANES] = jnp.where(lane < 3, 1.0, 0.0).astype(BF16)

    def max_row_norm(x):
        xf = x.astype(F32)
        return jnp.sum(xf * xf, axis=1, keepdims=True)

    @pl.when(i == 0)
    def _():
        def body(c, best):
            rows = pl.ds(pl.multiple_of(c * t, t), t)
            return jnp.maximum(best, max_row_norm(k_ref[rows, :]))
        best = lax.fori_loop(0, k_ref.shape[0] // t, body, jnp.zeros((t, 1), F32))
        kmax_ref[0] = jnp.sqrt(jnp.max(best))

    reach = jnp.sqrt(jnp.max(max_row_norm(q_ref[...]))) * kmax_ref[0]

    def alive(block):
        bias_top = end_ref[0, jnp.maximum(block, 0)]
        return jnp.logical_and(block >= 0,
                               reach + bias_top - jnp.min(m_ref[...]) > -SKIP_MARGIN)

    def scores(start, size, masked):
        rows = pl.ds(pl.multiple_of(start, t), size)
        kp = jnp.concatenate([k_ref[rows, :], aux_ref[rows, :]], axis=1)
        s_t = _dot_nt(kp, qp_ref[...])
        if masked:
            kr = lax.broadcasted_iota(I32, (size, t), 0)
            qc = lax.broadcasted_iota(I32, (size, t), 1)
            s_t = jnp.where(kr <= qc, s_t, MASK_VALUE)
        return s_t, rows

    def exact_block(start, masked):
        s_t, rows = scores(start, t, masked)
        m_old = m_ref[...]
        m_new = jnp.maximum(m_old, jnp.max(s_t, axis=0, keepdims=True))
        alpha = jnp.exp2(m_old - m_new)
        p = jnp.exp2(s_t - m_new)
        l_ref[...] = alpha * l_ref[...] + jnp.sum(p, axis=0, keepdims=True)
        pv = jnp.dot(vt_ref[:, rows], p.astype(BF16), preferred_element_type=F32)
        acc_ref[...] = alpha * acc_ref[...] + pv
        m_ref[...] = m_new

    def lagged_group(start, size):
        s_t, rows = scores(start, size, False)
        m_old = m_ref[...]
        p = jnp.exp2(s_t - m_old)
        pv = jnp.dot(vt_ref[:, rows], p.astype(BF16), preferred_element_type=F32)
        m_new = jnp.maximum(m_old, jnp.max(s_t, axis=0, keepdims=True))
        alpha = jnp.exp2(m_old - m_new)
        l_ref[...] = (l_ref[...] + jnp.sum(p, axis=0, keepdims=True)) * alpha
        acc_ref[...] = (acc_ref[...] + pv) * alpha
        m_ref[...] = m_new

    def reset():
        m_ref[...] = jnp.full_like(m_ref, MASK_VALUE)
        l_ref[...] = jnp.zeros_like(l_ref)
        acc_ref[...] = jnp.zeros_like(acc_ref)

    def finish():
        out = acc_ref[...] / l_ref[...]
        o_ref[...] = out.T.astype(o_ref.dtype)
        return out

    gb = ATTN_GROUP_BLOCKS
    n_groups = i // gb
    rest = i - n_groups * gb

    def diagonal_with_group():
        s_d, rows_d = scores(i * t, t, True)
        m_diag = jnp.max(s_d, axis=0, keepdims=True)
        s_o, rows_o = scores((i - gb) * t, gb * t, False)
        p_d = jnp.exp2(s_d - m_diag)
        p_o = jnp.exp2(s_o - m_diag)
        pv = (jnp.dot(vt_ref[:, rows_d], p_d.astype(BF16), preferred_element_type=F32)
              + jnp.dot(vt_ref[:, rows_o], p_o.astype(BF16), preferred_element_type=F32))
        m_new = jnp.maximum(m_diag, jnp.max(s_o, axis=0, keepdims=True))
        alpha = jnp.exp2(m_diag - m_new)
        l_ref[...] = (jnp.sum(p_d, axis=0, keepdims=True)
                      + jnp.sum(p_o, axis=0, keepdims=True)) * alpha
        acc_ref[...] = pv * alpha
        m_ref[...] = m_new

    reset()
    fused = i >= gb

    @pl.when(fused)
    def _():
        diagonal_with_group()

    @pl.when(jnp.logical_not(fused))
    def _():
        exact_block(i * t, True)

    first_group = jnp.where(fused, 1, 0)

    def group_cond(c):
        g, live = c
        return jnp.logical_and(g < n_groups, live)

    def group_body(c):
        g, _ = c
        first = i - (g + 1) * gb
        lagged_group(first * t, gb * t)
        return g + 1, alive(first - 1)

    _, live = lax.while_loop(group_cond, group_body,
                             (first_group, alive(i - 1 - first_group * gb)))

    size = gb // 2
    while size >= 1:
        top = rest & (2 * size - 1)
        use = jnp.logical_and((rest & size) != 0, jnp.logical_and(live, alive(top - 1)))

        @pl.when(use)
        def _(top=top, size=size):
            lagged_group((top - size) * t, size * t)

        size //= 2

    out = finish()
    overflowed = jnp.max(jnp.where(jnp.abs(out) < FINITE_LIMIT, 0.0, 1.0)) > 0.0

    @pl.when(overflowed)
    def _():
        reset()

        def exact_body(j, c):
            exact_block(j * t, False)
            return c

        lax.fori_loop(0, i, exact_body, 0)
        exact_block(i * t, True)
        finish()


def _attention(qg3, k3, aux, vt, block_end, n_heads):
    b, s, _ = k3.shape
    hd = k3.shape[2] // n_heads
    assert hd == LANES
    t = _tile(s, ATTN_BLOCK)
    end_tab = block_end[:, :, 0, :n_heads].transpose(0, 2, 1).reshape(b, n_heads, 1, s // t)
    return pl.pallas_call(
        _attn_kernel,
        out_shape=jax.ShapeDtypeStruct((b, s, n_heads * hd), BF16),
        grid=(b, n_heads, s // t),
        in_specs=[
            pl.BlockSpec((None, None, 1, s // t), lambda bi, h, i: (bi, h, 0, 0),
                         memory_space=pltpu.SMEM),
            pl.BlockSpec((None, t, hd), lambda bi, h, i: (bi, i, h)),
            pl.BlockSpec((None, s, hd), lambda bi, h, i: (bi, 0, h)),
            pl.BlockSpec((None, None, s, LANES), lambda bi, h, i: (bi, h, 0, 0)),
            pl.BlockSpec((None, hd, s), lambda bi, h, i: (bi, h, 0)),
        ],
        out_specs=pl.BlockSpec((None, t, hd), lambda bi, h, i: (bi, i, h)),
        scratch_shapes=[
            pltpu.VMEM((t, hd + LANES), BF16),
            pltpu.VMEM((1, t), F32),
            pltpu.VMEM((1, t), F32),
            pltpu.VMEM((hd, t), F32),
            pltpu.SMEM((1,), F32),
        ],
        compiler_params=_cparams(("parallel", "parallel", "arbitrary"), 56),
        name="fox_attention",
    )(end_tab, qg3, k3, aux, vt)


def _gated_out_kernel(o_ref, gate_ref, x_ref, w_ref, out_ref):
    g = o_ref[...].astype(F32) * jax.nn.sigmoid(gate_ref[...].astype(F32))
    out_ref[...] = x_ref[...] + jnp.dot(g.astype(BF16), w_ref[...], preferred_element_type=F32)


def _gated_out(o2, qg, x2, w_o):
    n, d = x2.shape
    hd_all = o2.shape[1]
    tm = _tile(n, 512)
    return pl.pallas_call(
        _gated_out_kernel,
        out_shape=jax.ShapeDtypeStruct((n, d), F32),
        grid=(n // tm,),
        in_specs=[
            pl.BlockSpec((tm, hd_all), lambda i: (i, 0)),
            pl.BlockSpec((tm, hd_all), lambda i: (i, 1)),
            pl.BlockSpec((tm, d), lambda i: (i, 0)),
            pl.BlockSpec((hd_all, d), lambda i: (0, 0)),
        ],
        out_specs=pl.BlockSpec((tm, d), lambda i: (i, 0)),
        compiler_params=_cparams(("parallel",), 48),
        name="gated_out_proj",
    )(o2, qg, x2, w_o)


def kernel(x, a_norm, a_w_in, a_conv_w, a_conv_b, a_w_rec, a_b_rec, a_w_inp, a_b_inp, a_lambda, a_w_out, kv_norm, kv_w, kv_b_forget, b_norm, b_w_qg, b_w_o, m_norm, m_w_group, m_b_group, m_w_router, m_b_router, m_w_in, m_w_out, final_norm):
    b, s, d = x.shape
    n = b * s
    depth = m_norm.shape[0]
    n_a = a_norm.shape[0]
    n_heads = kv_b_forget.shape[0]
    hd_all = b_w_o.shape[1]
    head_dim = hd_all // n_heads
    x2 = x.reshape(n, d)
    k3 = aux = vt = block_end = None
    for layer in range(depth):
        if layer < n_a:
            i = layer
            d_rnn = a_w_out.shape[1]
            proj = _norm_proj(x2, a_norm[i], a_w_in[i].astype(BF16), jnp.ones((2 * d_rnn,), F32),
                              BF16, "rglru_in_proj")
            x2 = _rglru(proj, x2.reshape(b, s, d), a_conv_w[i], a_conv_b[i], a_w_rec[i], a_b_rec[i],
                        a_w_inp[i], a_b_inp[i], a_lambda[i], a_w_out[i]).reshape(n, d)
        else:
            j = layer - n_a
            q_scale = jnp.concatenate([jnp.full((hd_all,), head_dim ** -0.5 * LOG2E, F32),
                                       jnp.ones((hd_all,), F32)])
            qg = _norm_proj(x2, b_norm[j], b_w_qg[j].astype(BF16), q_scale, BF16, "fox_qg_proj")
            o = _attention(qg.reshape(b, s, 2 * hd_all), k3, aux, vt, block_end, n_heads)
            x2 = _gated_out(o.reshape(n, hd_all), qg, x2, b_w_o[j].astype(BF16))
        last = layer == depth - 1
        x2 = _moe(x2, m_norm[layer], m_w_group[layer], m_b_group[layer], m_w_router[layer],
                  m_b_router[layer], m_w_in, m_w_out, layer, final_norm, last)
        if layer == n_a - 1:
            x3 = x2.reshape(b, s, d)
            k3 = _norm_proj(x2, kv_norm, kv_w[:, :hd_all].astype(BF16), jnp.ones((hd_all,), F32),
                            BF16, "shared_k_proj").reshape(b, s, hd_all)
            vt = _norm_proj_t(x3, kv_norm, kv_w[:, hd_all:2 * hd_all].T.astype(BF16), BF16,
                              "shared_vt_proj")
            aux, block_end = _forget_aux(x3, kv_norm, kv_w[:, 2 * hd_all:], kv_b_forget)
    if depth == 0:
        x2 = _rms(x2, final_norm)
    return x2.reshape(b, s, d)
```

```python
import functools
import math

import jax
import jax.numpy as jnp
from jax import lax
from jax.experimental import pallas as pl
from jax.experimental.pallas import tpu as pltpu

F32 = jnp.float32
BF16 = jnp.bfloat16
I32 = jnp.int32

EPS = 1e-6
LRU_C = 8.0
TOP_K = 2
LOG2E = 1.4426950408889634
MASK_VALUE = -1e30

V7X_VMEM_BYTES = 64 * 1024 * 1024
SUBLANES = 8
LANES = 128
MIB = 1024 * 1024


def _cparams(semantics, vmem_mib):
    assert vmem_mib * MIB < V7X_VMEM_BYTES
    return pltpu.CompilerParams(dimension_semantics=semantics, vmem_limit_bytes=vmem_mib * MIB)


def _tile(dim, pref):
    t = min(dim, pref)
    assert dim % t == 0, (dim, pref)
    return t


def _rms(x, gain):
    return x * lax.rsqrt(jnp.mean(x * x, axis=-1, keepdims=True) + EPS) * gain


def _norm_proj_kernel(x_ref, g_ref, w_ref, s_ref, o_ref, hn_ref):
    @pl.when(pl.program_id(1) == 0)
    def _():
        hn_ref[...] = _rms(x_ref[...], g_ref[...]).astype(BF16)

    acc = jnp.dot(hn_ref[...], w_ref[...], preferred_element_type=F32)
    o_ref[...] = (acc * s_ref[...]).astype(o_ref.dtype)


def _norm_proj(x, gain, w, col_scale, out_dtype, name):
    n, d = x.shape
    n_out = w.shape[1]
    tm = _tile(n, 1024)
    tn = _tile(n_out, 1024)
    return pl.pallas_call(
        _norm_proj_kernel,
        out_shape=jax.ShapeDtypeStruct((n, n_out), out_dtype),
        grid=(n // tm, n_out // tn),
        in_specs=[
            pl.BlockSpec((tm, d), lambda i, j: (i, 0)),
            pl.BlockSpec((1, d), lambda i, j: (0, 0)),
            pl.BlockSpec((d, tn), lambda i, j: (0, j)),
            pl.BlockSpec((1, tn), lambda i, j: (0, j)),
        ],
        out_specs=pl.BlockSpec((tm, tn), lambda i, j: (i, j)),
        scratch_shapes=[pltpu.VMEM((tm, d), BF16)],
        compiler_params=_cparams(("parallel", "arbitrary"), 48),
        name=name,
    )(x, gain.reshape(1, d), w, col_scale.reshape(1, n_out))


def _norm_proj_t_kernel(x_ref, g_ref, wt_ref, o_ref, hn_ref):
    @pl.when(pl.program_id(2) == 0)
    def _():
        hn_ref[...] = _rms(x_ref[0], g_ref[...]).astype(BF16)

    acc = lax.dot_general(wt_ref[...], hn_ref[...], (((1,), (1,)), ((), ())),
                          preferred_element_type=F32)
    o_ref[0] = acc.astype(o_ref.dtype)


def _norm_proj_t(x3, gain, w_t, out_dtype, name):
    b, s, d = x3.shape
    n_out = w_t.shape[0]
    tm = _tile(s, 1024)
    tn = _tile(n_out, 1024)
    return pl.pallas_call(
        _norm_proj_t_kernel,
        out_shape=jax.ShapeDtypeStruct((b, n_out, s), out_dtype),
        grid=(b, s // tm, n_out // tn),
        in_specs=[
            pl.BlockSpec((1, tm, d), lambda bi, i, j: (bi, i, 0)),
            pl.BlockSpec((1, d), lambda bi, i, j: (0, 0)),
            pl.BlockSpec((tn, d), lambda bi, i, j: (j, 0)),
        ],
        out_specs=pl.BlockSpec((1, tn, tm), lambda bi, i, j: (bi, j, i)),
        scratch_shapes=[pltpu.VMEM((tm, d), BF16)],
        compiler_params=_cparams(("parallel", "parallel", "arbitrary"), 48),
        name=name,
    )(x3, gain.reshape(1, d), w_t)


SCAN_STRIP = 512


def _rglru_kernel(u_ref, y_ref, x_ref, cw_ref, cb_ref, wr_ref, br_ref, wi_ref, bi_ref,
                  lam_ref, wo_ref, o_ref, ubuf_ref, a_ref, b_ref, h_ref, *, conv_width):
    ts, d = u_ref.shape
    nb, blk, _ = wr_ref.shape
    halo = SUBLANES

    @pl.when(pl.program_id(1) == 0)
    def _():
        ubuf_ref[0:halo, :] = jnp.zeros((halo, d), F32)
        h_ref[...] = jnp.zeros_like(h_ref)

    u = u_ref[...].astype(F32)
    ubuf_ref[halo:halo + ts, :] = u
    uc = u * cw_ref[conv_width - 1:conv_width, :] + cb_ref[...]
    for j in range(conv_width - 1):
        shift = conv_width - 1 - j
        uc = uc + ubuf_ref[halo - shift:halo - shift + ts, :] * cw_ref[j:j + 1, :]
    ubuf_ref[0:halo, :] = u[ts - halo:, :]

    ub = uc.astype(BF16)
    r_parts, i_parts = [], []
    for n in range(nb):
        ubn = ub[:, n * blk:(n + 1) * blk]
        r_parts.append(jnp.dot(ubn, wr_ref[n], preferred_element_type=F32))
        i_parts.append(jnp.dot(ubn, wi_ref[n], preferred_element_type=F32))
    r = jax.nn.sigmoid(jnp.concatenate(r_parts, axis=1) + br_ref[...])
    gi = jax.nn.sigmoid(jnp.concatenate(i_parts, axis=1) + bi_ref[...])
    lam = lam_ref[...]
    sp = jnp.maximum(-lam, 0.0) + jnp.log(1.0 + jnp.exp(-jnp.abs(lam)))
    log_a = (-LRU_C * r) * sp
    a_ref[...] = jnp.exp(log_a)
    th = jnp.tanh(log_a)
    b_ref[...] = jnp.sqrt(-2.0 * th / (1.0 - th)) * (gi * uc)

    row = lax.broadcasted_iota(I32, (SUBLANES, SCAN_STRIP), 0)
    strip = min(SCAN_STRIP, d)
    for c in range(d // strip):
        cols = pl.ds(c * strip, strip)

        def tile_step(t, h):
            rows = pl.ds(pl.multiple_of(t * SUBLANES, SUBLANES), SUBLANES)
            a = a_ref[rows, cols]
            bb = b_ref[rows, cols]
            for sh in (1, 2, 4):
                keep = row[:, :strip] >= sh
                a_prev = jnp.where(keep, pltpu.roll(a, sh, 0), 1.0)
                b_prev = jnp.where(keep, pltpu.roll(bb, sh, 0), 0.0)
                bb = a * b_prev + bb
                a = a * a_prev
            hs = a * h + bb
            b_ref[rows, cols] = hs
            return jnp.broadcast_to(hs[SUBLANES - 1:SUBLANES, :], (SUBLANES, strip))

        h_ref[:, cols] = lax.fori_loop(0, ts // SUBLANES, tile_step, h_ref[:, cols], unroll=2)

    y = y_ref[...].astype(F32)
    gelu = 0.5 * y * (1.0 + jnp.tanh(math.sqrt(2.0 / math.pi) * (y + 0.044715 * (y * y * y))))
    g = (b_ref[...] * gelu).astype(BF16)
    o_ref[...] = x_ref[...] + jnp.dot(g, wo_ref[...], preferred_element_type=F32)


def _rglru(proj, x3, conv_w, conv_b, w_rec, b_rec, w_inp, b_inp, lam, w_out):
    b, s, d = x3.shape
    d_rnn = w_out.shape[0]
    nb, blk, _ = w_rec.shape
    width = conv_w.shape[0]
    assert width - 1 <= SUBLANES
    ts = _tile(s, 256)
    proj3 = proj.reshape(b, s, 2 * d_rnn)
    row = lambda v: v.reshape(1, -1)
    const2 = lambda bi, i: (0, 0)
    const3 = lambda bi, i: (0, 0, 0)
    return pl.pallas_call(
        functools.partial(_rglru_kernel, conv_width=width),
        out_shape=jax.ShapeDtypeStruct((b, s, d), F32),
        grid=(b, s // ts),
        in_specs=[
            pl.BlockSpec((None, ts, d_rnn), lambda bi, i: (bi, i, 0)),
            pl.BlockSpec((None, ts, d_rnn), lambda bi, i: (bi, i, 1)),
            pl.BlockSpec((None, ts, d), lambda bi, i: (bi, i, 0)),
            pl.BlockSpec((width, d_rnn), const2),
            pl.BlockSpec((1, d_rnn), const2),
            pl.BlockSpec((nb, blk, blk), const3),
            pl.BlockSpec((1, d_rnn), const2),
            pl.BlockSpec((nb, blk, blk), const3),
            pl.BlockSpec((1, d_rnn), const2),
            pl.BlockSpec((1, d_rnn), const2),
            pl.BlockSpec((d_rnn, d), const2),
        ],
        out_specs=pl.BlockSpec((None, ts, d), lambda bi, i: (bi, i, 0)),
        scratch_shapes=[
            pltpu.VMEM((SUBLANES + ts, d_rnn), F32),
            pltpu.VMEM((ts, d_rnn), F32),
            pltpu.VMEM((ts, d_rnn), F32),
            pltpu.VMEM((SUBLANES, d_rnn), F32),
        ],
        compiler_params=_cparams(("parallel", "arbitrary"), 56),
        name="rglru",
    )(proj3, proj3, x3, conv_w, row(conv_b), w_rec.astype(BF16), row(b_rec),
      w_inp.astype(BF16), row(b_inp), row(lam), w_out.astype(BF16))


def _pack_bf16_pair(x):
    c = x.shape[1] // 2
    lo = lax.bitcast_convert_type(x[:, :c].astype(BF16).astype(F32), I32)
    hi = lax.bitcast_convert_type(x[:, c:].astype(BF16).astype(F32), I32)
    return jnp.bitwise_or(hi, lax.shift_right_logical(lo, jnp.int32(16)))


def _unpack_bf16_pair(w):
    lo = lax.bitcast_convert_type(lax.shift_left(w, jnp.int32(16)), F32)
    hi = lax.bitcast_convert_type(jnp.bitwise_and(w, jnp.int32(-65536)), F32)
    return lo, hi


def _split3(v):
    v1 = v.astype(BF16)
    r1 = v - v1.astype(F32)
    v2 = r1.astype(BF16)
    v3 = (r1 - v2.astype(F32)).astype(BF16)
    return v1, v2, v3


def _dot_nt(a, b):
    return lax.dot_general(a, b, (((1,), (1,)), ((), ())), preferred_element_type=F32)


def _first_argmax(v, n):
    idx = lax.broadcasted_iota(I32, v.shape, 0)
    vmax = jnp.max(v, axis=0, keepdims=True)
    amax = jnp.min(jnp.where(v == vmax, idx, n), axis=0, keepdims=True)
    return amax, vmax


def _router_kernel(x_ref, g_ref, w1_ref, w2_ref, w3_ref, bias_ref, hn_ref, eid_ref, wt_ref,
                   rank_ref, cnt_ref, tri_ref, carry_ref, *, n_groups, per_group):
    tm = x_ref.shape[0]
    n_exp = n_groups * per_group

    @pl.when(pl.program_id(0) == 0)
    def _():
        r = lax.broadcasted_iota(I32, (tm, tm), 0)
        c = lax.broadcasted_iota(I32, (tm, tm), 1)
        tri_ref[...] = (r < c).astype(BF16)
        carry_ref[...] = jnp.zeros_like(carry_ref)

    hn = _rms(x_ref[...], g_ref[...])
    hn_ref[...] = _pack_bf16_pair(hn)
    h1, h2, h3 = _split3(hn)
    w1, w2, w3 = w1_ref[...], w2_ref[...], w3_ref[...]
    logits = (_dot_nt(w1, h1) + (_dot_nt(w1, h2) + _dot_nt(w2, h1))
              + (_dot_nt(w1, h3) + _dot_nt(w2, h2) + _dot_nt(w3, h1)))
    logits = logits + bias_ref[:, 0:1]

    gl = logits[0:n_groups, :]
    g_idx, g_max = _first_argmax(gl, n_groups)
    gp_top = 1.0 / jnp.sum(jnp.exp(gl - g_max), axis=0, keepdims=True)
    el = jnp.zeros((per_group, tm), F32)
    for g in range(n_groups):
        lo = n_groups + g * per_group
        el = jnp.where(g_idx == g, logits[lo:lo + per_group, :], el)
    e_max = jnp.max(el, axis=0, keepdims=True)
    ex = jnp.exp(el - e_max)
    ep = ex / jnp.sum(ex, axis=0, keepdims=True)
    e1, p1 = _first_argmax(ep, per_group)
    sub = lax.broadcasted_iota(I32, ep.shape, 0)
    e2, p2 = _first_argmax(jnp.where(sub == e1, -1.0, ep), per_group)
    denom = p1 + p2
    eid1 = g_idx * per_group + e1
    eid2 = g_idx * per_group + e2
    eid_ref[...] = jnp.concatenate([eid1, eid2], axis=0)
    wt_ref[...] = jnp.concatenate([gp_top * p1 / denom, gp_top * p2 / denom], axis=0)

    e_iota = lax.broadcasted_iota(I32, (n_exp, tm), 0)
    hot1 = e_iota == eid1
    hot2 = e_iota == eid2
    chosen = jnp.logical_or(hot1, hot2)
    before = jnp.dot(chosen.astype(BF16), tri_ref[...], preferred_element_type=F32)
    base = (before + carry_ref[:, 0:1]).astype(I32)
    rank1 = jnp.sum(jnp.where(hot1, base, 0), axis=0, keepdims=True)
    rank2 = jnp.sum(jnp.where(hot2, base, 0), axis=0, keepdims=True)
    rank_ref[...] = jnp.concatenate([rank1, rank2], axis=0)
    carry_ref[...] = carry_ref[...] + jnp.sum(chosen.astype(F32), axis=1, keepdims=True)
    cnt_ref[...] = carry_ref[...].astype(I32)


def _router(x2, gain, w_group, b_group, w_router, b_router):
    n, d = x2.shape
    n_groups = w_group.shape[1]
    n_exp = w_router.shape[1]
    per_group = n_exp // n_groups
    assert per_group == SUBLANES and n_groups == SUBLANES
    rows = n_groups + n_exp
    rows_p = -(-rows // LANES) * LANES
    w_t = jnp.concatenate([w_group, w_router], axis=1).T
    w_t = jnp.pad(w_t, ((0, rows_p - rows), (0, 0)))
    w1, w2, w3 = _split3(w_t)
    bias = jnp.pad(jnp.concatenate([b_group, b_router]), (0, rows_p - rows))
    bias = jnp.broadcast_to(bias[:, None], (rows_p, LANES))
    tm = _tile(n, 512)
    const = lambda i: (0, 0)
    return pl.pallas_call(
        functools.partial(_router_kernel, n_groups=n_groups, per_group=per_group),
        out_shape=(
            jax.ShapeDtypeStruct((n, d // 2), I32),
            jax.ShapeDtypeStruct((TOP_K, n), I32),
            jax.ShapeDtypeStruct((TOP_K, n), F32),
            jax.ShapeDtypeStruct((TOP_K, n), I32),
            jax.ShapeDtypeStruct((n_exp, LANES), I32),
        ),
        grid=(n // tm,),
        in_specs=[
            pl.BlockSpec((tm, d), lambda i: (i, 0)),
            pl.BlockSpec((1, d), const),
            pl.BlockSpec((rows_p, d), const),
            pl.BlockSpec((rows_p, d), const),
            pl.BlockSpec((rows_p, d), const),
            pl.BlockSpec((rows_p, LANES), const),
        ],
        out_specs=(
            pl.BlockSpec((tm, d // 2), lambda i: (i, 0)),
            pl.BlockSpec((TOP_K, tm), lambda i: (0, i)),
            pl.BlockSpec((TOP_K, tm), lambda i: (0, i)),
            pl.BlockSpec((TOP_K, tm), lambda i: (0, i)),
            pl.BlockSpec((n_exp, LANES), const),
        ),
        scratch_shapes=[pltpu.VMEM((tm, tm), BF16), pltpu.VMEM((n_exp, LANES), F32)],
        compiler_params=_cparams(("arbitrary",), 40),
        name="moe_router",
    )(x2, gain.reshape(1, d), w1, w2, w3, bias)


def _dest_kernel(eid_ref, rank_ref, start_ref, dest_ref):
    eid = eid_ref[...]
    n_exp = start_ref.shape[0]
    dest = rank_ref[...]
    for k in range(eid.shape[0]):
        hot = lax.broadcasted_iota(I32, (n_exp, eid.shape[1]), 0) == eid[k:k + 1, :]
        off = jnp.sum(jnp.where(hot, start_ref[:, 0:1], 0), axis=0, keepdims=True)
        dest_ref[k:k + 1, :] = dest[k:k + 1, :] + off


def _dest_rows(eid, rank, seg_start):
    k, n = eid.shape
    n_exp = seg_start.shape[0]
    tm = _tile(n, 2048)
    start = jnp.broadcast_to(seg_start[:, None], (n_exp, LANES)).astype(I32)
    return pl.pallas_call(
        _dest_kernel,
        out_shape=jax.ShapeDtypeStruct((k, n), I32),
        grid=(n // tm,),
        in_specs=[
            pl.BlockSpec((k, tm), lambda i: (0, i)),
            pl.BlockSpec((k, tm), lambda i: (0, i)),
            pl.BlockSpec((n_exp, LANES), lambda i: (0, 0)),
        ],
        out_specs=pl.BlockSpec((k, tm), lambda i: (0, i)),
        compiler_params=_cparams(("parallel",), 16),
        name="moe_dest",
    )(eid, rank, start)


def _dispatch_kernel(nv_ref, dest_ref, hn_ref, xs_ref, zero_ref, sem, zsem, *, chunk_rows):
    k, tm = dest_ref.shape[1], dest_ref.shape[2]
    n_chunks = nv_ref.shape[0]

    @pl.when(pl.program_id(0) == 0)
    def _():
        zero_ref[...] = jnp.zeros_like(zero_ref)

        def zero_copy(c):
            rows = pl.ds(pl.multiple_of(c * chunk_rows, chunk_rows), chunk_rows)
            return pltpu.make_async_copy(zero_ref, xs_ref.at[rows], zsem)

        def start(c, carry):
            @pl.when(nv_ref[c] < chunk_rows)
            def _():
                zero_copy(c).start()
            return carry

        def finish(c, carry):
            @pl.when(nv_ref[c] < chunk_rows)
            def _():
                zero_copy(c).wait()
            return carry

        lax.fori_loop(0, n_chunks, start, 0)
        lax.fori_loop(0, n_chunks, finish, 0)


    def issue(t, c):
        for kk in range(k):
            pltpu.make_async_copy(hn_ref.at[pl.ds(t, 1)],
                                  xs_ref.at[pl.ds(dest_ref[0, kk, t], 1)], sem).start()
        return c

    lax.fori_loop(0, tm, issue, 0, unroll=8)
    for kk in range(k):
        pltpu.make_async_copy(hn_ref, xs_ref.at[pl.ds(0, tm)], sem).wait()


def _dispatch(dest, hn, n_valid, n_rows, chunk_rows):
    k, n = dest.shape
    d = hn.shape[1]
    tm = _tile(n, 512)
    dest3 = dest.reshape(k, n // tm, tm).transpose(1, 0, 2)
    grid_spec = pltpu.PrefetchScalarGridSpec(
        num_scalar_prefetch=1,
        grid=(n // tm,),
        in_specs=[
            pl.BlockSpec((1, k, tm), lambda i, nv: (i, 0, 0), memory_space=pltpu.SMEM),
            pl.BlockSpec((tm, d), lambda i, nv: (i, 0)),
        ],
        out_specs=pl.BlockSpec(memory_space=pl.ANY),
        scratch_shapes=[pltpu.VMEM((chunk_rows, d), hn.dtype), pltpu.SemaphoreType.DMA(()),
                        pltpu.SemaphoreType.DMA(())],
    )
    return pl.pallas_call(
        functools.partial(_dispatch_kernel, chunk_rows=chunk_rows),
        out_shape=jax.ShapeDtypeStruct((n_rows, d), hn.dtype),
        grid_spec=grid_spec,
        compiler_params=_cparams(("arbitrary",), 24),
        name="moe_dispatch",
    )(n_valid, dest3, hn)


def _expert_kernel(ce_ref, nu_ref, fresh_ref, xs_ref, wi_ref, wo_ref, ys_ref,
                   wib_ref, wob_ref, *, d_expert):
    i = pl.program_id(0)

    @pl.when(i < nu_ref[0])
    def _():
        @pl.when(fresh_ref[i] == 1)
        def _():
            wib_ref[...] = wi_ref[...].astype(BF16)
            wob_ref[...] = wo_ref[...].astype(BF16)

        x_lo, x_hi = _unpack_bf16_pair(xs_ref[...])
        xb = jnp.concatenate([x_lo, x_hi], axis=1).astype(BF16)
        gu = jnp.dot(xb, wib_ref[...], preferred_element_type=F32)
        gate = gu[:, :d_expert]
        act = (gate * jax.nn.sigmoid(gate) * gu[:, d_expert:]).astype(BF16)
        ys_ref[...] = _pack_bf16_pair(jnp.dot(act, wob_ref[...], preferred_element_type=F32))

    @pl.when(i >= nu_ref[0])
    def _():
        ys_ref[...] = jnp.zeros_like(ys_ref)


def _experts(xs, chunk_e, n_used, fresh, w_in_all, w_out_all, layer, chunk_rows):
    n_rows, d_packed = xs.shape
    d = 2 * d_packed
    d_expert = w_out_all.shape[2]
    n_chunks = n_rows // chunk_rows
    xs_map = lambda i, ce, nu, fr: (jnp.minimum(i, jnp.maximum(nu[0] - 1, 0)), 0)
    w_map = lambda i, ce, nu, fr: (layer, ce[i], 0, 0)
    grid_spec = pltpu.PrefetchScalarGridSpec(
        num_scalar_prefetch=3,
        grid=(n_chunks,),
        in_specs=[
            pl.BlockSpec((chunk_rows, d_packed), xs_map),
            pl.BlockSpec((None, None, d, 2 * d_expert), w_map),
            pl.BlockSpec((None, None, d_expert, d), w_map),
        ],
        out_specs=pl.BlockSpec((chunk_rows, d_packed), lambda i, ce, nu, fr: (i, 0)),
        scratch_shapes=[pltpu.VMEM((d, 2 * d_expert), BF16), pltpu.VMEM((d_expert, d), BF16)],
    )
    return pl.pallas_call(
        functools.partial(_expert_kernel, d_expert=d_expert),
        out_shape=jax.ShapeDtypeStruct((n_rows, d_packed), I32),
        grid_spec=grid_spec,
        compiler_params=_cparams(("arbitrary",), 56),
        name="moe_experts",
    )(chunk_e, n_used, fresh, xs, w_in_all, w_out_all)


def _combine_kernel(dest_ref, next_ref, x_ref, wt_ref, g_ref, ys_ref, o_ref, buf_ref, sem_ref, *,
                    final_norm, n_blocks):
    k, tm = dest_ref.shape[1], dest_ref.shape[2]
    i = pl.program_id(0)

    def issue_block(d_ref, slot):
        def body(t, c):
            for kk in range(k):
                pltpu.make_async_copy(ys_ref.at[pl.ds(d_ref[0, kk, t], 1)],
                                      buf_ref.at[slot, kk, pl.ds(t, 1)], sem_ref.at[slot]).start()
            return c
        lax.fori_loop(0, tm, body, 0, unroll=8)

    @pl.when(i == 0)
    def _():
        issue_block(dest_ref, 0)

    @pl.when(i + 1 < n_blocks)
    def _():
        issue_block(next_ref, (i + 1) % 2)

    slot = i % 2
    for kk in range(k):
        pltpu.make_async_copy(ys_ref.at[pl.ds(0, tm)], buf_ref.at[slot, kk], sem_ref.at[slot]).wait()

    c = buf_ref.shape[3]
    out_lo = x_ref[:, :c]
    out_hi = x_ref[:, c:]
    for kk in range(k):
        y_lo, y_hi = _unpack_bf16_pair(buf_ref[slot, kk])
        w = wt_ref[:, kk:kk + 1]
        out_lo = out_lo + w * y_lo
        out_hi = out_hi + w * y_hi
    if final_norm:
        ms = (jnp.sum(out_lo * out_lo, axis=-1, keepdims=True)
              + jnp.sum(out_hi * out_hi, axis=-1, keepdims=True)) / (2 * c)
        scale = lax.rsqrt(ms + EPS)
        out_lo = out_lo * scale * g_ref[:, :c]
        out_hi = out_hi * scale * g_ref[:, c:]
    o_ref[:, :c] = out_lo
    o_ref[:, c:] = out_hi


def _combine(dest, x2, wts, ys, gain, final_norm):
    k, n = dest.shape
    d = x2.shape[1]
    tm = _tile(n, 512)
    n_blocks = n // tm
    dest3 = dest.reshape(k, n_blocks, tm).transpose(1, 0, 2)
    return pl.pallas_call(
        functools.partial(_combine_kernel, final_norm=final_norm, n_blocks=n_blocks),
        out_shape=jax.ShapeDtypeStruct((n, d), F32),
        grid=(n_blocks,),
        in_specs=[
            pl.BlockSpec((1, k, tm), lambda i: (i, 0, 0), memory_space=pltpu.SMEM),
            pl.BlockSpec((1, k, tm), lambda i: (jnp.minimum(i + 1, n_blocks - 1), 0, 0),
                         memory_space=pltpu.SMEM),
            pl.BlockSpec((tm, d), lambda i: (i, 0)),
            pl.BlockSpec((tm, k), lambda i: (i, 0)),
            pl.BlockSpec((1, d), lambda i: (0, 0)),
            pl.BlockSpec(memory_space=pl.ANY),
        ],
        out_specs=pl.BlockSpec((tm, d), lambda i: (i, 0)),
        scratch_shapes=[pltpu.VMEM((2, k, tm, d // 2), I32), pltpu.SemaphoreType.DMA((2,))],
        compiler_params=_cparams(("arbitrary",), 40),
        name="moe_combine",
    )(dest3, dest3, x2, wts.T, gain.reshape(1, d), ys)


EXPERT_CHUNK_ROWS = 256


def _moe(x2, norm_gain, w_group, b_group, w_router, b_router, w_in_all, w_out_all, layer, out_gain,
         final_norm):
    n, d = x2.shape
    n_exp = w_router.shape[1]
    chunk = EXPERT_CHUNK_ROWS
    hn, eid, wts, rank, counts = _router(x2, norm_gain, w_group, b_group, w_router, b_router)
    counts = counts[:, 0]
    padded = (counts + chunk - 1) // chunk * chunk
    seg_end = jnp.cumsum(padded)
    seg_start = seg_end - padded
    n_rows = (-(-(n * TOP_K) // chunk) + n_exp) * chunk
    n_chunks = n_rows // chunk
    chunk_start = jnp.arange(n_chunks, dtype=I32) * chunk
    n_used = (seg_end[-1:] // chunk).astype(I32)
    live_start = jnp.minimum(chunk_start, jnp.maximum(seg_end[-1] - chunk, 0))
    chunk_e = jnp.sum(seg_end[None, :] <= live_start[:, None], axis=1).astype(I32)
    chunk_e = jnp.minimum(chunk_e, n_exp - 1)
    n_valid = jnp.clip(counts[chunk_e] - (chunk_start - seg_start[chunk_e]), 0, chunk).astype(I32)
    fresh = jnp.concatenate([jnp.ones((1,), I32), (chunk_e[1:] != chunk_e[:-1]).astype(I32)])
    dest = _dest_rows(eid, rank, seg_start)
    xs = _dispatch(dest, hn, n_valid, n_rows, chunk)
    ys = _experts(xs, chunk_e, n_used, fresh, w_in_all, w_out_all, layer, chunk)
    return _combine(dest, x2, wts, ys, out_gain, final_norm)


def _forget_kernel(x_ref, g_ref, w_ref, b_ref, aux_ref, end_ref, tri_ref, carry_ref, *, n_heads):
    ts = x_ref.shape[0]

    @pl.when(pl.program_id(1) == 0)
    def _():
        r = lax.broadcasted_iota(I32, (ts, ts), 0)
        c = lax.broadcasted_iota(I32, (ts, ts), 1)
        tri_ref[...] = (c <= r).astype(BF16)
        carry_ref[...] = jnp.zeros_like(carry_ref)

    hn = _rms(x_ref[...], g_ref[...]).astype(BF16)
    f = jnp.dot(hn, w_ref[...], preferred_element_type=F32) + b_ref[...]
    logf = jnp.minimum(f, 0.0) - jnp.log(1.0 + jnp.exp(-jnp.abs(f)))
    l1, l2, l3 = _split3(logf)
    tri = tri_ref[...]
    cum = (jnp.dot(tri, l1, preferred_element_type=F32)
           + jnp.dot(tri, l2, preferred_element_type=F32)
           + jnp.dot(tri, l3, preferred_element_type=F32)) + carry_ref[0:1, :]
    carry_ref[...] = jnp.broadcast_to(cum[ts - 1:ts, :], carry_ref.shape)
    end_ref[...] = carry_ref[...] * (-LOG2E)
    c1, c2, c3 = [c.astype(F32) for c in _split3(cum * (-LOG2E))]
    lane = lax.broadcasted_iota(I32, (ts, LANES), 1)
    for h in range(n_heads):
        col = lambda v: jnp.broadcast_to(v[:, h:h + 1], (ts, LANES))
        aux = jnp.where(lane == 0, col(c1), jnp.where(lane == 1, col(c2),
                        jnp.where(lane == 2, col(c3), 0.0)))
        aux_ref[h] = aux.astype(BF16)


def _forget_aux(x3, gain, w_f, b_f):
    b, s, d = x3.shape
    n_heads = w_f.shape[1]
    assert n_heads <= LANES
    w_p = jnp.pad(w_f, ((0, 0), (0, LANES - n_heads))).astype(BF16)
    b_p = jnp.pad(b_f, (0, LANES - n_heads)).reshape(1, LANES)
    ts = _tile(s, ATTN_BLOCK)
    return pl.pallas_call(
        functools.partial(_forget_kernel, n_heads=n_heads),
        out_shape=(jax.ShapeDtypeStruct((b, n_heads, s, LANES), BF16),
                   jax.ShapeDtypeStruct((b, s // ts, SUBLANES, LANES), F32)),
        grid=(b, s // ts),
        in_specs=[
            pl.BlockSpec((None, ts, d), lambda bi, i: (bi, i, 0)),
            pl.BlockSpec((1, d), lambda bi, i: (0, 0)),
            pl.BlockSpec((d, LANES), lambda bi, i: (0, 0)),
            pl.BlockSpec((1, LANES), lambda bi, i: (0, 0)),
        ],
        out_specs=(pl.BlockSpec((None, n_heads, ts, LANES), lambda bi, i: (bi, 0, i, 0)),
                   pl.BlockSpec((None, None, SUBLANES, LANES), lambda bi, i: (bi, i, 0, 0))),
        scratch_shapes=[pltpu.VMEM((ts, ts), BF16), pltpu.VMEM((SUBLANES, LANES), F32)],
        compiler_params=_cparams(("parallel", "arbitrary"), 32),
        name="forget_cumsum",
    )(x3, gain.reshape(1, d), w_p, b_p)


ATTN_BLOCK = 512
ATTN_GROUP_BLOCKS = 4
FINITE_LIMIT = 3.0e38
SKIP_MARGIN = 160.0


def _attn_kernel(end_ref, q_ref, k_ref, aux_ref, vt_ref, o_ref, qp_ref, m_ref, l_ref, acc_ref,
                 kmax_ref):
    t, hd = q_ref.shape
    i = pl.program_id(2)
    lane = lax.broadcasted_iota(I32, (t, LANES), 1)
    qp_ref[:, 0:hd] = q_ref[...]
    qp_ref[:, hd:hd + LANES] = jnp.where(lane < 3, 1.0, 0.0).astype(BF16)

    def max_row_norm(x):
        xf = x.astype(F32)
        return jnp.sum(xf * xf, axis=1, keepdims=True)

    @pl.when(i == 0)
    def _():
        def body(c, best):
            rows = pl.ds(pl.multiple_of(c * t, t), t)
            return jnp.maximum(best, max_row_norm(k_ref[rows, :]))
        best = lax.fori_loop(0, k_ref.shape[0] // t, body, jnp.zeros((t, 1), F32))
        kmax_ref[0] = jnp.sqrt(jnp.max(best))

    reach = jnp.sqrt(jnp.max(max_row_norm(q_ref[...]))) * kmax_ref[0]

    def alive(block):
        bias_top = end_ref[0, jnp.maximum(block, 0)]
        return jnp.logical_and(block >= 0,
                               reach + bias_top - jnp.min(m_ref[...]) > -SKIP_MARGIN)

    def scores(start, size):
        rows = pl.ds(pl.multiple_of(start, t), size)
        kp = jnp.concatenate([k_ref[rows, :], aux_ref[rows, :]], axis=1)
        return _dot_nt(kp, qp_ref[...]), rows

    def exact_block(j):
        s_t, rows = scores(j * t, t)
        kr = lax.broadcasted_iota(I32, (t, t), 0) + j * t
        qc = lax.broadcasted_iota(I32, (t, t), 1) + i * t
        s_t = jnp.where(kr <= qc, s_t, MASK_VALUE)
        m_old = m_ref[...]
        m_new = jnp.maximum(m_old, jnp.max(s_t, axis=0, keepdims=True))
        alpha = jnp.exp2(m_old - m_new)
        p = jnp.exp2(s_t - m_new)
        l_ref[...] = alpha * l_ref[...] + jnp.sum(p, axis=0, keepdims=True)
        pv = jnp.dot(vt_ref[:, rows], p.astype(BF16), preferred_element_type=F32)
        acc_ref[...] = alpha * acc_ref[...] + pv
        m_ref[...] = m_new

    def lagged_group(start, size):
        s_t, rows = scores(start, size)
        m_old = m_ref[...]
        p = jnp.exp2(s_t - m_old)
        pv = jnp.dot(vt_ref[:, rows], p.astype(BF16), preferred_element_type=F32)
        m_new = jnp.maximum(m_old, jnp.max(s_t, axis=0, keepdims=True))
        alpha = jnp.exp2(m_old - m_new)
        l_ref[...] = (l_ref[...] + jnp.sum(p, axis=0, keepdims=True)) * alpha
        acc_ref[...] = (acc_ref[...] + pv) * alpha
        m_ref[...] = m_new

    def reset():
        m_ref[...] = jnp.full_like(m_ref, MASK_VALUE)
        l_ref[...] = jnp.zeros_like(l_ref)
        acc_ref[...] = jnp.zeros_like(acc_ref)

    gb = ATTN_GROUP_BLOCKS
    n_groups = i // gb
    rest = i - n_groups * gb

    def older_keys_lagged():
        def group_cond(c):
            g, live = c
            return jnp.logical_and(g < n_groups, live)

        def group_body(c):
            g, _ = c
            first = i - (g + 1) * gb
            lagged_group(first * t, gb * t)
            return g + 1, alive(first - 1)

        _, live = lax.while_loop(group_cond, group_body, (0, alive(i - 1)))
        size = gb // 2
        while size >= 1:
            top = rest & (2 * size - 1)
            use = jnp.logical_and((rest & size) != 0, jnp.logical_and(live, alive(top - 1)))

            @pl.when(use)
            def _(top=top, size=size):
                lagged_group((top - size) * t, size * t)

            size //= 2

    def attempt_cond(c):
        attempt, overflowed = c
        return jnp.logical_or(attempt == 0, jnp.logical_and(attempt == 1, overflowed == 1))

    def attempt_body(c):
        attempt, _ = c
        reset()

        def exact_body(j, carry):
            exact_block(j)
            return carry

        lax.fori_loop(jnp.where(attempt == 0, i, 0), i + 1, exact_body, 0)

        @pl.when(attempt == 0)
        def _():
            older_keys_lagged()

        out = acc_ref[...] / l_ref[...]
        o_ref[...] = out.T.astype(o_ref.dtype)
        bad = jnp.max(jnp.where(jnp.abs(out) < FINITE_LIMIT, 0.0, 1.0)) > 0.0
        return attempt + 1, bad.astype(I32)

    lax.while_loop(attempt_cond, attempt_body, (0, 0))


def _attention(qg3, k3, aux, vt, block_end, n_heads):
    b, s, _ = k3.shape
    hd = k3.shape[2] // n_heads
    assert hd == LANES
    t = _tile(s, ATTN_BLOCK)
    end_tab = block_end[:, :, 0, :n_heads].transpose(0, 2, 1).reshape(b, n_heads, 1, s // t)
    return pl.pallas_call(
        _attn_kernel,
        out_shape=jax.ShapeDtypeStruct((b, s, n_heads * hd), BF16),
        grid=(b, n_heads, s // t),
        in_specs=[
            pl.BlockSpec((None, None, 1, s // t), lambda bi, h, i: (bi, h, 0, 0),
                         memory_space=pltpu.SMEM),
            pl.BlockSpec((None, t, hd), lambda bi, h, i: (bi, i, h)),
            pl.BlockSpec((None, s, hd), lambda bi, h, i: (bi, 0, h)),
            pl.BlockSpec((None, None, s, LANES), lambda bi, h, i: (bi, h, 0, 0)),
            pl.BlockSpec((None, hd, s), lambda bi, h, i: (bi, h, 0)),
        ],
        out_specs=pl.BlockSpec((None, t, hd), lambda bi, h, i: (bi, i, h)),
        scratch_shapes=[
            pltpu.VMEM((t, hd + LANES), BF16),
            pltpu.VMEM((1, t), F32),
            pltpu.VMEM((1, t), F32),
            pltpu.VMEM((hd, t), F32),
            pltpu.SMEM((1,), F32),
        ],
        compiler_params=_cparams(("parallel", "parallel", "arbitrary"), 56),
        name="fox_attention",
    )(end_tab, qg3, k3, aux, vt)


def _gated_out_kernel(o_ref, gate_ref, x_ref, w_ref, out_ref):
    g = o_ref[...].astype(F32) * jax.nn.sigmoid(gate_ref[...].astype(F32))
    out_ref[...] = x_ref[...] + jnp.dot(g.astype(BF16), w_ref[...], preferred_element_type=F32)


def _gated_out(o2, qg, x2, w_o):
    n, d = x2.shape
    hd_all = o2.shape[1]
    tm = _tile(n, 512)
    return pl.pallas_call(
        _gated_out_kernel,
        out_shape=jax.ShapeDtypeStruct((n, d), F32),
        grid=(n // tm,),
        in_specs=[
            pl.BlockSpec((tm, hd_all), lambda i: (i, 0)),
            pl.BlockSpec((tm, hd_all), lambda i: (i, 1)),
            pl.BlockSpec((tm, d), lambda i: (i, 0)),
            pl.BlockSpec((hd_all, d), lambda i: (0, 0)),
        ],
        out_specs=pl.BlockSpec((tm, d), lambda i: (i, 0)),
        compiler_params=_cparams(("parallel",), 48),
        name="gated_out_proj",
    )(o2, qg, x2, w_o)


def kernel(x, a_norm, a_w_in, a_conv_w, a_conv_b, a_w_rec, a_b_rec, a_w_inp, a_b_inp, a_lambda, a_w_out, kv_norm, kv_w, kv_b_forget, b_norm, b_w_qg, b_w_o, m_norm, m_w_group, m_b_group, m_w_router, m_b_router, m_w_in, m_w_out, final_norm):
    b, s, d = x.shape
    n = b * s
    depth = m_norm.shape[0]
    n_a = a_norm.shape[0]
    n_heads = kv_b_forget.shape[0]
    hd_all = b_w_o.shape[1]
    head_dim = hd_all // n_heads
    x2 = x.reshape(n, d)
    k3 = aux = vt = block_end = None
    for layer in range(depth):
        if layer < n_a:
            i = layer
            d_rnn = a_w_out.shape[1]
            proj = _norm_proj(x2, a_norm[i], a_w_in[i].astype(BF16), jnp.ones((2 * d_rnn,), F32),
                              BF16, "rglru_in_proj")
            x2 = _rglru(proj, x2.reshape(b, s, d), a_conv_w[i], a_conv_b[i], a_w_rec[i], a_b_rec[i],
                        a_w_inp[i], a_b_inp[i], a_lambda[i], a_w_out[i]).reshape(n, d)
        else:
            j = layer - n_a
            q_scale = jnp.concatenate([jnp.full((hd_all,), head_dim ** -0.5 * LOG2E, F32),
                                       jnp.ones((hd_all,), F32)])
            qg = _norm_proj(x2, b_norm[j], b_w_qg[j].astype(BF16), q_scale, BF16, "fox_qg_proj")
            o = _attention(qg.reshape(b, s, 2 * hd_all), k3, aux, vt, block_end, n_heads)
            x2 = _gated_out(o.reshape(n, hd_all), qg, x2, b_w_o[j].astype(BF16))
        last = layer == depth - 1
        x2 = _moe(x2, m_norm[layer], m_w_group[layer], m_b_group[layer], m_w_router[layer],
                  m_b_router[layer], m_w_in, m_w_out, layer, final_norm, last)
        if layer == n_a - 1:
            x3 = x2.reshape(b, s, d)
            k3 = _norm_proj(x2, kv_norm, kv_w[:, :hd_all].astype(BF16), jnp.ones((hd_all,), F32),
                            BF16, "shared_k_proj").reshape(b, s, hd_all)
            vt = _norm_proj_t(x3, kv_norm, kv_w[:, hd_all:2 * hd_all].T.astype(BF16), BF16,
                              "shared_vt_proj")
            aux, block_end = _forget_aux(x3, kv_norm, kv_w[:, 2 * hd_all:], kv_b_forget)
    if depth == 0:
        x2 = _rms(x2, final_norm)
    return x2.reshape(b, s, d)
```

```python
import functools
import math

import jax
import jax.numpy as jnp
from jax import lax
from jax.experimental import pallas as pl
from jax.experimental.pallas import tpu as pltpu

F32 = jnp.float32
BF16 = jnp.bfloat16
I32 = jnp.int32

EPS = 1e-6
LRU_C = 8.0
TOP_K = 2
LOG2E = 1.4426950408889634
MASK_VALUE = -1e30

V7X_VMEM_BYTES = 64 * 1024 * 1024
SUBLANES = 8
LANES = 128
MIB = 1024 * 1024


def _cparams(semantics, vmem_mib):
    assert vmem_mib * MIB < V7X_VMEM_BYTES
    return pltpu.CompilerParams(dimension_semantics=semantics, vmem_limit_bytes=vmem_mib * MIB)


def _tile(dim, pref):
    t = min(dim, pref)
    assert dim % t == 0, (dim, pref)
    return t


def _rms(x, gain):
    return x * lax.rsqrt(jnp.mean(x * x, axis=-1, keepdims=True) + EPS) * gain


def _norm_proj_kernel(x_ref, g_ref, w_ref, s_ref, o_ref, hn_ref):
    @pl.when(pl.program_id(1) == 0)
    def _():
        hn_ref[...] = _rms(x_ref[...], g_ref[...]).astype(BF16)

    acc = jnp.dot(hn_ref[...], w_ref[...], preferred_element_type=F32)
    o_ref[...] = (acc * s_ref[...]).astype(o_ref.dtype)


def _norm_proj(x, gain, w, col_scale, out_dtype, name):
    n, d = x.shape
    n_out = w.shape[1]
    tm = _tile(n, 1024)
    tn = _tile(n_out, 1024)
    return pl.pallas_call(
        _norm_proj_kernel,
        out_shape=jax.ShapeDtypeStruct((n, n_out), out_dtype),
        grid=(n // tm, n_out // tn),
        in_specs=[
            pl.BlockSpec((tm, d), lambda i, j: (i, 0)),
            pl.BlockSpec((1, d), lambda i, j: (0, 0)),
            pl.BlockSpec((d, tn), lambda i, j: (0, j)),
            pl.BlockSpec((1, tn), lambda i, j: (0, j)),
        ],
        out_specs=pl.BlockSpec((tm, tn), lambda i, j: (i, j)),
        scratch_shapes=[pltpu.VMEM((tm, d), BF16)],
        compiler_params=_cparams(("parallel", "arbitrary"), 48),
        name=name,
    )(x, gain.reshape(1, d), w, col_scale.reshape(1, n_out))


def _norm_proj_t_kernel(x_ref, g_ref, wt_ref, o_ref, hn_ref):
    @pl.when(pl.program_id(2) == 0)
    def _():
        hn_ref[...] = _rms(x_ref[0], g_ref[...]).astype(BF16)

    acc = lax.dot_general(wt_ref[...], hn_ref[...], (((1,), (1,)), ((), ())),
                          preferred_element_type=F32)
    o_ref[0] = acc.astype(o_ref.dtype)


def _norm_proj_t(x3, gain, w_t, out_dtype, name):
    b, s, d = x3.shape
    n_out = w_t.shape[0]
    tm = _tile(s, 1024)
    tn = _tile(n_out, 1024)
    return pl.pallas_call(
        _norm_proj_t_kernel,
        out_shape=jax.ShapeDtypeStruct((b, n_out, s), out_dtype),
        grid=(b, s // tm, n_out // tn),
        in_specs=[
            pl.BlockSpec((1, tm, d), lambda bi, i, j: (bi, i, 0)),
            pl.BlockSpec((1, d), lambda bi, i, j: (0, 0)),
            pl.BlockSpec((tn, d), lambda bi, i, j: (j, 0)),
        ],
        out_specs=pl.BlockSpec((1, tn, tm), lambda bi, i, j: (bi, j, i)),
        scratch_shapes=[pltpu.VMEM((tm, d), BF16)],
        compiler_params=_cparams(("parallel", "parallel", "arbitrary"), 48),
        name=name,
    )(x3, gain.reshape(1, d), w_t)


SCAN_STRIP = 1024


def _rglru_kernel(u_ref, y_ref, x_ref, cw_ref, cb_ref, wr_ref, br_ref, wi_ref, bi_ref,
                  lam_ref, wo_ref, o_ref, ubuf_ref, a_ref, b_ref, h_ref, *, conv_width):
    ts, d = u_ref.shape
    nb, blk, _ = wr_ref.shape
    halo = SUBLANES

    @pl.when(pl.program_id(1) == 0)
    def _():
        ubuf_ref[0:halo, :] = jnp.zeros((halo, d), F32)
        h_ref[...] = jnp.zeros_like(h_ref)

    u = u_ref[...].astype(F32)
    ubuf_ref[halo:halo + ts, :] = u
    uc = u * cw_ref[conv_width - 1:conv_width, :] + cb_ref[...]
    for j in range(conv_width - 1):
        shift = conv_width - 1 - j
        uc = uc + ubuf_ref[halo - shift:halo - shift + ts, :] * cw_ref[j:j + 1, :]
    ubuf_ref[0:halo, :] = u[ts - halo:, :]

    ub = uc.astype(BF16)
    r_parts, i_parts = [], []
    for n in range(nb):
        ubn = ub[:, n * blk:(n + 1) * blk]
        r_parts.append(jnp.dot(ubn, wr_ref[n], preferred_element_type=F32))
        i_parts.append(jnp.dot(ubn, wi_ref[n], preferred_element_type=F32))
    r = jax.nn.sigmoid(jnp.concatenate(r_parts, axis=1) + br_ref[...])
    gi = jax.nn.sigmoid(jnp.concatenate(i_parts, axis=1) + bi_ref[...])
    lam = lam_ref[...]
    sp = jnp.maximum(-lam, 0.0) + jnp.log(1.0 + jnp.exp(-jnp.abs(lam)))
    log_a = (-LRU_C * r) * sp
    a_ref[...] = jnp.exp(log_a)
    th = jnp.tanh(log_a)
    b_ref[...] = jnp.sqrt(-2.0 * th / (1.0 - th)) * (gi * uc)

    row = lax.broadcasted_iota(I32, (SUBLANES, SCAN_STRIP), 0)
    strip = min(SCAN_STRIP, d)
    for c in range(d // strip):
        cols = pl.ds(c * strip, strip)

        def tile_step(t, h):
            rows = pl.ds(pl.multiple_of(t * SUBLANES, SUBLANES), SUBLANES)
            a = a_ref[rows, cols]
            bb = b_ref[rows, cols]
            for sh in (1, 2, 4):
                keep = row[:, :strip] >= sh
                a_prev = jnp.where(keep, pltpu.roll(a, sh, 0), 1.0)
                b_prev = jnp.where(keep, pltpu.roll(bb, sh, 0), 0.0)
                bb = a * b_prev + bb
                a = a * a_prev
            hs = a * h + bb
            b_ref[rows, cols] = hs
            return jnp.broadcast_to(hs[SUBLANES - 1:SUBLANES, :], (SUBLANES, strip))

        h_ref[:, cols] = lax.fori_loop(0, ts // SUBLANES, tile_step, h_ref[:, cols], unroll=2)

    y = y_ref[...].astype(F32)
    gelu = 0.5 * y * (1.0 + jnp.tanh(math.sqrt(2.0 / math.pi) * (y + 0.044715 * (y * y * y))))
    g = (b_ref[...] * gelu).astype(BF16)
    o_ref[...] = x_ref[...] + jnp.dot(g, wo_ref[...], preferred_element_type=F32)


def _rglru(proj, x3, conv_w, conv_b, w_rec, b_rec, w_inp, b_inp, lam, w_out):
    b, s, d = x3.shape
    d_rnn = w_out.shape[0]
    nb, blk, _ = w_rec.shape
    width = conv_w.shape[0]
    assert width - 1 <= SUBLANES
    ts = _tile(s, 256)
    proj3 = proj.reshape(b, s, 2 * d_rnn)
    row = lambda v: v.reshape(1, -1)
    const2 = lambda bi, i: (0, 0)
    const3 = lambda bi, i: (0, 0, 0)
    return pl.pallas_call(
        functools.partial(_rglru_kernel, conv_width=width),
        out_shape=jax.ShapeDtypeStruct((b, s, d), F32),
        grid=(b, s // ts),
        in_specs=[
            pl.BlockSpec((None, ts, d_rnn), lambda bi, i: (bi, i, 0)),
            pl.BlockSpec((None, ts, d_rnn), lambda bi, i: (bi, i, 1)),
            pl.BlockSpec((None, ts, d), lambda bi, i: (bi, i, 0)),
            pl.BlockSpec((width, d_rnn), const2),
            pl.BlockSpec((1, d_rnn), const2),
            pl.BlockSpec((nb, blk, blk), const3),
            pl.BlockSpec((1, d_rnn), const2),
            pl.BlockSpec((nb, blk, blk), const3),
            pl.BlockSpec((1, d_rnn), const2),
            pl.BlockSpec((1, d_rnn), const2),
            pl.BlockSpec((d_rnn, d), const2),
        ],
        out_specs=pl.BlockSpec((None, ts, d), lambda bi, i: (bi, i, 0)),
        scratch_shapes=[
            pltpu.VMEM((SUBLANES + ts, d_rnn), F32),
            pltpu.VMEM((ts, d_rnn), F32),
            pltpu.VMEM((ts, d_rnn), F32),
            pltpu.VMEM((SUBLANES, d_rnn), F32),
        ],
        compiler_params=_cparams(("parallel", "arbitrary"), 56),
        name="rglru",
    )(proj3, proj3, x3, conv_w, row(conv_b), w_rec.astype(BF16), row(b_rec),
      w_inp.astype(BF16), row(b_inp), row(lam), w_out.astype(BF16))


def _pack_bf16_pair(x):
    c = x.shape[1] // 2
    lo = lax.bitcast_convert_type(x[:, :c].astype(BF16).astype(F32), I32)
    hi = lax.bitcast_convert_type(x[:, c:].astype(BF16).astype(F32), I32)
    return jnp.bitwise_or(hi, lax.shift_right_logical(lo, jnp.int32(16)))


def _unpack_bf16_pair(w):
    lo = lax.bitcast_convert_type(lax.shift_left(w, jnp.int32(16)), F32)
    hi = lax.bitcast_convert_type(jnp.bitwise_and(w, jnp.int32(-65536)), F32)
    return lo, hi


def _split3(v):
    v1 = v.astype(BF16)
    r1 = v - v1.astype(F32)
    v2 = r1.astype(BF16)
    v3 = (r1 - v2.astype(F32)).astype(BF16)
    return v1, v2, v3


def _dot_nt(a, b):
    return lax.dot_general(a, b, (((1,), (1,)), ((), ())), preferred_element_type=F32)


def _first_argmax(v, n):
    idx = lax.broadcasted_iota(I32, v.shape, 0)
    vmax = jnp.max(v, axis=0, keepdims=True)
    amax = jnp.min(jnp.where(v == vmax, idx, n), axis=0, keepdims=True)
    return amax, vmax


def _router_kernel(x_ref, g_ref, w1_ref, w2_ref, w3_ref, bias_ref, hn_ref, eid_ref, wt_ref,
                   rank_ref, cnt_ref, tri_ref, carry_ref, *, n_groups, per_group):
    tm = x_ref.shape[0]
    n_exp = n_groups * per_group

    @pl.when(pl.program_id(0) == 0)
    def _():
        r = lax.broadcasted_iota(I32, (tm, tm), 0)
        c = lax.broadcasted_iota(I32, (tm, tm), 1)
        tri_ref[...] = (r < c).astype(BF16)
        carry_ref[...] = jnp.zeros_like(carry_ref)

    hn = _rms(x_ref[...], g_ref[...])
    hn_ref[...] = _pack_bf16_pair(hn)
    h1, h2, h3 = _split3(hn)
    w1, w2, w3 = w1_ref[...], w2_ref[...], w3_ref[...]
    logits = (_dot_nt(w1, h1) + (_dot_nt(w1, h2) + _dot_nt(w2, h1))
              + (_dot_nt(w1, h3) + _dot_nt(w2, h2) + _dot_nt(w3, h1)))
    logits = logits + bias_ref[:, 0:1]

    gl = logits[0:n_groups, :]
    g_idx, g_max = _first_argmax(gl, n_groups)
    gp_top = 1.0 / jnp.sum(jnp.exp(gl - g_max), axis=0, keepdims=True)
    el = jnp.zeros((per_group, tm), F32)
    for g in range(n_groups):
        lo = n_groups + g * per_group
        el = jnp.where(g_idx == g, logits[lo:lo + per_group, :], el)
    e_max = jnp.max(el, axis=0, keepdims=True)
    ex = jnp.exp(el - e_max)
    ep = ex / jnp.sum(ex, axis=0, keepdims=True)
    e1, p1 = _first_argmax(ep, per_group)
    sub = lax.broadcasted_iota(I32, ep.shape, 0)
    e2, p2 = _first_argmax(jnp.where(sub == e1, -1.0, ep), per_group)
    denom = p1 + p2
    eid1 = g_idx * per_group + e1
    eid2 = g_idx * per_group + e2
    eid_ref[...] = jnp.concatenate([eid1, eid2], axis=0)
    wt_ref[...] = jnp.concatenate([gp_top * p1 / denom, gp_top * p2 / denom], axis=0)

    e_iota = lax.broadcasted_iota(I32, (n_exp, tm), 0)
    hot1 = e_iota == eid1
    hot2 = e_iota == eid2
    chosen = jnp.logical_or(hot1, hot2)
    before = jnp.dot(chosen.astype(BF16), tri_ref[...], preferred_element_type=F32)
    base = (before + carry_ref[:, 0:1]).astype(I32)
    rank1 = jnp.sum(jnp.where(hot1, base, 0), axis=0, keepdims=True)
    rank2 = jnp.sum(jnp.where(hot2, base, 0), axis=0, keepdims=True)
    rank_ref[...] = jnp.concatenate([rank1, rank2], axis=0)
    carry_ref[...] = carry_ref[...] + jnp.sum(chosen.astype(F32), axis=1, keepdims=True)
    cnt_ref[...] = carry_ref[...].astype(I32)


def _router(x2, gain, w_group, b_group, w_router, b_router):
    n, d = x2.shape
    n_groups = w_group.shape[1]
    n_exp = w_router.shape[1]
    per_group = n_exp // n_groups
    assert per_group == SUBLANES and n_groups == SUBLANES
    rows = n_groups + n_exp
    rows_p = -(-rows // LANES) * LANES
    w_t = jnp.concatenate([w_group, w_router], axis=1).T
    w_t = jnp.pad(w_t, ((0, rows_p - rows), (0, 0)))
    w1, w2, w3 = _split3(w_t)
    bias = jnp.pad(jnp.concatenate([b_group, b_router]), (0, rows_p - rows))
    bias = jnp.broadcast_to(bias[:, None], (rows_p, LANES))
    tm = _tile(n, 512)
    const = lambda i: (0, 0)
    return pl.pallas_call(
        functools.partial(_router_kernel, n_groups=n_groups, per_group=per_group),
        out_shape=(
            jax.ShapeDtypeStruct((n, d // 2), I32),
            jax.ShapeDtypeStruct((TOP_K, n), I32),
            jax.ShapeDtypeStruct((TOP_K, n), F32),
            jax.ShapeDtypeStruct((TOP_K, n), I32),
            jax.ShapeDtypeStruct((n_exp, LANES), I32),
        ),
        grid=(n // tm,),
        in_specs=[
            pl.BlockSpec((tm, d), lambda i: (i, 0)),
            pl.BlockSpec((1, d), const),
            pl.BlockSpec((rows_p, d), const),
            pl.BlockSpec((rows_p, d), const),
            pl.BlockSpec((rows_p, d), const),
            pl.BlockSpec((rows_p, LANES), const),
        ],
        out_specs=(
            pl.BlockSpec((tm, d // 2), lambda i: (i, 0)),
            pl.BlockSpec((TOP_K, tm), lambda i: (0, i)),
            pl.BlockSpec((TOP_K, tm), lambda i: (0, i)),
            pl.BlockSpec((TOP_K, tm), lambda i: (0, i)),
            pl.BlockSpec((n_exp, LANES), const),
        ),
        scratch_shapes=[pltpu.VMEM((tm, tm), BF16), pltpu.VMEM((n_exp, LANES), F32)],
        compiler_params=_cparams(("arbitrary",), 40),
        name="moe_router",
    )(x2, gain.reshape(1, d), w1, w2, w3, bias)


def _dest_kernel(eid_ref, rank_ref, start_ref, dest_ref):
    eid = eid_ref[...]
    n_exp = start_ref.shape[0]
    dest = rank_ref[...]
    for k in range(eid.shape[0]):
        hot = lax.broadcasted_iota(I32, (n_exp, eid.shape[1]), 0) == eid[k:k + 1, :]
        off = jnp.sum(jnp.where(hot, start_ref[:, 0:1], 0), axis=0, keepdims=True)
        dest_ref[k:k + 1, :] = dest[k:k + 1, :] + off


def _dest_rows(eid, rank, seg_start):
    k, n = eid.shape
    n_exp = seg_start.shape[0]
    tm = _tile(n, 2048)
    start = jnp.broadcast_to(seg_start[:, None], (n_exp, LANES)).astype(I32)
    return pl.pallas_call(
        _dest_kernel,
        out_shape=jax.ShapeDtypeStruct((k, n), I32),
        grid=(n // tm,),
        in_specs=[
            pl.BlockSpec((k, tm), lambda i: (0, i)),
            pl.BlockSpec((k, tm), lambda i: (0, i)),
            pl.BlockSpec((n_exp, LANES), lambda i: (0, 0)),
        ],
        out_specs=pl.BlockSpec((k, tm), lambda i: (0, i)),
        compiler_params=_cparams(("parallel",), 16),
        name="moe_dest",
    )(eid, rank, start)


def _dispatch_kernel(nv_ref, dest_ref, hn_ref, xs_ref, zero_ref, sem, zsem, *, chunk_rows):
    k, tm = dest_ref.shape[1], dest_ref.shape[2]
    n_chunks = nv_ref.shape[0]

    @pl.when(pl.program_id(0) == 0)
    def _():
        zero_ref[...] = jnp.zeros_like(zero_ref)

        def zero_copy(c):
            rows = pl.ds(pl.multiple_of(c * chunk_rows, chunk_rows), chunk_rows)
            return pltpu.make_async_copy(zero_ref, xs_ref.at[rows], zsem)

        def start(c, carry):
            @pl.when(nv_ref[c] < chunk_rows)
            def _():
                zero_copy(c).start()
            return carry

        def finish(c, carry):
            @pl.when(nv_ref[c] < chunk_rows)
            def _():
                zero_copy(c).wait()
            return carry

        lax.fori_loop(0, n_chunks, start, 0)
        lax.fori_loop(0, n_chunks, finish, 0)


    def issue(t, c):
        for kk in range(k):
            pltpu.make_async_copy(hn_ref.at[pl.ds(t, 1)],
                                  xs_ref.at[pl.ds(dest_ref[0, kk, t], 1)], sem).start()
        return c

    lax.fori_loop(0, tm, issue, 0, unroll=8)
    for kk in range(k):
        pltpu.make_async_copy(hn_ref, xs_ref.at[pl.ds(0, tm)], sem).wait()


def _dispatch(dest, hn, n_valid, n_rows, chunk_rows):
    k, n = dest.shape
    d = hn.shape[1]
    tm = _tile(n, 512)
    dest3 = dest.reshape(k, n // tm, tm).transpose(1, 0, 2)
    grid_spec = pltpu.PrefetchScalarGridSpec(
        num_scalar_prefetch=1,
        grid=(n // tm,),
        in_specs=[
            pl.BlockSpec((1, k, tm), lambda i, nv: (i, 0, 0), memory_space=pltpu.SMEM),
            pl.BlockSpec((tm, d), lambda i, nv: (i, 0)),
        ],
        out_specs=pl.BlockSpec(memory_space=pl.ANY),
        scratch_shapes=[pltpu.VMEM((chunk_rows, d), hn.dtype), pltpu.SemaphoreType.DMA(()),
                        pltpu.SemaphoreType.DMA(())],
    )
    return pl.pallas_call(
        functools.partial(_dispatch_kernel, chunk_rows=chunk_rows),
        out_shape=jax.ShapeDtypeStruct((n_rows, d), hn.dtype),
        grid_spec=grid_spec,
        compiler_params=_cparams(("arbitrary",), 24),
        name="moe_dispatch",
    )(n_valid, dest3, hn)


def _expert_kernel(ce_ref, nu_ref, fresh_ref, xs_ref, wi_ref, wo_ref, ys_ref,
                   wib_ref, wob_ref, *, d_expert):
    i = pl.program_id(0)

    @pl.when(i < nu_ref[0])
    def _():
        @pl.when(fresh_ref[i] == 1)
        def _():
            wib_ref[...] = wi_ref[...].astype(BF16)
            wob_ref[...] = wo_ref[...].astype(BF16)

        x_lo, x_hi = _unpack_bf16_pair(xs_ref[...])
        xb = jnp.concatenate([x_lo, x_hi], axis=1).astype(BF16)
        gu = jnp.dot(xb, wib_ref[...], preferred_element_type=F32)
        gate = gu[:, :d_expert]
        act = (gate * jax.nn.sigmoid(gate) * gu[:, d_expert:]).astype(BF16)
        ys_ref[...] = _pack_bf16_pair(jnp.dot(act, wob_ref[...], preferred_element_type=F32))

    @pl.when(i >= nu_ref[0])
    def _():
        ys_ref[...] = jnp.zeros_like(ys_ref)


def _experts(xs, chunk_e, n_used, fresh, w_in_all, w_out_all, layer, chunk_rows):
    n_rows, d_packed = xs.shape
    d = 2 * d_packed
    d_expert = w_out_all.shape[2]
    n_chunks = n_rows // chunk_rows
    xs_map = lambda i, ce, nu, fr: (jnp.minimum(i, jnp.maximum(nu[0] - 1, 0)), 0)
    w_map = lambda i, ce, nu, fr: (layer, ce[i], 0, 0)
    grid_spec = pltpu.PrefetchScalarGridSpec(
        num_scalar_prefetch=3,
        grid=(n_chunks,),
        in_specs=[
            pl.BlockSpec((chunk_rows, d_packed), xs_map),
            pl.BlockSpec((None, None, d, 2 * d_expert), w_map),
            pl.BlockSpec((None, None, d_expert, d), w_map),
        ],
        out_specs=pl.BlockSpec((chunk_rows, d_packed), lambda i, ce, nu, fr: (i, 0)),
        scratch_shapes=[pltpu.VMEM((d, 2 * d_expert), BF16), pltpu.VMEM((d_expert, d), BF16)],
    )
    return pl.pallas_call(
        functools.partial(_expert_kernel, d_expert=d_expert),
        out_shape=jax.ShapeDtypeStruct((n_rows, d_packed), I32),
        grid_spec=grid_spec,
        compiler_params=_cparams(("arbitrary",), 56),
        name="moe_experts",
    )(chunk_e, n_used, fresh, xs, w_in_all, w_out_all)


def _combine_kernel(dest_ref, next_ref, x_ref, wt_ref, g_ref, ys_ref, o_ref, buf_ref, sem_ref, *,
                    final_norm, n_blocks):
    k, tm = dest_ref.shape[1], dest_ref.shape[2]
    i = pl.program_id(0)

    def issue_block(d_ref, slot):
        def body(t, c):
            for kk in range(k):
                pltpu.make_async_copy(ys_ref.at[pl.ds(d_ref[0, kk, t], 1)],
                                      buf_ref.at[slot, kk, pl.ds(t, 1)], sem_ref.at[slot]).start()
            return c
        lax.fori_loop(0, tm, body, 0, unroll=8)

    @pl.when(i == 0)
    def _():
        issue_block(dest_ref, 0)

    @pl.when(i + 1 < n_blocks)
    def _():
        issue_block(next_ref, (i + 1) % 2)

    slot = i % 2
    for kk in range(k):
        pltpu.make_async_copy(ys_ref.at[pl.ds(0, tm)], buf_ref.at[slot, kk], sem_ref.at[slot]).wait()

    c = buf_ref.shape[3]
    out_lo = x_ref[:, :c]
    out_hi = x_ref[:, c:]
    for kk in range(k):
        y_lo, y_hi = _unpack_bf16_pair(buf_ref[slot, kk])
        w = wt_ref[:, kk:kk + 1]
        out_lo = out_lo + w * y_lo
        out_hi = out_hi + w * y_hi
    if final_norm:
        ms = (jnp.sum(out_lo * out_lo, axis=-1, keepdims=True)
              + jnp.sum(out_hi * out_hi, axis=-1, keepdims=True)) / (2 * c)
        scale = lax.rsqrt(ms + EPS)
        out_lo = out_lo * scale * g_ref[:, :c]
        out_hi = out_hi * scale * g_ref[:, c:]
    o_ref[:, :c] = out_lo
    o_ref[:, c:] = out_hi


def _combine(dest, x2, wts, ys, gain, final_norm):
    k, n = dest.shape
    d = x2.shape[1]
    tm = _tile(n, 512)
    n_blocks = n // tm
    dest3 = dest.reshape(k, n_blocks, tm).transpose(1, 0, 2)
    return pl.pallas_call(
        functools.partial(_combine_kernel, final_norm=final_norm, n_blocks=n_blocks),
        out_shape=jax.ShapeDtypeStruct((n, d), F32),
        grid=(n_blocks,),
        in_specs=[
            pl.BlockSpec((1, k, tm), lambda i: (i, 0, 0), memory_space=pltpu.SMEM),
            pl.BlockSpec((1, k, tm), lambda i: (jnp.minimum(i + 1, n_blocks - 1), 0, 0),
                         memory_space=pltpu.SMEM),
            pl.BlockSpec((tm, d), lambda i: (i, 0)),
            pl.BlockSpec((tm, k), lambda i: (i, 0)),
            pl.BlockSpec((1, d), lambda i: (0, 0)),
            pl.BlockSpec(memory_space=pl.ANY),
        ],
        out_specs=pl.BlockSpec((tm, d), lambda i: (i, 0)),
        scratch_shapes=[pltpu.VMEM((2, k, tm, d // 2), I32), pltpu.SemaphoreType.DMA((2,))],
        compiler_params=_cparams(("arbitrary",), 40),
        name="moe_combine",
    )(dest3, dest3, x2, wts.T, gain.reshape(1, d), ys)


EXPERT_CHUNK_ROWS = 256


def _moe(x2, norm_gain, w_group, b_group, w_router, b_router, w_in_all, w_out_all, layer, out_gain,
         final_norm):
    n, d = x2.shape
    n_exp = w_router.shape[1]
    chunk = EXPERT_CHUNK_ROWS
    hn, eid, wts, rank, counts = _router(x2, norm_gain, w_group, b_group, w_router, b_router)
    counts = counts[:, 0]
    padded = (counts + chunk - 1) // chunk * chunk
    seg_end = jnp.cumsum(padded)
    seg_start = seg_end - padded
    n_rows = (-(-(n * TOP_K) // chunk) + n_exp) * chunk
    n_chunks = n_rows // chunk
    chunk_start = jnp.arange(n_chunks, dtype=I32) * chunk
    n_used = (seg_end[-1:] // chunk).astype(I32)
    live_start = jnp.minimum(chunk_start, jnp.maximum(seg_end[-1] - chunk, 0))
    chunk_e = jnp.sum(seg_end[None, :] <= live_start[:, None], axis=1).astype(I32)
    chunk_e = jnp.minimum(chunk_e, n_exp - 1)
    n_valid = jnp.clip(counts[chunk_e] - (chunk_start - seg_start[chunk_e]), 0, chunk).astype(I32)
    fresh = jnp.concatenate([jnp.ones((1,), I32), (chunk_e[1:] != chunk_e[:-1]).astype(I32)])
    dest = _dest_rows(eid, rank, seg_start)
    xs = _dispatch(dest, hn, n_valid, n_rows, chunk)
    ys = _experts(xs, chunk_e, n_used, fresh, w_in_all, w_out_all, layer, chunk)
    return _combine(dest, x2, wts, ys, out_gain, final_norm)


def _forget_kernel(x_ref, g_ref, w_ref, b_ref, aux_ref, end_ref, tri_ref, carry_ref, *, n_heads):
    ts = x_ref.shape[0]

    @pl.when(pl.program_id(1) == 0)
    def _():
        r = lax.broadcasted_iota(I32, (ts, ts), 0)
        c = lax.broadcasted_iota(I32, (ts, ts), 1)
        tri_ref[...] = (c <= r).astype(BF16)
        carry_ref[...] = jnp.zeros_like(carry_ref)

    hn = _rms(x_ref[...], g_ref[...]).astype(BF16)
    f = jnp.dot(hn, w_ref[...], preferred_element_type=F32) + b_ref[...]
    logf = jnp.minimum(f, 0.0) - jnp.log(1.0 + jnp.exp(-jnp.abs(f)))
    l1, l2, l3 = _split3(logf)
    tri = tri_ref[...]
    cum = (jnp.dot(tri, l1, preferred_element_type=F32)
           + jnp.dot(tri, l2, preferred_element_type=F32)
           + jnp.dot(tri, l3, preferred_element_type=F32)) + carry_ref[0:1, :]
    carry_ref[...] = jnp.broadcast_to(cum[ts - 1:ts, :], carry_ref.shape)
    end_ref[...] = carry_ref[...] * (-LOG2E)
    c1, c2, c3 = [c.astype(F32) for c in _split3(cum * (-LOG2E))]
    lane = lax.broadcasted_iota(I32, (ts, LANES), 1)
    for h in range(n_heads):
        col = lambda v: jnp.broadcast_to(v[:, h:h + 1], (ts, LANES))
        aux = jnp.where(lane == 0, col(c1), jnp.where(lane == 1, col(c2),
                        jnp.where(lane == 2, col(c3), 0.0)))
        aux_ref[h] = aux.astype(BF16)


def _forget_aux(x3, gain, w_f, b_f):
    b, s, d = x3.shape
    n_heads = w_f.shape[1]
    assert n_heads <= LANES
    w_p = jnp.pad(w_f, ((0, 0), (0, LANES - n_heads))).astype(BF16)
    b_p = jnp.pad(b_f, (0, LANES - n_heads)).reshape(1, LANES)
    ts = _tile(s, ATTN_BLOCK)
    return pl.pallas_call(
        functools.partial(_forget_kernel, n_heads=n_heads),
        out_shape=(jax.ShapeDtypeStruct((b, n_heads, s, LANES), BF16),
                   jax.ShapeDtypeStruct((b, s // ts, SUBLANES, LANES), F32)),
        grid=(b, s // ts),
        in_specs=[
            pl.BlockSpec((None, ts, d), lambda bi, i: (bi, i, 0)),
            pl.BlockSpec((1, d), lambda bi, i: (0, 0)),
            pl.BlockSpec((d, LANES), lambda bi, i: (0, 0)),
            pl.BlockSpec((1, LANES), lambda bi, i: (0, 0)),
        ],
        out_specs=(pl.BlockSpec((None, n_heads, ts, LANES), lambda bi, i: (bi, 0, i, 0)),
                   pl.BlockSpec((None, None, SUBLANES, LANES), lambda bi, i: (bi, i, 0, 0))),
        scratch_shapes=[pltpu.VMEM((ts, ts), BF16), pltpu.VMEM((SUBLANES, LANES), F32)],
        compiler_params=_cparams(("parallel", "arbitrary"), 32),
        name="forget_cumsum",
    )(x3, gain.reshape(1, d), w_p, b_p)


ATTN_BLOCK = 512
ATTN_GROUP_BLOCKS = 4
FINITE_LIMIT = 3.0e38
SKIP_MARGIN = 160.0


def _attn_kernel(end_ref, q_ref, k_ref, aux_ref, vt_ref, o_ref, qp_ref, m_ref, l_ref, acc_ref,
                 kmax_ref):
    t, hd = q_ref.shape
    i = pl.program_id(2)
    lane = lax.broadcasted_iota(I32, (t, LANES), 1)
    qp_ref[:, 0:hd] = q_ref[...]
    qp_ref[:, hd:hd + LANES] = jnp.where(lane < 3, 1.0, 0.0).astype(BF16)

    def max_row_norm(x):
        xf = x.astype(F32)
        return jnp.sum(xf * xf, axis=1, keepdims=True)

    @pl.when(i == 0)
    def _():
        def body(c, best):
            rows = pl.ds(pl.multiple_of(c * t, t), t)
            return jnp.maximum(best, max_row_norm(k_ref[rows, :]))
        best = lax.fori_loop(0, k_ref.shape[0] // t, body, jnp.zeros((t, 1), F32))
        kmax_ref[0] = jnp.sqrt(jnp.max(best))

    reach = jnp.sqrt(jnp.max(max_row_norm(q_ref[...]))) * kmax_ref[0]

    def alive(block):
        bias_top = end_ref[0, jnp.maximum(block, 0)]
        return jnp.logical_and(block >= 0,
                               reach + bias_top - jnp.min(m_ref[...]) > -SKIP_MARGIN)

    def scores(start, size, masked):
        rows = pl.ds(pl.multiple_of(start, t), size)
        kp = jnp.concatenate([k_ref[rows, :], aux_ref[rows, :]], axis=1)
        s_t = _dot_nt(kp, qp_ref[...])
        if masked:
            kr = lax.broadcasted_iota(I32, (size, t), 0)
            qc = lax.broadcasted_iota(I32, (size, t), 1)
            s_t = jnp.where(kr <= qc, s_t, MASK_VALUE)
        return s_t, rows

    def exact_block(start, masked):
        s_t, rows = scores(start, t, masked)
        m_old = m_ref[...]
        m_new = jnp.maximum(m_old, jnp.max(s_t, axis=0, keepdims=True))
        alpha = jnp.exp2(m_old - m_new)
        p = jnp.exp2(s_t - m_new)
        l_ref[...] = alpha * l_ref[...] + jnp.sum(p, axis=0, keepdims=True)
        pv = jnp.dot(vt_ref[:, rows], p.astype(BF16), preferred_element_type=F32)
        acc_ref[...] = alpha * acc_ref[...] + pv
        m_ref[...] = m_new

    def lagged_group(start, size):
        s_t, rows = scores(start, size, False)
        m_old = m_ref[...]
        p = jnp.exp2(s_t - m_old)
        pv = jnp.dot(vt_ref[:, rows], p.astype(BF16), preferred_element_type=F32)
        m_new = jnp.maximum(m_old, jnp.max(s_t, axis=0, keepdims=True))
        alpha = jnp.exp2(m_old - m_new)
        l_ref[...] = (l_ref[...] + jnp.sum(p, axis=0, keepdims=True)) * alpha
        acc_ref[...] = (acc_ref[...] + pv) * alpha
        m_ref[...] = m_new

    def reset():
        m_ref[...] = jnp.full_like(m_ref, MASK_VALUE)
        l_ref[...] = jnp.zeros_like(l_ref)
        acc_ref[...] = jnp.zeros_like(acc_ref)

    def finish():
        out = acc_ref[...] / l_ref[...]
        o_ref[...] = out.T.astype(o_ref.dtype)
        return out

    reset()
    exact_block(i * t, True)
    gb = ATTN_GROUP_BLOCKS
    n_groups = i // gb
    rest = i - n_groups * gb

    def group_cond(c):
        g, live = c
        return jnp.logical_and(g < n_groups, live)

    def group_body(c):
        g, _ = c
        first = i - (g + 1) * gb
        lagged_group(first * t, gb * t)
        return g + 1, alive(first - 1)

    _, live = lax.while_loop(group_cond, group_body, (0, alive(i - 1)))

    @pl.when(jnp.logical_and((rest & 2) != 0, live))
    def _():
        lagged_group((rest - 2) * t, 2 * t)

    @pl.when(jnp.logical_and((rest & 1) != 0, jnp.logical_and(live, alive(0))))
    def _():
        lagged_group(0, t)

    out = finish()
    overflowed = jnp.max(jnp.where(jnp.abs(out) < FINITE_LIMIT, 0.0, 1.0)) > 0.0

    @pl.when(overflowed)
    def _():
        reset()

        def exact_body(j, c):
            exact_block(j * t, False)
            return c

        lax.fori_loop(0, i, exact_body, 0)
        exact_block(i * t, True)
        finish()


def _attention(qg3, k3, aux, vt, block_end, n_heads):
    b, s, _ = k3.shape
    hd = k3.shape[2] // n_heads
    assert hd == LANES
    t = _tile(s, ATTN_BLOCK)
    end_tab = block_end[:, :, 0, :n_heads].transpose(0, 2, 1).reshape(b, n_heads, 1, s // t)
    return pl.pallas_call(
        _attn_kernel,
        out_shape=jax.ShapeDtypeStruct((b, s, n_heads * hd), BF16),
        grid=(b, n_heads, s // t),
        in_specs=[
            pl.BlockSpec((None, None, 1, s // t), lambda bi, h, i: (bi, h, 0, 0),
                         memory_space=pltpu.SMEM),
            pl.BlockSpec((None, t, hd), lambda bi, h, i: (bi, i, h)),
            pl.BlockSpec((None, s, hd), lambda bi, h, i: (bi, 0, h)),
            pl.BlockSpec((None, None, s, LANES), lambda bi, h, i: (bi, h, 0, 0)),
            pl.BlockSpec((None, hd, s), lambda bi, h, i: (bi, h, 0)),
        ],
        out_specs=pl.BlockSpec((None, t, hd), lambda bi, h, i: (bi, i, h)),
        scratch_shapes=[
            pltpu.VMEM((t, hd + LANES), BF16),
            pltpu.VMEM((1, t), F32),
            pltpu.VMEM((1, t), F32),
            pltpu.VMEM((hd, t), F32),
            pltpu.SMEM((1,), F32),
        ],
        compiler_params=_cparams(("parallel", "parallel", "arbitrary"), 48),
        name="fox_attention",
    )(end_tab, qg3, k3, aux, vt)


def _gated_out_kernel(o_ref, gate_ref, x_ref, w_ref, out_ref):
    g = o_ref[...].astype(F32) * jax.nn.sigmoid(gate_ref[...].astype(F32))
    out_ref[...] = x_ref[...] + jnp.dot(g.astype(BF16), w_ref[...], preferred_element_type=F32)


def _gated_out(o2, qg, x2, w_o):
    n, d = x2.shape
    hd_all = o2.shape[1]
    tm = _tile(n, 512)
    return pl.pallas_call(
        _gated_out_kernel,
        out_shape=jax.ShapeDtypeStruct((n, d), F32),
        grid=(n // tm,),
        in_specs=[
            pl.BlockSpec((tm, hd_all), lambda i: (i, 0)),
            pl.BlockSpec((tm, hd_all), lambda i: (i, 1)),
            pl.BlockSpec((tm, d), lambda i: (i, 0)),
            pl.BlockSpec((hd_all, d), lambda i: (0, 0)),
        ],
        out_specs=pl.BlockSpec((tm, d), lambda i: (i, 0)),
        compiler_params=_cparams(("parallel",), 48),
        name="gated_out_proj",
    )(o2, qg, x2, w_o)


def kernel(x, a_norm, a_w_in, a_conv_w, a_conv_b, a_w_rec, a_b_rec, a_w_inp, a_b_inp, a_lambda, a_w_out, kv_norm, kv_w, kv_b_forget, b_norm, b_w_qg, b_w_o, m_norm, m_w_group, m_b_group, m_w_router, m_b_router, m_w_in, m_w_out, final_norm):
    b, s, d = x.shape
    n = b * s
    depth = m_norm.shape[0]
    n_a = a_norm.shape[0]
    n_heads = kv_b_forget.shape[0]
    hd_all = b_w_o.shape[1]
    head_dim = hd_all // n_heads
    x2 = x.reshape(n, d)
    k3 = aux = vt = block_end = None
    for layer in range(depth):
        if layer < n_a:
            i = layer
            d_rnn = a_w_out.shape[1]
            proj = _norm_proj(x2, a_norm[i], a_w_in[i].astype(BF16), jnp.ones((2 * d_rnn,), F32),
                              BF16, "rglru_in_proj")
            x2 = _rglru(proj, x2.reshape(b, s, d), a_conv_w[i], a_conv_b[i], a_w_rec[i], a_b_rec[i],
                        a_w_inp[i], a_b_inp[i], a_lambda[i], a_w_out[i]).reshape(n, d)
        else:
            j = layer - n_a
            q_scale = jnp.concatenate([jnp.full((hd_all,), head_dim ** -0.5 * LOG2E, F32),
                                       jnp.ones((hd_all,), F32)])
            qg = _norm_proj(x2, b_norm[j], b_w_qg[j].astype(BF16), q_scale, BF16, "fox_qg_proj")
            o = _attention(qg.reshape(b, s, 2 * hd_all), k3, aux, vt, block_end, n_heads)
            x2 = _gated_out(o.reshape(n, hd_all), qg, x2, b_w_o[j].astype(BF16))
        last = layer == depth - 1
        x2 = _moe(x2, m_norm[layer], m_w_group[layer], m_b_group[layer], m_w_router[layer],
                  m_b_router[layer], m_w_in, m_w_out, layer, final_norm, last)
        if layer == n_a - 1:
            x3 = x2.reshape(b, s, d)
            k3 = _norm_proj(x2, kv_norm, kv_w[:, :hd_all].astype(BF16), jnp.ones((hd_all,), F32),
                            BF16, "shared_k_proj").reshape(b, s, hd_all)
            vt = _norm_proj_t(x3, kv_norm, kv_w[:, hd_all:2 * hd_all].T.astype(BF16), BF16,
                              "shared_vt_proj")
            aux, block_end = _forget_aux(x3, kv_norm, kv_w[:, 2 * hd_all:], kv_b_forget)
    if depth == 0:
        x2 = _rms(x2, final_norm)
    return x2.reshape(b, s, d)
```

```python
import functools
import math

import jax
import jax.numpy as jnp
from jax import lax
from jax.experimental import pallas as pl
from jax.experimental.pallas import tpu as pltpu

F32 = jnp.float32
BF16 = jnp.bfloat16
I32 = jnp.int32

EPS = 1e-6
LRU_C = 8.0
TOP_K = 2
LOG2E = 1.4426950408889634
MASK_VALUE = -1e30

V7X_VMEM_BYTES = 64 * 1024 * 1024
SUBLANES = 8
LANES = 128
MIB = 1024 * 1024


def _cparams(semantics, vmem_mib):
    assert vmem_mib * MIB < V7X_VMEM_BYTES
    return pltpu.CompilerParams(dimension_semantics=semantics, vmem_limit_bytes=vmem_mib * MIB)


def _tile(dim, pref):
    t = min(dim, pref)
    assert dim % t == 0, (dim, pref)
    return t


def _rms(x, gain):
    return x * lax.rsqrt(jnp.mean(x * x, axis=-1, keepdims=True) + EPS) * gain


def _sigmoid(x):
    return 0.5 * jnp.tanh(0.5 * x) + 0.5


def _norm_proj_kernel(x_ref, g_ref, w_ref, s_ref, o_ref, hn_ref):
    @pl.when(pl.program_id(1) == 0)
    def _():
        hn_ref[...] = _rms(x_ref[...], g_ref[...]).astype(BF16)

    acc = jnp.dot(hn_ref[...], w_ref[...], preferred_element_type=F32)
    o_ref[...] = (acc * s_ref[...]).astype(o_ref.dtype)


def _norm_proj(x, gain, w, col_scale, out_dtype, name):
    n, d = x.shape
    n_out = w.shape[1]
    tm = _tile(n, 1024)
    tn = _tile(n_out, 1024)
    return pl.pallas_call(
        _norm_proj_kernel,
        out_shape=jax.ShapeDtypeStruct((n, n_out), out_dtype),
        grid=(n // tm, n_out // tn),
        in_specs=[
            pl.BlockSpec((tm, d), lambda i, j: (i, 0)),
            pl.BlockSpec((1, d), lambda i, j: (0, 0)),
            pl.BlockSpec((d, tn), lambda i, j: (0, j)),
            pl.BlockSpec((1, tn), lambda i, j: (0, j)),
        ],
        out_specs=pl.BlockSpec((tm, tn), lambda i, j: (i, j)),
        scratch_shapes=[pltpu.VMEM((tm, d), BF16)],
        compiler_params=_cparams(("parallel", "arbitrary"), 48),
        name=name,
    )(x, gain.reshape(1, d), w, col_scale.reshape(1, n_out))


SCAN_STRIP = 1024


def _rglru_kernel(u_ref, y_ref, x_ref, cw_ref, cb_ref, wr_ref, br_ref, wi_ref, bi_ref,
                  lam_ref, wo_ref, o_ref, ubuf_ref, a_ref, b_ref, h_ref, *, conv_width):
    ts, d = u_ref.shape
    nb, blk, _ = wr_ref.shape
    halo = SUBLANES

    @pl.when(pl.program_id(1) == 0)
    def _():
        ubuf_ref[0:halo, :] = jnp.zeros((halo, d), F32)
        h_ref[...] = jnp.zeros_like(h_ref)

    u = u_ref[...].astype(F32)
    ubuf_ref[halo:halo + ts, :] = u
    uc = u * cw_ref[conv_width - 1:conv_width, :] + cb_ref[...]
    for j in range(conv_width - 1):
        shift = conv_width - 1 - j
        uc = uc + ubuf_ref[halo - shift:halo - shift + ts, :] * cw_ref[j:j + 1, :]
    ubuf_ref[0:halo, :] = u[ts - halo:, :]

    ub = uc.astype(BF16)
    r_parts, i_parts = [], []
    for n in range(nb):
        ubn = ub[:, n * blk:(n + 1) * blk]
        r_parts.append(jnp.dot(ubn, wr_ref[n], preferred_element_type=F32))
        i_parts.append(jnp.dot(ubn, wi_ref[n], preferred_element_type=F32))
    r = _sigmoid(jnp.concatenate(r_parts, axis=1) + br_ref[...])
    gi = _sigmoid(jnp.concatenate(i_parts, axis=1) + bi_ref[...])
    lam = lam_ref[...]
    sp = jnp.maximum(-lam, 0.0) + jnp.log(1.0 + jnp.exp(-jnp.abs(lam)))
    log_a = (-LRU_C * r) * sp
    th = jnp.tanh(0.5 * log_a)
    inv = 1.0 / (1.0 - th)
    a_ref[...] = (1.0 + th) * inv
    b_ref[...] = (2.0 * inv) * jnp.sqrt(-th) * (gi * uc)

    row = lax.broadcasted_iota(I32, (SUBLANES, SCAN_STRIP), 0)
    strip = min(SCAN_STRIP, d)
    for c in range(d // strip):
        cols = pl.ds(c * strip, strip)

        def tile_step(t, h):
            rows = pl.ds(pl.multiple_of(t * SUBLANES, SUBLANES), SUBLANES)
            a = a_ref[rows, cols]
            bb = b_ref[rows, cols]
            for sh in (1, 2, 4):
                keep = row[:, :strip] >= sh
                a_prev = jnp.where(keep, pltpu.roll(a, sh, 0), 1.0)
                b_prev = jnp.where(keep, pltpu.roll(bb, sh, 0), 0.0)
                bb = a * b_prev + bb
                a = a * a_prev
            hs = a * h + bb
            b_ref[rows, cols] = hs
            return jnp.broadcast_to(hs[SUBLANES - 1:SUBLANES, :], (SUBLANES, strip))

        h_ref[:, cols] = lax.fori_loop(0, ts // SUBLANES, tile_step, h_ref[:, cols], unroll=2)

    y = y_ref[...].astype(F32)
    gelu = 0.5 * y * (1.0 + jnp.tanh(math.sqrt(2.0 / math.pi) * (y + 0.044715 * (y * y * y))))
    g = (b_ref[...] * gelu).astype(BF16)
    o_ref[...] = x_ref[...] + jnp.dot(g, wo_ref[...], preferred_element_type=F32)


def _rglru(proj, x3, conv_w, conv_b, w_rec, b_rec, w_inp, b_inp, lam, w_out):
    b, s, d = x3.shape
    d_rnn = w_out.shape[0]
    nb, blk, _ = w_rec.shape
    width = conv_w.shape[0]
    assert width - 1 <= SUBLANES
    ts = _tile(s, 256)
    proj3 = proj.reshape(b, s, 2 * d_rnn)
    row = lambda v: v.reshape(1, -1)
    const2 = lambda bi, i: (0, 0)
    const3 = lambda bi, i: (0, 0, 0)
    return pl.pallas_call(
        functools.partial(_rglru_kernel, conv_width=width),
        out_shape=jax.ShapeDtypeStruct((b, s, d), F32),
        grid=(b, s // ts),
        in_specs=[
            pl.BlockSpec((None, ts, d_rnn), lambda bi, i: (bi, i, 0)),
            pl.BlockSpec((None, ts, d_rnn), lambda bi, i: (bi, i, 1)),
            pl.BlockSpec((None, ts, d), lambda bi, i: (bi, i, 0)),
            pl.BlockSpec((width, d_rnn), const2),
            pl.BlockSpec((1, d_rnn), const2),
            pl.BlockSpec((nb, blk, blk), const3),
            pl.BlockSpec((1, d_rnn), const2),
            pl.BlockSpec((nb, blk, blk), const3),
            pl.BlockSpec((1, d_rnn), const2),
            pl.BlockSpec((1, d_rnn), const2),
            pl.BlockSpec((d_rnn, d), const2),
        ],
        out_specs=pl.BlockSpec((None, ts, d), lambda bi, i: (bi, i, 0)),
        scratch_shapes=[
            pltpu.VMEM((SUBLANES + ts, d_rnn), F32),
            pltpu.VMEM((ts, d_rnn), F32),
            pltpu.VMEM((ts, d_rnn), F32),
            pltpu.VMEM((SUBLANES, d_rnn), F32),
        ],
        compiler_params=_cparams(("parallel", "arbitrary"), 56),
        name="rglru",
    )(proj3, proj3, x3, conv_w, row(conv_b), w_rec.astype(BF16), row(b_rec),
      w_inp.astype(BF16), row(b_inp), row(lam), w_out.astype(BF16))


def _pack_bf16_pair(x):
    c = x.shape[1] // 2
    lo = lax.bitcast_convert_type(x[:, :c].astype(BF16).astype(F32), I32)
    hi = lax.bitcast_convert_type(x[:, c:].astype(BF16).astype(F32), I32)
    return jnp.bitwise_or(hi, lax.shift_right_logical(lo, jnp.int32(16)))


def _unpack_bf16_pair(w):
    lo = lax.bitcast_convert_type(lax.shift_left(w, jnp.int32(16)), F32)
    hi = lax.bitcast_convert_type(jnp.bitwise_and(w, jnp.int32(-65536)), F32)
    return lo, hi


def _split3(v):
    v1 = v.astype(BF16)
    r1 = v - v1.astype(F32)
    v2 = r1.astype(BF16)
    v3 = (r1 - v2.astype(F32)).astype(BF16)
    return v1, v2, v3


def _dot_nt(a, b):
    return lax.dot_general(a, b, (((1,), (1,)), ((), ())), preferred_element_type=F32)


def _first_argmax(v, n):
    idx = lax.broadcasted_iota(I32, v.shape, 0)
    vmax = jnp.max(v, axis=0, keepdims=True)
    amax = jnp.min(jnp.where(v == vmax, idx, n), axis=0, keepdims=True)
    return amax, vmax


def _router_kernel(x_ref, g_ref, w1_ref, w2_ref, w3_ref, bias_ref, hn_ref, eid_ref, wt_ref,
                   rank_ref, cnt_ref, tri_ref, carry_ref, *, n_groups, per_group):
    tm = x_ref.shape[0]
    n_exp = n_groups * per_group

    @pl.when(pl.program_id(0) == 0)
    def _():
        r = lax.broadcasted_iota(I32, (tm, tm), 0)
        c = lax.broadcasted_iota(I32, (tm, tm), 1)
        tri_ref[...] = (r < c).astype(BF16)
        carry_ref[...] = jnp.zeros_like(carry_ref)

    hn = _rms(x_ref[...], g_ref[...])
    hn_ref[...] = _pack_bf16_pair(hn)
    h1, h2, h3 = _split3(hn)
    w1, w2, w3 = w1_ref[...], w2_ref[...], w3_ref[...]
    logits = (_dot_nt(w1, h1) + (_dot_nt(w1, h2) + _dot_nt(w2, h1))
              + (_dot_nt(w1, h3) + _dot_nt(w2, h2) + _dot_nt(w3, h1)))
    logits = logits + bias_ref[:, 0:1]

    gl = logits[0:n_groups, :]
    g_idx, g_max = _first_argmax(gl, n_groups)
    gp_top = 1.0 / jnp.sum(jnp.exp(gl - g_max), axis=0, keepdims=True)
    el = jnp.zeros((per_group, tm), F32)
    for g in range(n_groups):
        lo = n_groups + g * per_group
        el = jnp.where(g_idx == g, logits[lo:lo + per_group, :], el)
    e_max = jnp.max(el, axis=0, keepdims=True)
    ex = jnp.exp(el - e_max)
    ep = ex / jnp.sum(ex, axis=0, keepdims=True)
    e1, p1 = _first_argmax(ep, per_group)
    sub = lax.broadcasted_iota(I32, ep.shape, 0)
    e2, p2 = _first_argmax(jnp.where(sub == e1, -1.0, ep), per_group)
    denom = p1 + p2
    eid1 = g_idx * per_group + e1
    eid2 = g_idx * per_group + e2
    eid_ref[...] = jnp.concatenate([eid1, eid2], axis=0)
    wt_ref[...] = jnp.concatenate([gp_top * p1 / denom, gp_top * p2 / denom], axis=0)

    e_iota = lax.broadcasted_iota(I32, (n_exp, tm), 0)
    hot1 = e_iota == eid1
    hot2 = e_iota == eid2
    chosen = jnp.logical_or(hot1, hot2)
    before = jnp.dot(chosen.astype(BF16), tri_ref[...], preferred_element_type=F32)
    base = (before + carry_ref[:, 0:1]).astype(I32)
    rank1 = jnp.sum(jnp.where(hot1, base, 0), axis=0, keepdims=True)
    rank2 = jnp.sum(jnp.where(hot2, base, 0), axis=0, keepdims=True)
    rank_ref[...] = jnp.concatenate([rank1, rank2], axis=0)
    carry_ref[...] = carry_ref[...] + jnp.sum(chosen.astype(F32), axis=1, keepdims=True)
    cnt_ref[...] = carry_ref[...].astype(I32)


def _router(x2, gain, w_group, b_group, w_router, b_router):
    n, d = x2.shape
    n_groups = w_group.shape[1]
    n_exp = w_router.shape[1]
    per_group = n_exp // n_groups
    assert per_group == SUBLANES and n_groups == SUBLANES
    rows = n_groups + n_exp
    rows_p = -(-rows // LANES) * LANES
    w_t = jnp.concatenate([w_group, w_router], axis=1).T
    w_t = jnp.pad(w_t, ((0, rows_p - rows), (0, 0)))
    w1, w2, w3 = _split3(w_t)
    bias = jnp.pad(jnp.concatenate([b_group, b_router]), (0, rows_p - rows))
    bias = jnp.broadcast_to(bias[:, None], (rows_p, LANES))
    tm = _tile(n, 512)
    const = lambda i: (0, 0)
    return pl.pallas_call(
        functools.partial(_router_kernel, n_groups=n_groups, per_group=per_group),
        out_shape=(
            jax.ShapeDtypeStruct((n, d // 2), I32),
            jax.ShapeDtypeStruct((TOP_K, n), I32),
            jax.ShapeDtypeStruct((TOP_K, n), F32),
            jax.ShapeDtypeStruct((TOP_K, n), I32),
            jax.ShapeDtypeStruct((n_exp, LANES), I32),
        ),
        grid=(n // tm,),
        in_specs=[
            pl.BlockSpec((tm, d), lambda i: (i, 0)),
            pl.BlockSpec((1, d), const),
            pl.BlockSpec((rows_p, d), const),
            pl.BlockSpec((rows_p, d), const),
            pl.BlockSpec((rows_p, d), const),
            pl.BlockSpec((rows_p, LANES), const),
        ],
        out_specs=(
            pl.BlockSpec((tm, d // 2), lambda i: (i, 0)),
            pl.BlockSpec((TOP_K, tm), lambda i: (0, i)),
            pl.BlockSpec((TOP_K, tm), lambda i: (0, i)),
            pl.BlockSpec((TOP_K, tm), lambda i: (0, i)),
            pl.BlockSpec((n_exp, LANES), const),
        ),
        scratch_shapes=[pltpu.VMEM((tm, tm), BF16), pltpu.VMEM((n_exp, LANES), F32)],
        compiler_params=_cparams(("arbitrary",), 40),
        name="moe_router",
    )(x2, gain.reshape(1, d), w1, w2, w3, bias)


def _dest_kernel(eid_ref, rank_ref, start_ref, dest_ref):
    eid = eid_ref[...]
    n_exp = start_ref.shape[0]
    dest = rank_ref[...]
    for k in range(eid.shape[0]):
        hot = lax.broadcasted_iota(I32, (n_exp, eid.shape[1]), 0) == eid[k:k + 1, :]
        off = jnp.sum(jnp.where(hot, start_ref[:, 0:1], 0), axis=0, keepdims=True)
        dest_ref[k:k + 1, :] = dest[k:k + 1, :] + off


def _dest_rows(eid, rank, seg_start):
    k, n = eid.shape
    n_exp = seg_start.shape[0]
    tm = _tile(n, 2048)
    start = jnp.broadcast_to(seg_start[:, None], (n_exp, LANES)).astype(I32)
    return pl.pallas_call(
        _dest_kernel,
        out_shape=jax.ShapeDtypeStruct((k, n), I32),
        grid=(n // tm,),
        in_specs=[
            pl.BlockSpec((k, tm), lambda i: (0, i)),
            pl.BlockSpec((k, tm), lambda i: (0, i)),
            pl.BlockSpec((n_exp, LANES), lambda i: (0, 0)),
        ],
        out_specs=pl.BlockSpec((k, tm), lambda i: (0, i)),
        compiler_params=_cparams(("parallel",), 16),
        name="moe_dest",
    )(eid, rank, start)


def _dispatch_kernel(nv_ref, dest_ref, hn_ref, xs_ref, zero_ref, sem, zsem, *, chunk_rows):
    k, tm = dest_ref.shape[1], dest_ref.shape[2]
    n_chunks = nv_ref.shape[0]

    @pl.when(pl.program_id(0) == 0)
    def _():
        zero_ref[...] = jnp.zeros_like(zero_ref)

        def zero_copy(c):
            rows = pl.ds(pl.multiple_of(c * chunk_rows, chunk_rows), chunk_rows)
            return pltpu.make_async_copy(zero_ref, xs_ref.at[rows], zsem)

        def start(c, carry):
            @pl.when(nv_ref[c] < chunk_rows)
            def _():
                zero_copy(c).start()
            return carry

        def finish(c, carry):
            @pl.when(nv_ref[c] < chunk_rows)
            def _():
                zero_copy(c).wait()
            return carry

        lax.fori_loop(0, n_chunks, start, 0)
        lax.fori_loop(0, n_chunks, finish, 0)


    def issue(t, c):
        for kk in range(k):
            pltpu.make_async_copy(hn_ref.at[pl.ds(t, 1)],
                                  xs_ref.at[pl.ds(dest_ref[0, kk, t], 1)], sem).start()
        return c

    lax.fori_loop(0, tm, issue, 0, unroll=8)
    for kk in range(k):
        pltpu.make_async_copy(hn_ref, xs_ref.at[pl.ds(0, tm)], sem).wait()


def _dispatch(dest, hn, n_valid, n_rows, chunk_rows):
    k, n = dest.shape
    d = hn.shape[1]
    tm = _tile(n, 512)
    dest3 = dest.reshape(k, n // tm, tm).transpose(1, 0, 2)
    grid_spec = pltpu.PrefetchScalarGridSpec(
        num_scalar_prefetch=1,
        grid=(n // tm,),
        in_specs=[
            pl.BlockSpec((1, k, tm), lambda i, nv: (i, 0, 0), memory_space=pltpu.SMEM),
            pl.BlockSpec((tm, d), lambda i, nv: (i, 0)),
        ],
        out_specs=pl.BlockSpec(memory_space=pl.ANY),
        scratch_shapes=[pltpu.VMEM((chunk_rows, d), hn.dtype), pltpu.SemaphoreType.DMA(()),
                        pltpu.SemaphoreType.DMA(())],
    )
    return pl.pallas_call(
        functools.partial(_dispatch_kernel, chunk_rows=chunk_rows),
        out_shape=jax.ShapeDtypeStruct((n_rows, d), hn.dtype),
        grid_spec=grid_spec,
        compiler_params=_cparams(("arbitrary",), 24),
        name="moe_dispatch",
    )(n_valid, dest3, hn)


def _expert_kernel(ce_ref, nu_ref, fresh_ref, xs_ref, wi_ref, wo_ref, ys_ref,
                   wib_ref, wob_ref, *, d_expert):
    i = pl.program_id(0)

    @pl.when(i < nu_ref[0])
    def _():
        @pl.when(fresh_ref[i] == 1)
        def _():
            wib_ref[...] = wi_ref[...].astype(BF16)
            wob_ref[...] = wo_ref[...].astype(BF16)

        x_lo, x_hi = _unpack_bf16_pair(xs_ref[...])
        xb = jnp.concatenate([x_lo, x_hi], axis=1).astype(BF16)
        gu = jnp.dot(xb, wib_ref[...], preferred_element_type=F32)
        gate = gu[:, :d_expert]
        act = (gate * jax.nn.sigmoid(gate) * gu[:, d_expert:]).astype(BF16)
        ys_ref[...] = _pack_bf16_pair(jnp.dot(act, wob_ref[...], preferred_element_type=F32))

    @pl.when(i >= nu_ref[0])
    def _():
        ys_ref[...] = jnp.zeros_like(ys_ref)


def _experts(xs, chunk_e, n_used, fresh, w_in_all, w_out_all, layer, chunk_rows):
    n_rows, d_packed = xs.shape
    d = 2 * d_packed
    d_expert = w_out_all.shape[2]
    n_chunks = n_rows // chunk_rows
    xs_map = lambda i, ce, nu, fr: (jnp.minimum(i, jnp.maximum(nu[0] - 1, 0)), 0)
    w_map = lambda i, ce, nu, fr: (layer, ce[i], 0, 0)
    grid_spec = pltpu.PrefetchScalarGridSpec(
        num_scalar_prefetch=3,
        grid=(n_chunks,),
        in_specs=[
            pl.BlockSpec((chunk_rows, d_packed), xs_map),
            pl.BlockSpec((None, None, d, 2 * d_expert), w_map),
            pl.BlockSpec((None, None, d_expert, d), w_map),
        ],
        out_specs=pl.BlockSpec((chunk_rows, d_packed), lambda i, ce, nu, fr: (i, 0)),
        scratch_shapes=[pltpu.VMEM((d, 2 * d_expert), BF16), pltpu.VMEM((d_expert, d), BF16)],
    )
    return pl.pallas_call(
        functools.partial(_expert_kernel, d_expert=d_expert),
        out_shape=jax.ShapeDtypeStruct((n_rows, d_packed), I32),
        grid_spec=grid_spec,
        compiler_params=_cparams(("arbitrary",), 56),
        name="moe_experts",
    )(chunk_e, n_used, fresh, xs, w_in_all, w_out_all)


def _combine_kernel(dest_ref, next_ref, x_ref, wt_ref, g_ref, ys_ref, o_ref, buf_ref, sem_ref, *,
                    final_norm, n_blocks):
    k, tm = dest_ref.shape[1], dest_ref.shape[2]
    i = pl.program_id(0)

    def issue_block(d_ref, slot):
        def body(t, c):
            for kk in range(k):
                pltpu.make_async_copy(ys_ref.at[pl.ds(d_ref[0, kk, t], 1)],
                                      buf_ref.at[slot, kk, pl.ds(t, 1)], sem_ref.at[slot]).start()
            return c
        lax.fori_loop(0, tm, body, 0, unroll=8)

    @pl.when(i == 0)
    def _():
        issue_block(dest_ref, 0)

    @pl.when(i + 1 < n_blocks)
    def _():
        issue_block(next_ref, (i + 1) % 2)

    slot = i % 2
    for kk in range(k):
        pltpu.make_async_copy(ys_ref.at[pl.ds(0, tm)], buf_ref.at[slot, kk], sem_ref.at[slot]).wait()

    c = buf_ref.shape[3]
    out_lo = x_ref[:, :c]
    out_hi = x_ref[:, c:]
    for kk in range(k):
        y_lo, y_hi = _unpack_bf16_pair(buf_ref[slot, kk])
        w = wt_ref[:, kk:kk + 1]
        out_lo = out_lo + w * y_lo
        out_hi = out_hi + w * y_hi
    if final_norm:
        ms = (jnp.sum(out_lo * out_lo, axis=-1, keepdims=True)
              + jnp.sum(out_hi * out_hi, axis=-1, keepdims=True)) / (2 * c)
        scale = lax.rsqrt(ms + EPS)
        out_lo = out_lo * scale * g_ref[:, :c]
        out_hi = out_hi * scale * g_ref[:, c:]
    o_ref[:, :c] = out_lo
    o_ref[:, c:] = out_hi


def _combine(dest, x2, wts, ys, gain, final_norm):
    k, n = dest.shape
    d = x2.shape[1]
    tm = _tile(n, 512)
    n_blocks = n // tm
    dest3 = dest.reshape(k, n_blocks, tm).transpose(1, 0, 2)
    return pl.pallas_call(
        functools.partial(_combine_kernel, final_norm=final_norm, n_blocks=n_blocks),
        out_shape=jax.ShapeDtypeStruct((n, d), F32),
        grid=(n_blocks,),
        in_specs=[
            pl.BlockSpec((1, k, tm), lambda i: (i, 0, 0), memory_space=pltpu.SMEM),
            pl.BlockSpec((1, k, tm), lambda i: (jnp.minimum(i + 1, n_blocks - 1), 0, 0),
                         memory_space=pltpu.SMEM),
            pl.BlockSpec((tm, d), lambda i: (i, 0)),
            pl.BlockSpec((tm, k), lambda i: (i, 0)),
            pl.BlockSpec((1, d), lambda i: (0, 0)),
            pl.BlockSpec(memory_space=pl.ANY),
        ],
        out_specs=pl.BlockSpec((tm, d), lambda i: (i, 0)),
        scratch_shapes=[pltpu.VMEM((2, k, tm, d // 2), I32), pltpu.SemaphoreType.DMA((2,))],
        compiler_params=_cparams(("arbitrary",), 40),
        name="moe_combine",
    )(dest3, dest3, x2, wts.T, gain.reshape(1, d), ys)


EXPERT_CHUNK_ROWS = 256


def _moe(x2, norm_gain, w_group, b_group, w_router, b_router, w_in_all, w_out_all, layer, out_gain,
         final_norm):
    n, d = x2.shape
    n_exp = w_router.shape[1]
    chunk = EXPERT_CHUNK_ROWS
    hn, eid, wts, rank, counts = _router(x2, norm_gain, w_group, b_group, w_router, b_router)
    counts = counts[:, 0]
    padded = (counts + chunk - 1) // chunk * chunk
    seg_end = jnp.cumsum(padded)
    seg_start = seg_end - padded
    n_rows = (-(-(n * TOP_K) // chunk) + n_exp) * chunk
    n_chunks = n_rows // chunk
    chunk_start = jnp.arange(n_chunks, dtype=I32) * chunk
    n_used = (seg_end[-1:] // chunk).astype(I32)
    live_start = jnp.minimum(chunk_start, jnp.maximum(seg_end[-1] - chunk, 0))
    chunk_e = jnp.sum(seg_end[None, :] <= live_start[:, None], axis=1).astype(I32)
    chunk_e = jnp.minimum(chunk_e, n_exp - 1)
    n_valid = jnp.clip(counts[chunk_e] - (chunk_start - seg_start[chunk_e]), 0, chunk).astype(I32)
    fresh = jnp.concatenate([jnp.ones((1,), I32), (chunk_e[1:] != chunk_e[:-1]).astype(I32)])
    dest = _dest_rows(eid, rank, seg_start)
    xs = _dispatch(dest, hn, n_valid, n_rows, chunk)
    ys = _experts(xs, chunk_e, n_used, fresh, w_in_all, w_out_all, layer, chunk)
    return _combine(dest, x2, wts, ys, out_gain, final_norm)


def _shared_kv_kernel(x_ref, g_ref, wk_ref, wvt_ref, w_ref, b_ref, k_ref, vt_ref, aux_ref, end_ref,
                      tri_ref, carry_ref, *, n_heads):
    ts = x_ref.shape[0]

    @pl.when(pl.program_id(1) == 0)
    def _():
        r = lax.broadcasted_iota(I32, (ts, ts), 0)
        c = lax.broadcasted_iota(I32, (ts, ts), 1)
        tri_ref[...] = (c <= r).astype(BF16)
        carry_ref[...] = jnp.zeros_like(carry_ref)

    hn = _rms(x_ref[...], g_ref[...]).astype(BF16)
    k_ref[...] = jnp.dot(hn, wk_ref[...], preferred_element_type=F32).astype(k_ref.dtype)
    vt_ref[...] = _dot_nt(wvt_ref[...], hn).astype(vt_ref.dtype)
    f = jnp.dot(hn, w_ref[...], preferred_element_type=F32) + b_ref[...]
    logf = jnp.minimum(f, 0.0) - jnp.log(1.0 + jnp.exp(-jnp.abs(f)))
    l1, l2, l3 = _split3(logf)
    tri = tri_ref[...]
    cum = (jnp.dot(tri, l1, preferred_element_type=F32)
           + jnp.dot(tri, l2, preferred_element_type=F32)
           + jnp.dot(tri, l3, preferred_element_type=F32)) + carry_ref[0:1, :]
    carry_ref[...] = jnp.broadcast_to(cum[ts - 1:ts, :], carry_ref.shape)
    end_ref[...] = carry_ref[...] * (-LOG2E)
    c1, c2, c3 = [c.astype(F32) for c in _split3(cum * (-LOG2E))]
    lane = lax.broadcasted_iota(I32, (ts, LANES), 1)
    for h in range(n_heads):
        col = lambda v: jnp.broadcast_to(v[:, h:h + 1], (ts, LANES))
        aux = jnp.where(lane == 0, col(c1), jnp.where(lane == 1, col(c2),
                        jnp.where(lane == 2, col(c3), 0.0)))
        aux_ref[h] = aux.astype(BF16)


def _shared_kv(x3, gain, w_k, w_vt, w_f, b_f):
    b, s, d = x3.shape
    hd_all = w_k.shape[1]
    n_heads = w_f.shape[1]
    assert n_heads <= LANES
    w_p = jnp.pad(w_f, ((0, 0), (0, LANES - n_heads))).astype(BF16)
    b_p = jnp.pad(b_f, (0, LANES - n_heads)).reshape(1, LANES)
    ts = _tile(s, ATTN_BLOCK)
    const = lambda bi, i: (0, 0)
    return pl.pallas_call(
        functools.partial(_shared_kv_kernel, n_heads=n_heads),
        out_shape=(jax.ShapeDtypeStruct((b, s, hd_all), BF16),
                   jax.ShapeDtypeStruct((b, hd_all, s), BF16),
                   jax.ShapeDtypeStruct((b, n_heads, s, LANES), BF16),
                   jax.ShapeDtypeStruct((b, s // ts, SUBLANES, LANES), F32)),
        grid=(b, s // ts),
        in_specs=[
            pl.BlockSpec((None, ts, d), lambda bi, i: (bi, i, 0)),
            pl.BlockSpec((1, d), const),
            pl.BlockSpec((d, hd_all), const, pipeline_mode=pl.Buffered(1)),
            pl.BlockSpec((hd_all, d), const, pipeline_mode=pl.Buffered(1)),
            pl.BlockSpec((d, LANES), const),
            pl.BlockSpec((1, LANES), const),
        ],
        out_specs=(pl.BlockSpec((None, ts, hd_all), lambda bi, i: (bi, i, 0)),
                   pl.BlockSpec((None, hd_all, ts), lambda bi, i: (bi, 0, i)),
                   pl.BlockSpec((None, n_heads, ts, LANES), lambda bi, i: (bi, 0, i, 0)),
                   pl.BlockSpec((None, None, SUBLANES, LANES), lambda bi, i: (bi, i, 0, 0))),
        scratch_shapes=[pltpu.VMEM((ts, ts), BF16), pltpu.VMEM((SUBLANES, LANES), F32)],
        compiler_params=_cparams(("parallel", "arbitrary"), 56),
        name="shared_kv",
    )(x3, gain.reshape(1, d), w_k, w_vt, w_p, b_p)


ATTN_BLOCK = 512
ATTN_GROUP_BLOCKS = 4
FINITE_LIMIT = 3.0e38
SKIP_MARGIN = 160.0


def _attn_kernel(end_ref, q_ref, k_ref, aux_ref, vt_ref, o_ref, qp_ref, m_ref, l_ref, acc_ref,
                 kmax_ref):
    t, hd = q_ref.shape
    i = pl.program_id(2)
    lane = lax.broadcasted_iota(I32, (t, LANES), 1)
    qp_ref[:, 0:hd] = q_ref[...]
    qp_ref[:, hd:hd + LANES] = jnp.where(lane < 3, 1.0, 0.0).astype(BF16)

    def max_row_norm(x):
        xf = x.astype(F32)
        return jnp.sum(xf * xf, axis=1, keepdims=True)

    @pl.when(i == 0)
    def _():
        def body(c, best):
            rows = pl.ds(pl.multiple_of(c * t, t), t)
            return jnp.maximum(best, max_row_norm(k_ref[rows, :]))
        best = lax.fori_loop(0, k_ref.shape[0] // t, body, jnp.zeros((t, 1), F32))
        kmax_ref[0] = jnp.sqrt(jnp.max(best))

    reach = jnp.sqrt(jnp.max(max_row_norm(q_ref[...]))) * kmax_ref[0]

    def alive(block):
        bias_top = end_ref[0, jnp.maximum(block, 0)]
        return jnp.logical_and(block >= 0,
                               reach + bias_top - jnp.min(m_ref[...]) > -SKIP_MARGIN)

    def scores(start, size, masked):
        rows = pl.ds(pl.multiple_of(start, t), size)
        kp = jnp.concatenate([k_ref[rows, :], aux_ref[rows, :]], axis=1)
        s_t = _dot_nt(kp, qp_ref[...])
        if masked:
            kr = lax.broadcasted_iota(I32, (size, t), 0)
            qc = lax.broadcasted_iota(I32, (size, t), 1)
            s_t = jnp.where(kr <= qc, s_t, MASK_VALUE)
        return s_t, rows

    def exact_block(start, masked):
        s_t, rows = scores(start, t, masked)
        m_old = m_ref[...]
        m_new = jnp.maximum(m_old, jnp.max(s_t, axis=0, keepdims=True))
        alpha = jnp.exp2(m_old - m_new)
        p = jnp.exp2(s_t - m_new)
        l_ref[...] = alpha * l_ref[...] + jnp.sum(p, axis=0, keepdims=True)
        pv = jnp.dot(vt_ref[:, rows], p.astype(BF16), preferred_element_type=F32)
        acc_ref[...] = alpha * acc_ref[...] + pv
        m_ref[...] = m_new

    def lagged_group(start, size):
        s_t, rows = scores(start, size, False)
        m_old = m_ref[...]
        p = jnp.exp2(s_t - m_old)
        pv = jnp.dot(vt_ref[:, rows], p.astype(BF16), preferred_element_type=F32)
        m_new = jnp.maximum(m_old, jnp.max(s_t, axis=0, keepdims=True))
        alpha = jnp.exp2(m_old - m_new)
        l_ref[...] = (l_ref[...] + jnp.sum(p, axis=0, keepdims=True)) * alpha
        acc_ref[...] = (acc_ref[...] + pv) * alpha
        m_ref[...] = m_new

    def reset():
        m_ref[...] = jnp.full_like(m_ref, MASK_VALUE)
        l_ref[...] = jnp.zeros_like(l_ref)
        acc_ref[...] = jnp.zeros_like(acc_ref)

    def finish():
        out = acc_ref[...] / l_ref[...]
        o_ref[...] = out.T.astype(o_ref.dtype)
        return out

    reset()
    exact_block(i * t, True)
    gb = ATTN_GROUP_BLOCKS
    n_groups = i // gb
    rest = i - n_groups * gb

    def group_cond(c):
        g, live = c
        return jnp.logical_and(g < n_groups, live)

    def group_body(c):
        g, _ = c
        first = i - (g + 1) * gb
        lagged_group(first * t, gb * t)
        return g + 1, alive(first - 1)

    _, live = lax.while_loop(group_cond, group_body, (0, alive(i - 1)))

    @pl.when(jnp.logical_and((rest & 2) != 0, live))
    def _():
        lagged_group((rest - 2) * t, 2 * t)

    @pl.when(jnp.logical_and((rest & 1) != 0, jnp.logical_and(live, alive(0))))
    def _():
        lagged_group(0, t)

    out = finish()
    overflowed = jnp.max(jnp.where(jnp.abs(out) < FINITE_LIMIT, 0.0, 1.0)) > 0.0

    @pl.when(overflowed)
    def _():
        reset()

        def exact_body(j, c):
            exact_block(j * t, False)
            return c

        lax.fori_loop(0, i, exact_body, 0)
        exact_block(i * t, True)
        finish()


def _attention(qg3, k3, aux, vt, block_end, n_heads):
    b, s, _ = k3.shape
    hd = k3.shape[2] // n_heads
    assert hd == LANES
    t = _tile(s, ATTN_BLOCK)
    end_tab = block_end[:, :, 0, :n_heads].transpose(0, 2, 1).reshape(b, n_heads, 1, s // t)
    return pl.pallas_call(
        _attn_kernel,
        out_shape=jax.ShapeDtypeStruct((b, s, n_heads * hd), BF16),
        grid=(b, n_heads, s // t),
        in_specs=[
            pl.BlockSpec((None, None, 1, s // t), lambda bi, h, i: (bi, h, 0, 0),
                         memory_space=pltpu.SMEM),
            pl.BlockSpec((None, t, hd), lambda bi, h, i: (bi, i, h)),
            pl.BlockSpec((None, s, hd), lambda bi, h, i: (bi, 0, h)),
            pl.BlockSpec((None, None, s, LANES), lambda bi, h, i: (bi, h, 0, 0)),
            pl.BlockSpec((None, hd, s), lambda bi, h, i: (bi, h, 0)),
        ],
        out_specs=pl.BlockSpec((None, t, hd), lambda bi, h, i: (bi, i, h)),
        scratch_shapes=[
            pltpu.VMEM((t, hd + LANES), BF16),
            pltpu.VMEM((1, t), F32),
            pltpu.VMEM((1, t), F32),
            pltpu.VMEM((hd, t), F32),
            pltpu.SMEM((1,), F32),
        ],
        compiler_params=_cparams(("parallel", "parallel", "arbitrary"), 48),
        name="fox_attention",
    )(end_tab, qg3, k3, aux, vt)


def _gated_out_kernel(o_ref, gate_ref, x_ref, w_ref, out_ref):
    g = o_ref[...].astype(F32) * jax.nn.sigmoid(gate_ref[...].astype(F32))
    out_ref[...] = x_ref[...] + jnp.dot(g.astype(BF16), w_ref[...], preferred_element_type=F32)


def _gated_out(o2, qg, x2, w_o):
    n, d = x2.shape
    hd_all = o2.shape[1]
    tm = _tile(n, 512)
    return pl.pallas_call(
        _gated_out_kernel,
        out_shape=jax.ShapeDtypeStruct((n, d), F32),
        grid=(n // tm,),
        in_specs=[
            pl.BlockSpec((tm, hd_all), lambda i: (i, 0)),
            pl.BlockSpec((tm, hd_all), lambda i: (i, 1)),
            pl.BlockSpec((tm, d), lambda i: (i, 0)),
            pl.BlockSpec((hd_all, d), lambda i: (0, 0)),
        ],
        out_specs=pl.BlockSpec((tm, d), lambda i: (i, 0)),
        compiler_params=_cparams(("parallel",), 48),
        name="gated_out_proj",
    )(o2, qg, x2, w_o)


def kernel(x, a_norm, a_w_in, a_conv_w, a_conv_b, a_w_rec, a_b_rec, a_w_inp, a_b_inp, a_lambda, a_w_out, kv_norm, kv_w, kv_b_forget, b_norm, b_w_qg, b_w_o, m_norm, m_w_group, m_b_group, m_w_router, m_b_router, m_w_in, m_w_out, final_norm):
    b, s, d = x.shape
    n = b * s
    depth = m_norm.shape[0]
    n_a = a_norm.shape[0]
    n_heads = kv_b_forget.shape[0]
    hd_all = b_w_o.shape[1]
    head_dim = hd_all // n_heads
    x2 = x.reshape(n, d)
    k3 = aux = vt = block_end = None
    for layer in range(depth):
        if layer < n_a:
            i = layer
            d_rnn = a_w_out.shape[1]
            proj = _norm_proj(x2, a_norm[i], a_w_in[i].astype(BF16), jnp.ones((2 * d_rnn,), F32),
                              BF16, "rglru_in_proj")
            x2 = _rglru(proj, x2.reshape(b, s, d), a_conv_w[i], a_conv_b[i], a_w_rec[i], a_b_rec[i],
                        a_w_inp[i], a_b_inp[i], a_lambda[i], a_w_out[i]).reshape(n, d)
        else:
            j = layer - n_a
            q_scale = jnp.concatenate([jnp.full((hd_all,), head_dim ** -0.5 * LOG2E, F32),
                                       jnp.ones((hd_all,), F32)])
            qg = _norm_proj(x2, b_norm[j], b_w_qg[j].astype(BF16), q_scale, BF16, "fox_qg_proj")
            o = _attention(qg.reshape(b, s, 2 * hd_all), k3, aux, vt, block_end, n_heads)
            x2 = _gated_out(o.reshape(n, hd_all), qg, x2, b_w_o[j].astype(BF16))
        last = layer == depth - 1
        x2 = _moe(x2, m_norm[layer], m_w_group[layer], m_b_group[layer], m_w_router[layer],
                  m_b_router[layer], m_w_in, m_w_out, layer, final_norm, last)
        if layer == n_a - 1:
            k3, vt, aux, block_end = _shared_kv(
                x2.reshape(b, s, d), kv_norm, kv_w[:, :hd_all].astype(BF16),
                kv_w[:, hd_all:2 * hd_all].T.astype(BF16), kv_w[:, 2 * hd_all:], kv_b_forget)
    if depth == 0:
        x2 = _rms(x2, final_norm)
    return x2.reshape(b, s, d)
```

```python
import functools
import math

import jax
import jax.numpy as jnp
from jax import lax
from jax.experimental import pallas as pl
from jax.experimental.pallas import tpu as pltpu

F32 = jnp.float32
BF16 = jnp.bfloat16
I32 = jnp.int32

EPS = 1e-6
LRU_C = 8.0
TOP_K = 2
LOG2E = 1.4426950408889634
MASK_VALUE = -1e30

V7X_VMEM_BYTES = 64 * 1024 * 1024
SUBLANES = 8
LANES = 128
MIB = 1024 * 1024


def _cparams(semantics, vmem_mib):
    assert vmem_mib * MIB < V7X_VMEM_BYTES
    return pltpu.CompilerParams(dimension_semantics=semantics, vmem_limit_bytes=vmem_mib * MIB)


def _tile(dim, pref):
    t = min(dim, pref)
    assert dim % t == 0, (dim, pref)
    return t


def _rms(x, gain):
    return x * lax.rsqrt(jnp.mean(x * x, axis=-1, keepdims=True) + EPS) * gain


def _sigmoid(x):
    return 0.5 * jnp.tanh(0.5 * x) + 0.5


def _norm_proj_kernel(x_ref, g_ref, w_ref, s_ref, o_ref, hn_ref):
    @pl.when(pl.program_id(1) == 0)
    def _():
        hn_ref[...] = _rms(x_ref[...], g_ref[...]).astype(BF16)

    acc = jnp.dot(hn_ref[...], w_ref[...], preferred_element_type=F32)
    o_ref[...] = (acc * s_ref[...]).astype(o_ref.dtype)


def _norm_proj(x, gain, w, col_scale, out_dtype, name):
    n, d = x.shape
    n_out = w.shape[1]
    tm = _tile(n, 1024)
    tn = _tile(n_out, 1024)
    return pl.pallas_call(
        _norm_proj_kernel,
        out_shape=jax.ShapeDtypeStruct((n, n_out), out_dtype),
        grid=(n // tm, n_out // tn),
        in_specs=[
            pl.BlockSpec((tm, d), lambda i, j: (i, 0)),
            pl.BlockSpec((1, d), lambda i, j: (0, 0)),
            pl.BlockSpec((d, tn), lambda i, j: (0, j)),
            pl.BlockSpec((1, tn), lambda i, j: (0, j)),
        ],
        out_specs=pl.BlockSpec((tm, tn), lambda i, j: (i, j)),
        scratch_shapes=[pltpu.VMEM((tm, d), BF16)],
        compiler_params=_cparams(("parallel", "arbitrary"), 48),
        name=name,
    )(x, gain.reshape(1, d), w, col_scale.reshape(1, n_out))


SCAN_STRIP = 1024


def _rglru_kernel(u_ref, y_ref, x_ref, cw_ref, cb_ref, wr_ref, br_ref, wi_ref, bi_ref,
                  lam_ref, wo_ref, o_ref, ubuf_ref, a_ref, b_ref, h_ref, *, conv_width):
    ts, d = u_ref.shape
    nb, blk, _ = wr_ref.shape
    halo = SUBLANES

    @pl.when(pl.program_id(1) == 0)
    def _():
        ubuf_ref[0:halo, :] = jnp.zeros((halo, d), F32)
        h_ref[...] = jnp.zeros_like(h_ref)

    u = u_ref[...].astype(F32)
    ubuf_ref[halo:halo + ts, :] = u
    uc = u * cw_ref[conv_width - 1:conv_width, :] + cb_ref[...]
    for j in range(conv_width - 1):
        shift = conv_width - 1 - j
        uc = uc + ubuf_ref[halo - shift:halo - shift + ts, :] * cw_ref[j:j + 1, :]
    ubuf_ref[0:halo, :] = u[ts - halo:, :]

    ub = uc.astype(BF16)
    r_parts, i_parts = [], []
    for n in range(nb):
        ubn = ub[:, n * blk:(n + 1) * blk]
        r_parts.append(jnp.dot(ubn, wr_ref[n], preferred_element_type=F32))
        i_parts.append(jnp.dot(ubn, wi_ref[n], preferred_element_type=F32))
    r = _sigmoid(jnp.concatenate(r_parts, axis=1) + br_ref[...])
    gi = _sigmoid(jnp.concatenate(i_parts, axis=1) + bi_ref[...])
    lam = lam_ref[...]
    sp = jnp.maximum(-lam, 0.0) + jnp.log(1.0 + jnp.exp(-jnp.abs(lam)))
    log_a = (-LRU_C * r) * sp
    th = jnp.tanh(0.5 * log_a)
    inv = 1.0 / (1.0 - th)
    a_ref[...] = (1.0 + th) * inv
    b_ref[...] = (2.0 * inv) * jnp.sqrt(-th) * (gi * uc)

    row = lax.broadcasted_iota(I32, (SUBLANES, SCAN_STRIP), 0)
    strip = min(SCAN_STRIP, d)
    for c in range(d // strip):
        cols = pl.ds(c * strip, strip)

        def tile_step(t, h):
            rows = pl.ds(pl.multiple_of(t * SUBLANES, SUBLANES), SUBLANES)
            a = a_ref[rows, cols]
            bb = b_ref[rows, cols]
            for sh in (1, 2, 4):
                keep = row[:, :strip] >= sh
                a_prev = jnp.where(keep, pltpu.roll(a, sh, 0), 1.0)
                b_prev = jnp.where(keep, pltpu.roll(bb, sh, 0), 0.0)
                bb = a * b_prev + bb
                a = a * a_prev
            hs = a * h + bb
            b_ref[rows, cols] = hs
            return jnp.broadcast_to(hs[SUBLANES - 1:SUBLANES, :], (SUBLANES, strip))

        h_ref[:, cols] = lax.fori_loop(0, ts // SUBLANES, tile_step, h_ref[:, cols], unroll=2)

    y = y_ref[...].astype(F32)
    gelu = 0.5 * y * (1.0 + jnp.tanh(math.sqrt(2.0 / math.pi) * (y + 0.044715 * (y * y * y))))
    g = (b_ref[...] * gelu).astype(BF16)
    o_ref[...] = x_ref[...] + jnp.dot(g, wo_ref[...], preferred_element_type=F32)


def _rglru(proj, x3, conv_w, conv_b, w_rec, b_rec, w_inp, b_inp, lam, w_out):
    b, s, d = x3.shape
    d_rnn = w_out.shape[0]
    nb, blk, _ = w_rec.shape
    width = conv_w.shape[0]
    assert width - 1 <= SUBLANES
    ts = _tile(s, 256)
    proj3 = proj.reshape(b, s, 2 * d_rnn)
    row = lambda v: v.reshape(1, -1)
    const2 = lambda bi, i: (0, 0)
    const3 = lambda bi, i: (0, 0, 0)
    return pl.pallas_call(
        functools.partial(_rglru_kernel, conv_width=width),
        out_shape=jax.ShapeDtypeStruct((b, s, d), F32),
        grid=(b, s // ts),
        in_specs=[
            pl.BlockSpec((None, ts, d_rnn), lambda bi, i: (bi, i, 0)),
            pl.BlockSpec((None, ts, d_rnn), lambda bi, i: (bi, i, 1)),
            pl.BlockSpec((None, ts, d), lambda bi, i: (bi, i, 0)),
            pl.BlockSpec((width, d_rnn), const2),
            pl.BlockSpec((1, d_rnn), const2),
            pl.BlockSpec((nb, blk, blk), const3),
            pl.BlockSpec((1, d_rnn), const2),
            pl.BlockSpec((nb, blk, blk), const3),
            pl.BlockSpec((1, d_rnn), const2),
            pl.BlockSpec((1, d_rnn), const2),
            pl.BlockSpec((d_rnn, d), const2),
        ],
        out_specs=pl.BlockSpec((None, ts, d), lambda bi, i: (bi, i, 0)),
        scratch_shapes=[
            pltpu.VMEM((SUBLANES + ts, d_rnn), F32),
            pltpu.VMEM((ts, d_rnn), F32),
            pltpu.VMEM((ts, d_rnn), F32),
            pltpu.VMEM((SUBLANES, d_rnn), F32),
        ],
        compiler_params=_cparams(("parallel", "arbitrary"), 56),
        name="rglru",
    )(proj3, proj3, x3, conv_w, row(conv_b), w_rec.astype(BF16), row(b_rec),
      w_inp.astype(BF16), row(b_inp), row(lam), w_out.astype(BF16))


def _pack_bf16_pair(x):
    c = x.shape[1] // 2
    lo = lax.bitcast_convert_type(x[:, :c].astype(BF16).astype(F32), I32)
    hi = lax.bitcast_convert_type(x[:, c:].astype(BF16).astype(F32), I32)
    return jnp.bitwise_or(hi, lax.shift_right_logical(lo, jnp.int32(16)))


def _unpack_bf16_pair(w):
    lo = lax.bitcast_convert_type(lax.shift_left(w, jnp.int32(16)), F32)
    hi = lax.bitcast_convert_type(jnp.bitwise_and(w, jnp.int32(-65536)), F32)
    return lo, hi


def _split3(v):
    v1 = v.astype(BF16)
    r1 = v - v1.astype(F32)
    v2 = r1.astype(BF16)
    v3 = (r1 - v2.astype(F32)).astype(BF16)
    return v1, v2, v3


def _dot_nt(a, b):
    return lax.dot_general(a, b, (((1,), (1,)), ((), ())), preferred_element_type=F32)


def _first_argmax(v, n):
    idx = lax.broadcasted_iota(I32, v.shape, 0)
    vmax = jnp.max(v, axis=0, keepdims=True)
    amax = jnp.min(jnp.where(v == vmax, idx, n), axis=0, keepdims=True)
    return amax, vmax


def _router_kernel(x_ref, g_ref, w1_ref, w2_ref, bias_ref, hn_ref, eid_ref, wt_ref,
                   rank_ref, cnt_ref, tri_ref, carry_ref, *, n_groups, per_group):
    tm = x_ref.shape[0]
    n_exp = n_groups * per_group

    @pl.when(pl.program_id(0) == 0)
    def _():
        r = lax.broadcasted_iota(I32, (tm, tm), 0)
        c = lax.broadcasted_iota(I32, (tm, tm), 1)
        tri_ref[...] = (r < c).astype(BF16)
        carry_ref[...] = jnp.zeros_like(carry_ref)

    hn = _rms(x_ref[...], g_ref[...])
    hn_ref[...] = _pack_bf16_pair(hn)
    h1, h2, _ = _split3(hn)
    w1, w2 = w1_ref[...], w2_ref[...]
    logits = _dot_nt(w1, h1) + (_dot_nt(w1, h2) + _dot_nt(w2, h1))
    logits = logits + bias_ref[:, 0:1]

    gl = logits[0:n_groups, :]
    g_idx, g_max = _first_argmax(gl, n_groups)
    gp_top = 1.0 / jnp.sum(jnp.exp(gl - g_max), axis=0, keepdims=True)
    el = jnp.zeros((per_group, tm), F32)
    for g in range(n_groups):
        lo = n_groups + g * per_group
        el = jnp.where(g_idx == g, logits[lo:lo + per_group, :], el)
    e_max = jnp.max(el, axis=0, keepdims=True)
    ex = jnp.exp(el - e_max)
    ep = ex / jnp.sum(ex, axis=0, keepdims=True)
    e1, p1 = _first_argmax(ep, per_group)
    sub = lax.broadcasted_iota(I32, ep.shape, 0)
    e2, p2 = _first_argmax(jnp.where(sub == e1, -1.0, ep), per_group)
    denom = p1 + p2
    eid1 = g_idx * per_group + e1
    eid2 = g_idx * per_group + e2
    eid_ref[...] = jnp.concatenate([eid1, eid2], axis=0)
    wt_ref[...] = jnp.concatenate([gp_top * p1 / denom, gp_top * p2 / denom], axis=0)

    e_iota = lax.broadcasted_iota(I32, (n_exp, tm), 0)
    hot1 = e_iota == eid1
    hot2 = e_iota == eid2
    chosen = jnp.logical_or(hot1, hot2)
    before = jnp.dot(chosen.astype(BF16), tri_ref[...], preferred_element_type=F32)
    base = (before + carry_ref[:, 0:1]).astype(I32)
    rank1 = jnp.sum(jnp.where(hot1, base, 0), axis=0, keepdims=True)
    rank2 = jnp.sum(jnp.where(hot2, base, 0), axis=0, keepdims=True)
    rank_ref[...] = jnp.concatenate([rank1, rank2], axis=0)
    carry_ref[...] = carry_ref[...] + jnp.sum(chosen.astype(F32), axis=1, keepdims=True)
    cnt_ref[...] = carry_ref[...].astype(I32)


def _router(x2, gain, w_group, b_group, w_router, b_router):
    n, d = x2.shape
    n_groups = w_group.shape[1]
    n_exp = w_router.shape[1]
    per_group = n_exp // n_groups
    assert per_group == SUBLANES and n_groups == SUBLANES
    rows = n_groups + n_exp
    rows_p = -(-rows // LANES) * LANES
    w_t = jnp.concatenate([w_group, w_router], axis=1).T
    w_t = jnp.pad(w_t, ((0, rows_p - rows), (0, 0)))
    w1, w2, _ = _split3(w_t)
    bias = jnp.pad(jnp.concatenate([b_group, b_router]), (0, rows_p - rows))
    bias = jnp.broadcast_to(bias[:, None], (rows_p, LANES))
    tm = _tile(n, 512)
    const = lambda i: (0, 0)
    return pl.pallas_call(
        functools.partial(_router_kernel, n_groups=n_groups, per_group=per_group),
        out_shape=(
            jax.ShapeDtypeStruct((n, d // 2), I32),
            jax.ShapeDtypeStruct((TOP_K, n), I32),
            jax.ShapeDtypeStruct((TOP_K, n), F32),
            jax.ShapeDtypeStruct((TOP_K, n), I32),
            jax.ShapeDtypeStruct((n_exp, LANES), I32),
        ),
        grid=(n // tm,),
        in_specs=[
            pl.BlockSpec((tm, d), lambda i: (i, 0)),
            pl.BlockSpec((1, d), const),
            pl.BlockSpec((rows_p, d), const),
            pl.BlockSpec((rows_p, d), const),
            pl.BlockSpec((rows_p, LANES), const),
        ],
        out_specs=(
            pl.BlockSpec((tm, d // 2), lambda i: (i, 0)),
            pl.BlockSpec((TOP_K, tm), lambda i: (0, i)),
            pl.BlockSpec((TOP_K, tm), lambda i: (0, i)),
            pl.BlockSpec((TOP_K, tm), lambda i: (0, i)),
            pl.BlockSpec((n_exp, LANES), const),
        ),
        scratch_shapes=[pltpu.VMEM((tm, tm), BF16), pltpu.VMEM((n_exp, LANES), F32)],
        compiler_params=_cparams(("arbitrary",), 40),
        name="moe_router",
    )(x2, gain.reshape(1, d), w1, w2, bias)


def _dest_kernel(eid_ref, rank_ref, start_ref, dest_ref):
    eid = eid_ref[...]
    n_exp = start_ref.shape[0]
    dest = rank_ref[...]
    for k in range(eid.shape[0]):
        hot = lax.broadcasted_iota(I32, (n_exp, eid.shape[1]), 0) == eid[k:k + 1, :]
        off = jnp.sum(jnp.where(hot, start_ref[:, 0:1], 0), axis=0, keepdims=True)
        dest_ref[k:k + 1, :] = dest[k:k + 1, :] + off


def _dest_rows(eid, rank, seg_start):
    k, n = eid.shape
    n_exp = seg_start.shape[0]
    tm = _tile(n, 2048)
    start = jnp.broadcast_to(seg_start[:, None], (n_exp, LANES)).astype(I32)
    return pl.pallas_call(
        _dest_kernel,
        out_shape=jax.ShapeDtypeStruct((k, n), I32),
        grid=(n // tm,),
        in_specs=[
            pl.BlockSpec((k, tm), lambda i: (0, i)),
            pl.BlockSpec((k, tm), lambda i: (0, i)),
            pl.BlockSpec((n_exp, LANES), lambda i: (0, 0)),
        ],
        out_specs=pl.BlockSpec((k, tm), lambda i: (0, i)),
        compiler_params=_cparams(("parallel",), 16),
        name="moe_dest",
    )(eid, rank, start)


def _dispatch_kernel(nv_ref, dest_ref, hn_ref, xs_ref, zero_ref, sem, zsem, *, chunk_rows):
    k, tm = dest_ref.shape[1], dest_ref.shape[2]
    n_chunks = nv_ref.shape[0]

    @pl.when(pl.program_id(0) == 0)
    def _():
        zero_ref[...] = jnp.zeros_like(zero_ref)

        def zero_copy(c):
            rows = pl.ds(pl.multiple_of(c * chunk_rows, chunk_rows), chunk_rows)
            return pltpu.make_async_copy(zero_ref, xs_ref.at[rows], zsem)

        def start(c, carry):
            @pl.when(nv_ref[c] < chunk_rows)
            def _():
                zero_copy(c).start()
            return carry

        def finish(c, carry):
            @pl.when(nv_ref[c] < chunk_rows)
            def _():
                zero_copy(c).wait()
            return carry

        lax.fori_loop(0, n_chunks, start, 0)
        lax.fori_loop(0, n_chunks, finish, 0)


    def issue(t, c):
        for kk in range(k):
            pltpu.make_async_copy(hn_ref.at[pl.ds(t, 1)],
                                  xs_ref.at[pl.ds(dest_ref[0, kk, t], 1)], sem).start()
        return c

    lax.fori_loop(0, tm, issue, 0, unroll=8)
    for kk in range(k):
        pltpu.make_async_copy(hn_ref, xs_ref.at[pl.ds(0, tm)], sem).wait()


def _dispatch(dest, hn, n_valid, n_rows, chunk_rows):
    k, n = dest.shape
    d = hn.shape[1]
    tm = _tile(n, 512)
    dest3 = dest.reshape(k, n // tm, tm).transpose(1, 0, 2)
    grid_spec = pltpu.PrefetchScalarGridSpec(
        num_scalar_prefetch=1,
        grid=(n // tm,),
        in_specs=[
            pl.BlockSpec((1, k, tm), lambda i, nv: (i, 0, 0), memory_space=pltpu.SMEM),
            pl.BlockSpec((tm, d), lambda i, nv: (i, 0)),
        ],
        out_specs=pl.BlockSpec(memory_space=pl.ANY),
        scratch_shapes=[pltpu.VMEM((chunk_rows, d), hn.dtype), pltpu.SemaphoreType.DMA(()),
                        pltpu.SemaphoreType.DMA(())],
    )
    return pl.pallas_call(
        functools.partial(_dispatch_kernel, chunk_rows=chunk_rows),
        out_shape=jax.ShapeDtypeStruct((n_rows, d), hn.dtype),
        grid_spec=grid_spec,
        compiler_params=_cparams(("arbitrary",), 24),
        name="moe_dispatch",
    )(n_valid, dest3, hn)


def _expert_kernel(ce_ref, nu_ref, fresh_ref, xs_ref, wi_ref, wo_ref, ys_ref,
                   wib_ref, wob_ref, *, d_expert):
    i = pl.program_id(0)

    @pl.when(i < nu_ref[0])
    def _():
        @pl.when(fresh_ref[i] == 1)
        def _():
            wib_ref[...] = wi_ref[...].astype(BF16)
            wob_ref[...] = wo_ref[...].astype(BF16)

        x_lo, x_hi = _unpack_bf16_pair(xs_ref[...])
        xb = jnp.concatenate([x_lo, x_hi], axis=1).astype(BF16)
        gu = jnp.dot(xb, wib_ref[...], preferred_element_type=F32)
        gate = gu[:, :d_expert]
        act = (gate * jax.nn.sigmoid(gate) * gu[:, d_expert:]).astype(BF16)
        ys_ref[...] = _pack_bf16_pair(jnp.dot(act, wob_ref[...], preferred_element_type=F32))

    @pl.when(i >= nu_ref[0])
    def _():
        ys_ref[...] = jnp.zeros_like(ys_ref)


def _experts(xs, chunk_e, n_used, fresh, w_in_all, w_out_all, layer, chunk_rows):
    n_rows, d_packed = xs.shape
    d = 2 * d_packed
    d_expert = w_out_all.shape[2]
    n_chunks = n_rows // chunk_rows
    xs_map = lambda i, ce, nu, fr: (jnp.minimum(i, jnp.maximum(nu[0] - 1, 0)), 0)
    w_map = lambda i, ce, nu, fr: (layer, ce[i], 0, 0)
    grid_spec = pltpu.PrefetchScalarGridSpec(
        num_scalar_prefetch=3,
        grid=(n_chunks,),
        in_specs=[
            pl.BlockSpec((chunk_rows, d_packed), xs_map),
            pl.BlockSpec((None, None, d, 2 * d_expert), w_map),
            pl.BlockSpec((None, None, d_expert, d), w_map),
        ],
        out_specs=pl.BlockSpec((chunk_rows, d_packed), lambda i, ce, nu, fr: (i, 0)),
        scratch_shapes=[pltpu.VMEM((d, 2 * d_expert), BF16), pltpu.VMEM((d_expert, d), BF16)],
    )
    return pl.pallas_call(
        functools.partial(_expert_kernel, d_expert=d_expert),
        out_shape=jax.ShapeDtypeStruct((n_rows, d_packed), I32),
        grid_spec=grid_spec,
        compiler_params=_cparams(("arbitrary",), 56),
        name="moe_experts",
    )(chunk_e, n_used, fresh, xs, w_in_all, w_out_all)


def _combine_kernel(dest_ref, next_ref, x_ref, wt_ref, g_ref, ys_ref, o_ref, buf_ref, sem_ref, *,
                    final_norm, n_blocks):
    k, tm = dest_ref.shape[1], dest_ref.shape[2]
    i = pl.program_id(0)

    def issue_block(d_ref, slot):
        def body(t, c):
            for kk in range(k):
                pltpu.make_async_copy(ys_ref.at[pl.ds(d_ref[0, kk, t], 1)],
                                      buf_ref.at[slot, kk, pl.ds(t, 1)], sem_ref.at[slot]).start()
            return c
        lax.fori_loop(0, tm, body, 0, unroll=8)

    @pl.when(i == 0)
    def _():
        issue_block(dest_ref, 0)

    @pl.when(i + 1 < n_blocks)
    def _():
        issue_block(next_ref, (i + 1) % 2)

    slot = i % 2
    for kk in range(k):
        pltpu.make_async_copy(ys_ref.at[pl.ds(0, tm)], buf_ref.at[slot, kk], sem_ref.at[slot]).wait()

    c = buf_ref.shape[3]
    out_lo = x_ref[:, :c]
    out_hi = x_ref[:, c:]
    for kk in range(k):
        y_lo, y_hi = _unpack_bf16_pair(buf_ref[slot, kk])
        w = wt_ref[:, kk:kk + 1]
        out_lo = out_lo + w * y_lo
        out_hi = out_hi + w * y_hi
    if final_norm:
        ms = (jnp.sum(out_lo * out_lo, axis=-1, keepdims=True)
              + jnp.sum(out_hi * out_hi, axis=-1, keepdims=True)) / (2 * c)
        scale = lax.rsqrt(ms + EPS)
        out_lo = out_lo * scale * g_ref[:, :c]
        out_hi = out_hi * scale * g_ref[:, c:]
    o_ref[:, :c] = out_lo
    o_ref[:, c:] = out_hi


def _combine(dest, x2, wts, ys, gain, final_norm):
    k, n = dest.shape
    d = x2.shape[1]
    tm = _tile(n, 512)
    n_blocks = n // tm
    dest3 = dest.reshape(k, n_blocks, tm).transpose(1, 0, 2)
    return pl.pallas_call(
        functools.partial(_combine_kernel, final_norm=final_norm, n_blocks=n_blocks),
        out_shape=jax.ShapeDtypeStruct((n, d), F32),
        grid=(n_blocks,),
        in_specs=[
            pl.BlockSpec((1, k, tm), lambda i: (i, 0, 0), memory_space=pltpu.SMEM),
            pl.BlockSpec((1, k, tm), lambda i: (jnp.minimum(i + 1, n_blocks - 1), 0, 0),
                         memory_space=pltpu.SMEM),
            pl.BlockSpec((tm, d), lambda i: (i, 0)),
            pl.BlockSpec((tm, k), lambda i: (i, 0)),
            pl.BlockSpec((1, d), lambda i: (0, 0)),
            pl.BlockSpec(memory_space=pl.ANY),
        ],
        out_specs=pl.BlockSpec((tm, d), lambda i: (i, 0)),
        scratch_shapes=[pltpu.VMEM((2, k, tm, d // 2), I32), pltpu.SemaphoreType.DMA((2,))],
        compiler_params=_cparams(("arbitrary",), 40),
        name="moe_combine",
    )(dest3, dest3, x2, wts.T, gain.reshape(1, d), ys)


EXPERT_CHUNK_ROWS = 256


def _moe(x2, norm_gain, w_group, b_group, w_router, b_router, w_in_all, w_out_all, layer, out_gain,
         final_norm):
    n, d = x2.shape
    n_exp = w_router.shape[1]
    chunk = EXPERT_CHUNK_ROWS
    hn, eid, wts, rank, counts = _router(x2, norm_gain, w_group, b_group, w_router, b_router)
    counts = counts[:, 0]
    padded = (counts + chunk - 1) // chunk * chunk
    seg_end = jnp.cumsum(padded)
    seg_start = seg_end - padded
    n_rows = (-(-(n * TOP_K) // chunk) + n_exp) * chunk
    n_chunks = n_rows // chunk
    chunk_start = jnp.arange(n_chunks, dtype=I32) * chunk
    n_used = (seg_end[-1:] // chunk).astype(I32)
    live_start = jnp.minimum(chunk_start, jnp.maximum(seg_end[-1] - chunk, 0))
    chunk_e = jnp.sum(seg_end[None, :] <= live_start[:, None], axis=1).astype(I32)
    chunk_e = jnp.minimum(chunk_e, n_exp - 1)
    mine = chunk_e[:, None] == jnp.arange(n_exp, dtype=I32)[None, :]
    left = jnp.sum(jnp.where(mine, (counts + seg_start)[None, :], 0), axis=1) - chunk_start
    n_valid = jnp.clip(left, 0, chunk).astype(I32)
    fresh = jnp.concatenate([jnp.ones((1,), I32), (chunk_e[1:] != chunk_e[:-1]).astype(I32)])
    dest = _dest_rows(eid, rank, seg_start)
    xs = _dispatch(dest, hn, n_valid, n_rows, chunk)
    ys = _experts(xs, chunk_e, n_used, fresh, w_in_all, w_out_all, layer, chunk)
    return _combine(dest, x2, wts, ys, out_gain, final_norm)


def _shared_kv_kernel(x_ref, g_ref, wk_ref, wvt_ref, w_ref, b_ref, k_ref, vt_ref, aux_ref, end_ref,
                      kn_ref, tri_ref, carry_ref, *, n_heads):
    ts = x_ref.shape[0]

    @pl.when(pl.program_id(1) == 0)
    def _():
        r = lax.broadcasted_iota(I32, (ts, ts), 0)
        c = lax.broadcasted_iota(I32, (ts, ts), 1)
        tri_ref[...] = (c <= r).astype(BF16)
        carry_ref[...] = jnp.zeros_like(carry_ref)

    hn = _rms(x_ref[...], g_ref[...]).astype(BF16)
    kb = jnp.dot(hn, wk_ref[...], preferred_element_type=F32).astype(k_ref.dtype)
    k_ref[...] = kb
    sq = kb.astype(F32) * kb.astype(F32)
    hd = kb.shape[1] // n_heads
    tile_lane = lax.broadcasted_iota(I32, (SUBLANES, LANES), 1)
    norms = jnp.zeros((SUBLANES, LANES), F32)
    for h in range(n_heads):
        row_sq = jnp.sum(sq[:, h * hd:(h + 1) * hd], axis=1, keepdims=True)
        norms = jnp.where(tile_lane == h, jnp.max(row_sq, axis=0, keepdims=True), norms)
    kn_ref[...] = norms
    vt_ref[...] = _dot_nt(wvt_ref[...], hn).astype(vt_ref.dtype)
    f = jnp.dot(hn, w_ref[...], preferred_element_type=F32) + b_ref[...]
    logf = jnp.minimum(f, 0.0) - jnp.log(1.0 + jnp.exp(-jnp.abs(f)))
    l1, l2, l3 = _split3(logf)
    tri = tri_ref[...]
    cum = (jnp.dot(tri, l1, preferred_element_type=F32)
           + jnp.dot(tri, l2, preferred_element_type=F32)
           + jnp.dot(tri, l3, preferred_element_type=F32)) + carry_ref[0:1, :]
    carry_ref[...] = jnp.broadcast_to(cum[ts - 1:ts, :], carry_ref.shape)
    end_ref[...] = carry_ref[...] * (-LOG2E)
    c1, c2, c3 = [c.astype(F32) for c in _split3(cum * (-LOG2E))]
    lane = lax.broadcasted_iota(I32, (ts, LANES), 1)
    for h in range(n_heads):
        col = lambda v: jnp.broadcast_to(v[:, h:h + 1], (ts, LANES))
        aux = jnp.where(lane == 0, col(c1), jnp.where(lane == 1, col(c2),
                        jnp.where(lane == 2, col(c3), 0.0)))
        aux_ref[h] = aux.astype(BF16)


def _shared_kv(x3, gain, w_k, w_vt, w_f, b_f):
    b, s, d = x3.shape
    hd_all = w_k.shape[1]
    n_heads = w_f.shape[1]
    assert n_heads <= LANES
    w_p = jnp.pad(w_f, ((0, 0), (0, LANES - n_heads))).astype(BF16)
    b_p = jnp.pad(b_f, (0, LANES - n_heads)).reshape(1, LANES)
    ts = _tile(s, ATTN_BLOCK)
    const = lambda bi, i: (0, 0)
    return pl.pallas_call(
        functools.partial(_shared_kv_kernel, n_heads=n_heads),
        out_shape=(jax.ShapeDtypeStruct((b, s, hd_all), BF16),
                   jax.ShapeDtypeStruct((b, hd_all, s), BF16),
                   jax.ShapeDtypeStruct((b, n_heads, s, LANES), BF16),
                   jax.ShapeDtypeStruct((b, s // ts, SUBLANES, LANES), F32),
                   jax.ShapeDtypeStruct((b, s // ts, SUBLANES, LANES), F32)),
        grid=(b, s // ts),
        in_specs=[
            pl.BlockSpec((None, ts, d), lambda bi, i: (bi, i, 0)),
            pl.BlockSpec((1, d), const),
            pl.BlockSpec((d, hd_all), const, pipeline_mode=pl.Buffered(1)),
            pl.BlockSpec((hd_all, d), const, pipeline_mode=pl.Buffered(1)),
            pl.BlockSpec((d, LANES), const),
            pl.BlockSpec((1, LANES), const),
        ],
        out_specs=(pl.BlockSpec((None, ts, hd_all), lambda bi, i: (bi, i, 0)),
                   pl.BlockSpec((None, hd_all, ts), lambda bi, i: (bi, 0, i)),
                   pl.BlockSpec((None, n_heads, ts, LANES), lambda bi, i: (bi, 0, i, 0)),
                   pl.BlockSpec((None, None, SUBLANES, LANES), lambda bi, i: (bi, i, 0, 0)),
                   pl.BlockSpec((None, None, SUBLANES, LANES), lambda bi, i: (bi, i, 0, 0))),
        scratch_shapes=[pltpu.VMEM((ts, ts), BF16), pltpu.VMEM((SUBLANES, LANES), F32)],
        compiler_params=_cparams(("parallel", "arbitrary"), 56),
        name="shared_kv",
    )(x3, gain.reshape(1, d), w_k, w_vt, w_p, b_p)


ATTN_BLOCK = 512
ATTN_GROUP_BLOCKS = 4
FINITE_LIMIT = 3.0e38
SKIP_MARGIN = 160.0


def _attn_kernel(end_ref, kmax_ref, q_ref, k_ref, aux_ref, vt_ref, o_ref, qp_ref, m_ref, l_ref,
                 acc_ref):
    t, hd = q_ref.shape
    i = pl.program_id(2)
    lane = lax.broadcasted_iota(I32, (t, LANES), 1)
    qp_ref[:, 0:hd] = q_ref[...]
    qp_ref[:, hd:hd + LANES] = jnp.where(lane < 3, 1.0, 0.0).astype(BF16)

    qf = q_ref[...].astype(F32)
    reach = jnp.sqrt(jnp.max(jnp.sum(qf * qf, axis=1, keepdims=True))) * kmax_ref[0, i]

    def alive(block):
        bias_top = end_ref[0, jnp.maximum(block, 0)]
        return jnp.logical_and(block >= 0,
                               reach + bias_top - jnp.min(m_ref[...]) > -SKIP_MARGIN)

    def scores(start, size, masked):
        rows = pl.ds(pl.multiple_of(start, t), size)
        kp = jnp.concatenate([k_ref[rows, :], aux_ref[rows, :]], axis=1)
        s_t = _dot_nt(kp, qp_ref[...])
        if masked:
            kr = lax.broadcasted_iota(I32, (size, t), 0)
            qc = lax.broadcasted_iota(I32, (size, t), 1)
            s_t = jnp.where(kr <= qc, s_t, MASK_VALUE)
        return s_t, rows

    def exact_block(start, masked):
        s_t, rows = scores(start, t, masked)
        m_old = m_ref[...]
        m_new = jnp.maximum(m_old, jnp.max(s_t, axis=0, keepdims=True))
        alpha = jnp.exp2(m_old - m_new)
        p = jnp.exp2(s_t - m_new)
        l_ref[...] = alpha * l_ref[...] + jnp.sum(p, axis=0, keepdims=True)
        pv = jnp.dot(vt_ref[:, rows], p.astype(BF16), preferred_element_type=F32)
        acc_ref[...] = alpha * acc_ref[...] + pv
        m_ref[...] = m_new

    def lagged_group(start, size):
        s_t, rows = scores(start, size, False)
        m_old = m_ref[...]
        p = jnp.exp2(s_t - m_old)
        pv = jnp.dot(vt_ref[:, rows], p.astype(BF16), preferred_element_type=F32)
        m_new = jnp.maximum(m_old, jnp.max(s_t, axis=0, keepdims=True))
        alpha = jnp.exp2(m_old - m_new)
        l_ref[...] = (l_ref[...] + jnp.sum(p, axis=0, keepdims=True)) * alpha
        acc_ref[...] = (acc_ref[...] + pv) * alpha
        m_ref[...] = m_new

    def reset():
        m_ref[...] = jnp.full_like(m_ref, MASK_VALUE)
        l_ref[...] = jnp.zeros_like(l_ref)
        acc_ref[...] = jnp.zeros_like(acc_ref)

    def finish():
        out = acc_ref[...] / l_ref[...]
        o_ref[...] = out.T.astype(o_ref.dtype)
        return out

    reset()
    exact_block(i * t, True)
    gb = ATTN_GROUP_BLOCKS
    n_groups = i // gb
    rest = i - n_groups * gb

    def group_cond(c):
        g, live = c
        return jnp.logical_and(g < n_groups, live)

    def group_body(c):
        g, _ = c
        first = i - (g + 1) * gb
        lagged_group(first * t, gb * t)
        return g + 1, alive(first - 1)

    _, live = lax.while_loop(group_cond, group_body, (0, alive(i - 1)))

    @pl.when(jnp.logical_and((rest & 2) != 0, live))
    def _():
        lagged_group((rest - 2) * t, 2 * t)

    @pl.when(jnp.logical_and((rest & 1) != 0, jnp.logical_and(live, alive(0))))
    def _():
        lagged_group(0, t)

    out = finish()
    overflowed = jnp.max(jnp.where(jnp.abs(out) < FINITE_LIMIT, 0.0, 1.0)) > 0.0

    @pl.when(overflowed)
    def _():
        reset()

        def exact_body(j, c):
            exact_block(j * t, False)
            return c

        lax.fori_loop(0, i, exact_body, 0)
        exact_block(i * t, True)
        finish()


def _attention(qg3, k3, aux, vt, block_end, block_knorm, n_heads):
    b, s, _ = k3.shape
    hd = k3.shape[2] // n_heads
    assert hd == LANES
    t = _tile(s, ATTN_BLOCK)
    by_head = lambda tab: tab[:, :, 0, :n_heads].transpose(0, 2, 1).reshape(b, n_heads, 1, s // t)
    end_tab = by_head(block_end)
    kmax_tab = jnp.sqrt(lax.cummax(by_head(block_knorm), axis=3))
    table = pl.BlockSpec((None, None, 1, s // t), lambda bi, h, i: (bi, h, 0, 0),
                         memory_space=pltpu.SMEM)
    return pl.pallas_call(
        _attn_kernel,
        out_shape=jax.ShapeDtypeStruct((b, s, n_heads * hd), BF16),
        grid=(b, n_heads, s // t),
        in_specs=[
            table,
            table,
            pl.BlockSpec((None, t, hd), lambda bi, h, i: (bi, i, h)),
            pl.BlockSpec((None, s, hd), lambda bi, h, i: (bi, 0, h)),
            pl.BlockSpec((None, None, s, LANES), lambda bi, h, i: (bi, h, 0, 0)),
            pl.BlockSpec((None, hd, s), lambda bi, h, i: (bi, h, 0)),
        ],
        out_specs=pl.BlockSpec((None, t, hd), lambda bi, h, i: (bi, i, h)),
        scratch_shapes=[
            pltpu.VMEM((t, hd + LANES), BF16),
            pltpu.VMEM((1, t), F32),
            pltpu.VMEM((1, t), F32),
            pltpu.VMEM((hd, t), F32),
        ],
        compiler_params=_cparams(("parallel", "parallel", "arbitrary"), 48),
        name="fox_attention",
    )(end_tab, kmax_tab, qg3, k3, aux, vt)


def _gated_out_kernel(o_ref, gate_ref, x_ref, w_ref, out_ref):
    g = o_ref[...].astype(F32) * jax.nn.sigmoid(gate_ref[...].astype(F32))
    out_ref[...] = x_ref[...] + jnp.dot(g.astype(BF16), w_ref[...], preferred_element_type=F32)


def _gated_out(o2, qg, x2, w_o):
    n, d = x2.shape
    hd_all = o2.shape[1]
    tm = _tile(n, 512)
    return pl.pallas_call(
        _gated_out_kernel,
        out_shape=jax.ShapeDtypeStruct((n, d), F32),
        grid=(n // tm,),
        in_specs=[
            pl.BlockSpec((tm, hd_all), lambda i: (i, 0)),
            pl.BlockSpec((tm, hd_all), lambda i: (i, 1)),
            pl.BlockSpec((tm, d), lambda i: (i, 0)),
            pl.BlockSpec((hd_all, d), lambda i: (0, 0)),
        ],
        out_specs=pl.BlockSpec((tm, d), lambda i: (i, 0)),
        compiler_params=_cparams(("parallel",), 48),
        name="gated_out_proj",
    )(o2, qg, x2, w_o)


def kernel(x, a_norm, a_w_in, a_conv_w, a_conv_b, a_w_rec, a_b_rec, a_w_inp, a_b_inp, a_lambda, a_w_out, kv_norm, kv_w, kv_b_forget, b_norm, b_w_qg, b_w_o, m_norm, m_w_group, m_b_group, m_w_router, m_b_router, m_w_in, m_w_out, final_norm):
    b, s, d = x.shape
    n = b * s
    depth = m_norm.shape[0]
    n_a = a_norm.shape[0]
    n_heads = kv_b_forget.shape[0]
    hd_all = b_w_o.shape[1]
    head_dim = hd_all // n_heads
    x2 = x.reshape(n, d)
    k3 = aux = vt = block_end = block_knorm = None
    for layer in range(depth):
        if layer < n_a:
            i = layer
            d_rnn = a_w_out.shape[1]
            proj = _norm_proj(x2, a_norm[i], a_w_in[i].astype(BF16), jnp.ones((2 * d_rnn,), F32),
                              BF16, "rglru_in_proj")
            x2 = _rglru(proj, x2.reshape(b, s, d), a_conv_w[i], a_conv_b[i], a_w_rec[i], a_b_rec[i],
                        a_w_inp[i], a_b_inp[i], a_lambda[i], a_w_out[i]).reshape(n, d)
        else:
            j = layer - n_a
            q_scale = jnp.concatenate([jnp.full((hd_all,), head_dim ** -0.5 * LOG2E, F32),
                                       jnp.ones((hd_all,), F32)])
            qg = _norm_proj(x2, b_norm[j], b_w_qg[j].astype(BF16), q_scale, BF16, "fox_qg_proj")
            o = _attention(qg.reshape(b, s, 2 * hd_all), k3, aux, vt, block_end, block_knorm,
                           n_heads)
            x2 = _gated_out(o.reshape(n, hd_all), qg, x2, b_w_o[j].astype(BF16))
        last = layer == depth - 1
        x2 = _moe(x2, m_norm[layer], m_w_group[layer], m_b_group[layer], m_w_router[layer],
                  m_b_router[layer], m_w_in, m_w_out, layer, final_norm, last)
        if layer == n_a - 1:
            k3, vt, aux, block_end, block_knorm = _shared_kv(
                x2.reshape(b, s, d), kv_norm, kv_w[:, :hd_all].astype(BF16),
                kv_w[:, hd_all:2 * hd_all].T.astype(BF16), kv_w[:, 2 * hd_all:], kv_b_forget)
    if depth == 0:
        x2 = _rms(x2, final_norm)
    return x2.reshape(b, s, d)
```

```python
import functools
import math

import jax
import jax.numpy as jnp
from jax import lax
from jax.experimental import pallas as pl
from jax.experimental.pallas import tpu as pltpu

F32 = jnp.float32
BF16 = jnp.bfloat16
I32 = jnp.int32

EPS = 1e-6
LRU_C = 8.0
TOP_K = 2
LOG2E = 1.4426950408889634
MASK_VALUE = -1e30

V7X_VMEM_BYTES = 64 * 1024 * 1024
SUBLANES = 8
LANES = 128
MIB = 1024 * 1024


def _cparams(semantics, vmem_mib):
    assert vmem_mib * MIB < V7X_VMEM_BYTES
    return pltpu.CompilerParams(dimension_semantics=semantics, vmem_limit_bytes=vmem_mib * MIB)


def _tile(dim, pref):
    t = min(dim, pref)
    assert dim % t == 0, (dim, pref)
    return t


def _rms(x, gain):
    return x * lax.rsqrt(jnp.mean(x * x, axis=-1, keepdims=True) + EPS) * gain


def _sigmoid(x):
    return 0.5 * jnp.tanh(0.5 * x) + 0.5


def _norm_proj_kernel(x_ref, g_ref, w_ref, s_ref, o_ref, hn_ref):
    @pl.when(pl.program_id(1) == 0)
    def _():
        hn_ref[...] = _rms(x_ref[...], g_ref[...]).astype(BF16)

    acc = jnp.dot(hn_ref[...], w_ref[...], preferred_element_type=F32)
    o_ref[...] = (acc * s_ref[...]).astype(o_ref.dtype)


def _norm_proj(x, gain, w, col_scale, out_dtype, name):
    n, d = x.shape
    n_out = w.shape[1]
    tm = _tile(n, 1024)
    tn = _tile(n_out, 1024)
    return pl.pallas_call(
        _norm_proj_kernel,
        out_shape=jax.ShapeDtypeStruct((n, n_out), out_dtype),
        grid=(n // tm, n_out // tn),
        in_specs=[
            pl.BlockSpec((tm, d), lambda i, j: (i, 0)),
            pl.BlockSpec((1, d), lambda i, j: (0, 0)),
            pl.BlockSpec((d, tn), lambda i, j: (0, j)),
            pl.BlockSpec((1, tn), lambda i, j: (0, j)),
        ],
        out_specs=pl.BlockSpec((tm, tn), lambda i, j: (i, j)),
        scratch_shapes=[pltpu.VMEM((tm, d), BF16)],
        compiler_params=_cparams(("parallel", "arbitrary"), 48),
        name=name,
    )(x, gain.reshape(1, d), w, col_scale.reshape(1, n_out))


SCAN_STRIP = 1024


def _rglru_kernel(u_ref, y_ref, x_ref, cw_ref, cb_ref, wr_ref, br_ref, wi_ref, bi_ref,
                  lam_ref, wo_ref, o_ref, ubuf_ref, a_ref, b_ref, h_ref, *, conv_width):
    ts, d = u_ref.shape
    nb, blk, _ = wr_ref.shape
    halo = SUBLANES

    @pl.when(pl.program_id(1) == 0)
    def _():
        ubuf_ref[0:halo, :] = jnp.zeros((halo, d), F32)
        h_ref[...] = jnp.zeros_like(h_ref)

    u = u_ref[...].astype(F32)
    ubuf_ref[halo:halo + ts, :] = u
    uc = u * cw_ref[conv_width - 1:conv_width, :] + cb_ref[...]
    for j in range(conv_width - 1):
        shift = conv_width - 1 - j
        uc = uc + ubuf_ref[halo - shift:halo - shift + ts, :] * cw_ref[j:j + 1, :]
    ubuf_ref[0:halo, :] = u[ts - halo:, :]

    ub = uc.astype(BF16)
    r_parts, i_parts = [], []
    for n in range(nb):
        ubn = ub[:, n * blk:(n + 1) * blk]
        r_parts.append(jnp.dot(ubn, wr_ref[n], preferred_element_type=F32))
        i_parts.append(jnp.dot(ubn, wi_ref[n], preferred_element_type=F32))
    r = _sigmoid(jnp.concatenate(r_parts, axis=1) + br_ref[...])
    gi = _sigmoid(jnp.concatenate(i_parts, axis=1) + bi_ref[...])
    lam = lam_ref[...]
    sp = jnp.maximum(-lam, 0.0) + jnp.log(1.0 + jnp.exp(-jnp.abs(lam)))
    log_a = (-LRU_C * r) * sp
    th = jnp.tanh(0.5 * log_a)
    inv = 1.0 / (1.0 - th)
    a_ref[...] = (1.0 + th) * inv
    b_ref[...] = (2.0 * inv) * jnp.sqrt(-th) * (gi * uc)

    row = lax.broadcasted_iota(I32, (SUBLANES, SCAN_STRIP), 0)
    strip = min(SCAN_STRIP, d)
    for c in range(d // strip):
        cols = pl.ds(c * strip, strip)

        def tile_step(t, h):
            rows = pl.ds(pl.multiple_of(t * SUBLANES, SUBLANES), SUBLANES)
            a = a_ref[rows, cols]
            bb = b_ref[rows, cols]
            for sh in (1, 2, 4):
                keep = row[:, :strip] >= sh
                a_prev = jnp.where(keep, pltpu.roll(a, sh, 0), 1.0)
                b_prev = jnp.where(keep, pltpu.roll(bb, sh, 0), 0.0)
                bb = a * b_prev + bb
                a = a * a_prev
            hs = a * h + bb
            b_ref[rows, cols] = hs
            return jnp.broadcast_to(hs[SUBLANES - 1:SUBLANES, :], (SUBLANES, strip))

        h_ref[:, cols] = lax.fori_loop(0, ts // SUBLANES, tile_step, h_ref[:, cols], unroll=2)

    y = y_ref[...].astype(F32)
    gelu = 0.5 * y * (1.0 + jnp.tanh(math.sqrt(2.0 / math.pi) * (y + 0.044715 * (y * y * y))))
    g = (b_ref[...] * gelu).astype(BF16)
    o_ref[...] = x_ref[...] + jnp.dot(g, wo_ref[...], preferred_element_type=F32)


def _rglru(proj, x3, conv_w, conv_b, w_rec, b_rec, w_inp, b_inp, lam, w_out):
    b, s, d = x3.shape
    d_rnn = w_out.shape[0]
    nb, blk, _ = w_rec.shape
    width = conv_w.shape[0]
    assert width - 1 <= SUBLANES
    ts = _tile(s, 256)
    proj3 = proj.reshape(b, s, 2 * d_rnn)
    row = lambda v: v.reshape(1, -1)
    const2 = lambda bi, i: (0, 0)
    const3 = lambda bi, i: (0, 0, 0)
    return pl.pallas_call(
        functools.partial(_rglru_kernel, conv_width=width),
        out_shape=jax.ShapeDtypeStruct((b, s, d), F32),
        grid=(b, s // ts),
        in_specs=[
            pl.BlockSpec((None, ts, d_rnn), lambda bi, i: (bi, i, 0)),
            pl.BlockSpec((None, ts, d_rnn), lambda bi, i: (bi, i, 1)),
            pl.BlockSpec((None, ts, d), lambda bi, i: (bi, i, 0)),
            pl.BlockSpec((width, d_rnn), const2),
            pl.BlockSpec((1, d_rnn), const2),
            pl.BlockSpec((nb, blk, blk), const3),
            pl.BlockSpec((1, d_rnn), const2),
            pl.BlockSpec((nb, blk, blk), const3),
            pl.BlockSpec((1, d_rnn), const2),
            pl.BlockSpec((1, d_rnn), const2),
            pl.BlockSpec((d_rnn, d), const2),
        ],
        out_specs=pl.BlockSpec((None, ts, d), lambda bi, i: (bi, i, 0)),
        scratch_shapes=[
            pltpu.VMEM((SUBLANES + ts, d_rnn), F32),
            pltpu.VMEM((ts, d_rnn), F32),
            pltpu.VMEM((ts, d_rnn), F32),
            pltpu.VMEM((SUBLANES, d_rnn), F32),
        ],
        compiler_params=_cparams(("parallel", "arbitrary"), 56),
        name="rglru",
    )(proj3, proj3, x3, conv_w, row(conv_b), w_rec.astype(BF16), row(b_rec),
      w_inp.astype(BF16), row(b_inp), row(lam), w_out.astype(BF16))


def _pack_bf16_pair(x):
    c = x.shape[1] // 2
    lo = lax.bitcast_convert_type(x[:, :c].astype(BF16).astype(F32), I32)
    hi = lax.bitcast_convert_type(x[:, c:].astype(BF16).astype(F32), I32)
    return jnp.bitwise_or(hi, lax.shift_right_logical(lo, jnp.int32(16)))


def _unpack_bf16_pair(w):
    lo = lax.bitcast_convert_type(lax.shift_left(w, jnp.int32(16)), F32)
    hi = lax.bitcast_convert_type(jnp.bitwise_and(w, jnp.int32(-65536)), F32)
    return lo, hi


def _split3(v):
    v1 = v.astype(BF16)
    r1 = v - v1.astype(F32)
    v2 = r1.astype(BF16)
    v3 = (r1 - v2.astype(F32)).astype(BF16)
    return v1, v2, v3


def _dot_nt(a, b):
    return lax.dot_general(a, b, (((1,), (1,)), ((), ())), preferred_element_type=F32)


def _first_argmax(v, n):
    idx = lax.broadcasted_iota(I32, v.shape, 0)
    vmax = jnp.max(v, axis=0, keepdims=True)
    amax = jnp.min(jnp.where(v == vmax, idx, n), axis=0, keepdims=True)
    return amax, vmax


def _router_kernel(x_ref, g_ref, w1_ref, w2_ref, bias_ref, hn_ref, eid_ref, wt_ref,
                   rank_ref, cnt_ref, tri_ref, carry_ref, *, n_groups, per_group):
    tm = x_ref.shape[0]
    n_exp = n_groups * per_group

    @pl.when(pl.program_id(0) == 0)
    def _():
        r = lax.broadcasted_iota(I32, (tm, tm), 0)
        c = lax.broadcasted_iota(I32, (tm, tm), 1)
        tri_ref[...] = (r < c).astype(BF16)
        carry_ref[...] = jnp.zeros_like(carry_ref)

    hn = _rms(x_ref[...], g_ref[...])
    hn_ref[...] = _pack_bf16_pair(hn)
    h1, h2, _ = _split3(hn)
    w1, w2 = w1_ref[...], w2_ref[...]
    logits = _dot_nt(w1, h1) + (_dot_nt(w1, h2) + _dot_nt(w2, h1))
    logits = logits + bias_ref[:, 0:1]

    gl = logits[0:n_groups, :]
    g_idx, g_max = _first_argmax(gl, n_groups)
    gp_top = 1.0 / jnp.sum(jnp.exp(gl - g_max), axis=0, keepdims=True)
    el = jnp.zeros((per_group, tm), F32)
    for g in range(n_groups):
        lo = n_groups + g * per_group
        el = jnp.where(g_idx == g, logits[lo:lo + per_group, :], el)
    e_max = jnp.max(el, axis=0, keepdims=True)
    ex = jnp.exp(el - e_max)
    ep = ex / jnp.sum(ex, axis=0, keepdims=True)
    e1, p1 = _first_argmax(ep, per_group)
    sub = lax.broadcasted_iota(I32, ep.shape, 0)
    e2, p2 = _first_argmax(jnp.where(sub == e1, -1.0, ep), per_group)
    denom = p1 + p2
    eid1 = g_idx * per_group + e1
    eid2 = g_idx * per_group + e2
    eid_ref[...] = jnp.concatenate([eid1, eid2], axis=0)
    wt_ref[...] = jnp.concatenate([gp_top * p1 / denom, gp_top * p2 / denom], axis=0)

    e_iota = lax.broadcasted_iota(I32, (n_exp, tm), 0)
    hot1 = e_iota == eid1
    hot2 = e_iota == eid2
    chosen = jnp.logical_or(hot1, hot2)
    before = jnp.dot(chosen.astype(BF16), tri_ref[...], preferred_element_type=F32)
    base = (before + carry_ref[:, 0:1]).astype(I32)
    rank1 = jnp.sum(jnp.where(hot1, base, 0), axis=0, keepdims=True)
    rank2 = jnp.sum(jnp.where(hot2, base, 0), axis=0, keepdims=True)
    rank_ref[...] = jnp.concatenate([rank1, rank2], axis=0)
    carry_ref[...] = carry_ref[...] + jnp.sum(chosen.astype(F32), axis=1, keepdims=True)
    cnt_ref[...] = carry_ref[...].astype(I32)


def _router(x2, gain, w_group, b_group, w_router, b_router):
    n, d = x2.shape
    n_groups = w_group.shape[1]
    n_exp = w_router.shape[1]
    per_group = n_exp // n_groups
    assert per_group == SUBLANES and n_groups == SUBLANES
    rows = n_groups + n_exp
    rows_p = -(-rows // LANES) * LANES
    w_t = jnp.concatenate([w_group, w_router], axis=1).T
    w_t = jnp.pad(w_t, ((0, rows_p - rows), (0, 0)))
    w1, w2, _ = _split3(w_t)
    bias = jnp.pad(jnp.concatenate([b_group, b_router]), (0, rows_p - rows))
    bias = jnp.broadcast_to(bias[:, None], (rows_p, LANES))
    tm = _tile(n, 512)
    const = lambda i: (0, 0)
    return pl.pallas_call(
        functools.partial(_router_kernel, n_groups=n_groups, per_group=per_group),
        out_shape=(
            jax.ShapeDtypeStruct((n, d // 2), I32),
            jax.ShapeDtypeStruct((TOP_K, n), I32),
            jax.ShapeDtypeStruct((TOP_K, n), F32),
            jax.ShapeDtypeStruct((TOP_K, n), I32),
            jax.ShapeDtypeStruct((n_exp, LANES), I32),
        ),
        grid=(n // tm,),
        in_specs=[
            pl.BlockSpec((tm, d), lambda i: (i, 0)),
            pl.BlockSpec((1, d), const),
            pl.BlockSpec((rows_p, d), const),
            pl.BlockSpec((rows_p, d), const),
            pl.BlockSpec((rows_p, LANES), const),
        ],
        out_specs=(
            pl.BlockSpec((tm, d // 2), lambda i: (i, 0)),
            pl.BlockSpec((TOP_K, tm), lambda i: (0, i)),
            pl.BlockSpec((TOP_K, tm), lambda i: (0, i)),
            pl.BlockSpec((TOP_K, tm), lambda i: (0, i)),
            pl.BlockSpec((n_exp, LANES), const),
        ),
        scratch_shapes=[pltpu.VMEM((tm, tm), BF16), pltpu.VMEM((n_exp, LANES), F32)],
        compiler_params=_cparams(("arbitrary",), 40),
        name="moe_router",
    )(x2, gain.reshape(1, d), w1, w2, bias)


def _dest_kernel(eid_ref, rank_ref, start_ref, dest_ref):
    eid = eid_ref[...]
    n_exp = start_ref.shape[0]
    dest = rank_ref[...]
    for k in range(eid.shape[0]):
        hot = lax.broadcasted_iota(I32, (n_exp, eid.shape[1]), 0) == eid[k:k + 1, :]
        off = jnp.sum(jnp.where(hot, start_ref[:, 0:1], 0), axis=0, keepdims=True)
        dest_ref[k:k + 1, :] = dest[k:k + 1, :] + off


def _dest_rows(eid, rank, seg_start):
    k, n = eid.shape
    n_exp = seg_start.shape[0]
    tm = _tile(n, 2048)
    start = jnp.broadcast_to(seg_start[:, None], (n_exp, LANES)).astype(I32)
    return pl.pallas_call(
        _dest_kernel,
        out_shape=jax.ShapeDtypeStruct((k, n), I32),
        grid=(n // tm,),
        in_specs=[
            pl.BlockSpec((k, tm), lambda i: (0, i)),
            pl.BlockSpec((k, tm), lambda i: (0, i)),
            pl.BlockSpec((n_exp, LANES), lambda i: (0, 0)),
        ],
        out_specs=pl.BlockSpec((k, tm), lambda i: (0, i)),
        compiler_params=_cparams(("parallel",), 16),
        name="moe_dest",
    )(eid, rank, start)


def _dispatch_kernel(nv_ref, dest_ref, hn_ref, xs_ref, zero_ref, sem, zsem, *, chunk_rows):
    k, tm = dest_ref.shape[1], dest_ref.shape[2]
    n_chunks = nv_ref.shape[0]

    @pl.when(pl.program_id(0) == 0)
    def _():
        zero_ref[...] = jnp.zeros_like(zero_ref)

        def zero_copy(c):
            rows = pl.ds(pl.multiple_of(c * chunk_rows, chunk_rows), chunk_rows)
            return pltpu.make_async_copy(zero_ref, xs_ref.at[rows], zsem)

        def start(c, carry):
            @pl.when(nv_ref[c] < chunk_rows)
            def _():
                zero_copy(c).start()
            return carry

        def finish(c, carry):
            @pl.when(nv_ref[c] < chunk_rows)
            def _():
                zero_copy(c).wait()
            return carry

        lax.fori_loop(0, n_chunks, start, 0)
        lax.fori_loop(0, n_chunks, finish, 0)


    def issue(t, c):
        for kk in range(k):
            pltpu.make_async_copy(hn_ref.at[pl.ds(t, 1)],
                                  xs_ref.at[pl.ds(dest_ref[0, kk, t], 1)], sem).start()
        return c

    lax.fori_loop(0, tm, issue, 0, unroll=8)
    for kk in range(k):
        pltpu.make_async_copy(hn_ref, xs_ref.at[pl.ds(0, tm)], sem).wait()


def _dispatch(dest, hn, n_valid, n_rows, chunk_rows):
    k, n = dest.shape
    d = hn.shape[1]
    tm = _tile(n, 512)
    dest3 = dest.reshape(k, n // tm, tm).transpose(1, 0, 2)
    grid_spec = pltpu.PrefetchScalarGridSpec(
        num_scalar_prefetch=1,
        grid=(n // tm,),
        in_specs=[
            pl.BlockSpec((1, k, tm), lambda i, nv: (i, 0, 0), memory_space=pltpu.SMEM),
            pl.BlockSpec((tm, d), lambda i, nv: (i, 0)),
        ],
        out_specs=pl.BlockSpec(memory_space=pl.ANY),
        scratch_shapes=[pltpu.VMEM((chunk_rows, d), hn.dtype), pltpu.SemaphoreType.DMA(()),
                        pltpu.SemaphoreType.DMA(())],
    )
    return pl.pallas_call(
        functools.partial(_dispatch_kernel, chunk_rows=chunk_rows),
        out_shape=jax.ShapeDtypeStruct((n_rows, d), hn.dtype),
        grid_spec=grid_spec,
        compiler_params=_cparams(("arbitrary",), 24),
        name="moe_dispatch",
    )(n_valid, dest3, hn)


def _expert_kernel(ce_ref, nu_ref, fresh_ref, next_ref, slot_ref, xs_ref, wi_hbm, wo_hbm, ys_ref,
                   wif_ref, wof_ref, wib_ref, wob_ref, sem_ref, *, d_expert, layer):
    i = pl.program_id(0)

    def weight_copies(e, s):
        return (pltpu.make_async_copy(wi_hbm.at[layer, e], wif_ref.at[s], sem_ref.at[0, s]),
                pltpu.make_async_copy(wo_hbm.at[layer, e], wof_ref.at[s], sem_ref.at[1, s]))

    @pl.when(i < nu_ref[0])
    def _():
        @pl.when(fresh_ref[i] == 1)
        def _():
            s = slot_ref[i]

            @pl.when(i == 0)
            def _():
                for c in weight_copies(ce_ref[i], s):
                    c.start()

            @pl.when(next_ref[i] >= 0)
            def _():
                for c in weight_copies(next_ref[i], 1 - s):
                    c.start()

            for c in weight_copies(ce_ref[i], s):
                c.wait()
            wib_ref[...] = wif_ref[s].astype(BF16)
            wob_ref[...] = wof_ref[s].astype(BF16)

        x_lo, x_hi = _unpack_bf16_pair(xs_ref[...])
        xb = jnp.concatenate([x_lo, x_hi], axis=1).astype(BF16)
        gu = jnp.dot(xb, wib_ref[...], preferred_element_type=F32)
        gate = gu[:, :d_expert]
        act = (gate * jax.nn.sigmoid(gate) * gu[:, d_expert:]).astype(BF16)
        ys_ref[...] = _pack_bf16_pair(jnp.dot(act, wob_ref[...], preferred_element_type=F32))

    @pl.when(i >= nu_ref[0])
    def _():
        ys_ref[...] = jnp.zeros_like(ys_ref)


def _experts(xs, chunk_e, n_used, fresh, next_e, slot, w_in_all, w_out_all, layer, chunk_rows):
    n_rows, d_packed = xs.shape
    d = 2 * d_packed
    d_expert = w_out_all.shape[2]
    n_chunks = n_rows // chunk_rows
    xs_map = lambda i, ce, nu, fr, nx, sl: (jnp.minimum(i, jnp.maximum(nu[0] - 1, 0)), 0)
    grid_spec = pltpu.PrefetchScalarGridSpec(
        num_scalar_prefetch=5,
        grid=(n_chunks,),
        in_specs=[
            pl.BlockSpec((chunk_rows, d_packed), xs_map),
            pl.BlockSpec(memory_space=pl.ANY),
            pl.BlockSpec(memory_space=pl.ANY),
        ],
        out_specs=pl.BlockSpec((chunk_rows, d_packed), lambda i, ce, nu, fr, nx, sl: (i, 0)),
        scratch_shapes=[
            pltpu.VMEM((2, d, 2 * d_expert), F32),
            pltpu.VMEM((2, d_expert, d), F32),
            pltpu.VMEM((d, 2 * d_expert), BF16),
            pltpu.VMEM((d_expert, d), BF16),
            pltpu.SemaphoreType.DMA((2, 2)),
        ],
    )
    return pl.pallas_call(
        functools.partial(_expert_kernel, d_expert=d_expert, layer=layer),
        out_shape=jax.ShapeDtypeStruct((n_rows, d_packed), I32),
        grid_spec=grid_spec,
        compiler_params=_cparams(("arbitrary",), 56),
        name="moe_experts",
    )(chunk_e, n_used, fresh, next_e, slot, xs, w_in_all, w_out_all)


def _combine_kernel(dest_ref, next_ref, x_ref, wt_ref, g_ref, ys_ref, o_ref, buf_ref, sem_ref, *,
                    final_norm, n_blocks):
    k, tm = dest_ref.shape[1], dest_ref.shape[2]
    i = pl.program_id(0)

    def issue_block(d_ref, slot):
        def body(t, c):
            for kk in range(k):
                pltpu.make_async_copy(ys_ref.at[pl.ds(d_ref[0, kk, t], 1)],
                                      buf_ref.at[slot, kk, pl.ds(t, 1)], sem_ref.at[slot]).start()
            return c
        lax.fori_loop(0, tm, body, 0, unroll=8)

    @pl.when(i == 0)
    def _():
        issue_block(dest_ref, 0)

    @pl.when(i + 1 < n_blocks)
    def _():
        issue_block(next_ref, (i + 1) % 2)

    slot = i % 2
    for kk in range(k):
        pltpu.make_async_copy(ys_ref.at[pl.ds(0, tm)], buf_ref.at[slot, kk], sem_ref.at[slot]).wait()

    c = buf_ref.shape[3]
    out_lo = x_ref[:, :c]
    out_hi = x_ref[:, c:]
    for kk in range(k):
        y_lo, y_hi = _unpack_bf16_pair(buf_ref[slot, kk])
        w = wt_ref[:, kk:kk + 1]
        out_lo = out_lo + w * y_lo
        out_hi = out_hi + w * y_hi
    if final_norm:
        ms = (jnp.sum(out_lo * out_lo, axis=-1, keepdims=True)
              + jnp.sum(out_hi * out_hi, axis=-1, keepdims=True)) / (2 * c)
        scale = lax.rsqrt(ms + EPS)
        out_lo = out_lo * scale * g_ref[:, :c]
        out_hi = out_hi * scale * g_ref[:, c:]
    o_ref[:, :c] = out_lo
    o_ref[:, c:] = out_hi


def _combine(dest, x2, wts, ys, gain, final_norm):
    k, n = dest.shape
    d = x2.shape[1]
    tm = _tile(n, 512)
    n_blocks = n // tm
    dest3 = dest.reshape(k, n_blocks, tm).transpose(1, 0, 2)
    return pl.pallas_call(
        functools.partial(_combine_kernel, final_norm=final_norm, n_blocks=n_blocks),
        out_shape=jax.ShapeDtypeStruct((n, d), F32),
        grid=(n_blocks,),
        in_specs=[
            pl.BlockSpec((1, k, tm), lambda i: (i, 0, 0), memory_space=pltpu.SMEM),
            pl.BlockSpec((1, k, tm), lambda i: (jnp.minimum(i + 1, n_blocks - 1), 0, 0),
                         memory_space=pltpu.SMEM),
            pl.BlockSpec((tm, d), lambda i: (i, 0)),
            pl.BlockSpec((tm, k), lambda i: (i, 0)),
            pl.BlockSpec((1, d), lambda i: (0, 0)),
            pl.BlockSpec(memory_space=pl.ANY),
        ],
        out_specs=pl.BlockSpec((tm, d), lambda i: (i, 0)),
        scratch_shapes=[pltpu.VMEM((2, k, tm, d // 2), I32), pltpu.SemaphoreType.DMA((2,))],
        compiler_params=_cparams(("arbitrary",), 40),
        name="moe_combine",
    )(dest3, dest3, x2, wts.T, gain.reshape(1, d), ys)


EXPERT_CHUNK_ROWS = 256


def _moe(x2, norm_gain, w_group, b_group, w_router, b_router, w_in_all, w_out_all, layer, out_gain,
         final_norm):
    n, d = x2.shape
    n_exp = w_router.shape[1]
    chunk = EXPERT_CHUNK_ROWS
    hn, eid, wts, rank, counts = _router(x2, norm_gain, w_group, b_group, w_router, b_router)
    counts = counts[:, 0]
    padded = (counts + chunk - 1) // chunk * chunk
    seg_end = jnp.cumsum(padded)
    seg_start = seg_end - padded
    n_rows = (-(-(n * TOP_K) // chunk) + n_exp) * chunk
    n_chunks = n_rows // chunk
    chunk_start = jnp.arange(n_chunks, dtype=I32) * chunk
    n_used = (seg_end[-1:] // chunk).astype(I32)
    live_start = jnp.minimum(chunk_start, jnp.maximum(seg_end[-1] - chunk, 0))
    chunk_e = jnp.sum(seg_end[None, :] <= live_start[:, None], axis=1).astype(I32)
    chunk_e = jnp.minimum(chunk_e, n_exp - 1)
    mine = chunk_e[:, None] == jnp.arange(n_exp, dtype=I32)[None, :]
    left = jnp.sum(jnp.where(mine, (counts + seg_start)[None, :], 0), axis=1) - chunk_start
    n_valid = jnp.clip(left, 0, chunk).astype(I32)
    fresh = jnp.concatenate([jnp.ones((1,), I32), (chunk_e[1:] != chunk_e[:-1]).astype(I32)])
    dest = _dest_rows(eid, rank, seg_start)
    xs = _dispatch(dest, hn, n_valid, n_rows, chunk)
    idx = jnp.arange(n_chunks, dtype=I32)
    slot = ((jnp.cumsum(fresh) - 1) % 2).astype(I32)
    starts = jnp.where(jnp.logical_and(fresh == 1, idx < n_used[0]), idx, n_chunks)
    next_start = jnp.concatenate([lax.cummin(starts[::-1])[::-1][1:],
                                  jnp.full((1,), n_chunks, I32)])
    next_e = jnp.sum(jnp.where(idx[None, :] == next_start[:, None], chunk_e[None, :] + 1, 0),
                     axis=1).astype(I32) - 1
    ys = _experts(xs, chunk_e, n_used, fresh, next_e, slot, w_in_all, w_out_all, layer, chunk)
    return _combine(dest, x2, wts, ys, out_gain, final_norm)


def _shared_kv_kernel(x_ref, g_ref, wk_ref, wvt_ref, w_ref, b_ref, k_ref, vt_ref, aux_ref, end_ref,
                      kn_ref, tri_ref, carry_ref, *, n_heads):
    ts = x_ref.shape[0]

    @pl.when(pl.program_id(1) == 0)
    def _():
        r = lax.broadcasted_iota(I32, (ts, ts), 0)
        c = lax.broadcasted_iota(I32, (ts, ts), 1)
        tri_ref[...] = (c <= r).astype(BF16)
        carry_ref[...] = jnp.zeros_like(carry_ref)

    hn = _rms(x_ref[...], g_ref[...]).astype(BF16)
    kb = jnp.dot(hn, wk_ref[...], preferred_element_type=F32).astype(k_ref.dtype)
    k_ref[...] = kb
    sq = kb.astype(F32) * kb.astype(F32)
    hd = kb.shape[1] // n_heads
    tile_lane = lax.broadcasted_iota(I32, (SUBLANES, LANES), 1)
    norms = jnp.zeros((SUBLANES, LANES), F32)
    for h in range(n_heads):
        row_sq = jnp.sum(sq[:, h * hd:(h + 1) * hd], axis=1, keepdims=True)
        norms = jnp.where(tile_lane == h, jnp.max(row_sq, axis=0, keepdims=True), norms)
    kn_ref[...] = norms
    vt_ref[...] = _dot_nt(wvt_ref[...], hn).astype(vt_ref.dtype)
    f = jnp.dot(hn, w_ref[...], preferred_element_type=F32) + b_ref[...]
    logf = jnp.minimum(f, 0.0) - jnp.log(1.0 + jnp.exp(-jnp.abs(f)))
    l1, l2, l3 = _split3(logf)
    tri = tri_ref[...]
    cum = (jnp.dot(tri, l1, preferred_element_type=F32)
           + jnp.dot(tri, l2, preferred_element_type=F32)
           + jnp.dot(tri, l3, preferred_element_type=F32)) + carry_ref[0:1, :]
    carry_ref[...] = jnp.broadcast_to(cum[ts - 1:ts, :], carry_ref.shape)
    end_ref[...] = carry_ref[...] * (-LOG2E)
    c1, c2, c3 = [c.astype(F32) for c in _split3(cum * (-LOG2E))]
    lane = lax.broadcasted_iota(I32, (ts, LANES), 1)
    for h in range(n_heads):
        col = lambda v: jnp.broadcast_to(v[:, h:h + 1], (ts, LANES))
        aux = jnp.where(lane == 0, col(c1), jnp.where(lane == 1, col(c2),
                        jnp.where(lane == 2, col(c3), 0.0)))
        aux_ref[h] = aux.astype(BF16)


def _shared_kv(x3, gain, w_k, w_vt, w_f, b_f):
    b, s, d = x3.shape
    hd_all = w_k.shape[1]
    n_heads = w_f.shape[1]
    assert n_heads <= LANES
    w_p = jnp.pad(w_f, ((0, 0), (0, LANES - n_heads))).astype(BF16)
    b_p = jnp.pad(b_f, (0, LANES - n_heads)).reshape(1, LANES)
    ts = _tile(s, ATTN_BLOCK)
    const = lambda bi, i: (0, 0)
    return pl.pallas_call(
        functools.partial(_shared_kv_kernel, n_heads=n_heads),
        out_shape=(jax.ShapeDtypeStruct((b, s, hd_all), BF16),
                   jax.ShapeDtypeStruct((b, hd_all, s), BF16),
                   jax.ShapeDtypeStruct((b, n_heads, s, LANES), BF16),
                   jax.ShapeDtypeStruct((b, s // ts, SUBLANES, LANES), F32),
                   jax.ShapeDtypeStruct((b, s // ts, SUBLANES, LANES), F32)),
        grid=(b, s // ts),
        in_specs=[
            pl.BlockSpec((None, ts, d), lambda bi, i: (bi, i, 0)),
            pl.BlockSpec((1, d), const),
            pl.BlockSpec((d, hd_all), const, pipeline_mode=pl.Buffered(1)),
            pl.BlockSpec((hd_all, d), const, pipeline_mode=pl.Buffered(1)),
            pl.BlockSpec((d, LANES), const),
            pl.BlockSpec((1, LANES), const),
        ],
        out_specs=(pl.BlockSpec((None, ts, hd_all), lambda bi, i: (bi, i, 0)),
                   pl.BlockSpec((None, hd_all, ts), lambda bi, i: (bi, 0, i)),
                   pl.BlockSpec((None, n_heads, ts, LANES), lambda bi, i: (bi, 0, i, 0)),
                   pl.BlockSpec((None, None, SUBLANES, LANES), lambda bi, i: (bi, i, 0, 0)),
                   pl.BlockSpec((None, None, SUBLANES, LANES), lambda bi, i: (bi, i, 0, 0))),
        scratch_shapes=[pltpu.VMEM((ts, ts), BF16), pltpu.VMEM((SUBLANES, LANES), F32)],
        compiler_params=_cparams(("parallel", "arbitrary"), 56),
        name="shared_kv",
    )(x3, gain.reshape(1, d), w_k, w_vt, w_p, b_p)


ATTN_BLOCK = 512
ATTN_GROUP_BLOCKS = 4
FINITE_LIMIT = 3.0e38
SKIP_MARGIN = 160.0


def _attn_kernel(end_ref, kmax_ref, q_ref, k_ref, aux_ref, vt_ref, o_ref, qp_ref, m_ref, l_ref,
                 acc_ref):
    t, hd = q_ref.shape
    i = pl.program_id(2)
    lane = lax.broadcasted_iota(I32, (t, LANES), 1)
    qp_ref[:, 0:hd] = q_ref[...]
    qp_ref[:, hd:hd + LANES] = jnp.where(lane < 3, 1.0, 0.0).astype(BF16)

    qf = q_ref[...].astype(F32)
    reach = jnp.sqrt(jnp.max(jnp.sum(qf * qf, axis=1, keepdims=True))) * kmax_ref[0, i]

    def alive(block):
        bias_top = end_ref[0, jnp.maximum(block, 0)]
        return jnp.logical_and(block >= 0,
                               reach + bias_top - jnp.min(m_ref[...]) > -SKIP_MARGIN)

    def scores(start, size, masked):
        rows = pl.ds(pl.multiple_of(start, t), size)
        kp = jnp.concatenate([k_ref[rows, :], aux_ref[rows, :]], axis=1)
        s_t = _dot_nt(kp, qp_ref[...])
        if masked:
            kr = lax.broadcasted_iota(I32, (size, t), 0)
            qc = lax.broadcasted_iota(I32, (size, t), 1)
            s_t = jnp.where(kr <= qc, s_t, MASK_VALUE)
        return s_t, rows

    def exact_block(start, masked):
        s_t, rows = scores(start, t, masked)
        m_old = m_ref[...]
        m_new = jnp.maximum(m_old, jnp.max(s_t, axis=0, keepdims=True))
        alpha = jnp.exp2(m_old - m_new)
        p = jnp.exp2(s_t - m_new)
        l_ref[...] = alpha * l_ref[...] + jnp.sum(p, axis=0, keepdims=True)
        pv = jnp.dot(vt_ref[:, rows], p.astype(BF16), preferred_element_type=F32)
        acc_ref[...] = alpha * acc_ref[...] + pv
        m_ref[...] = m_new

    def lagged_group(start, size):
        s_t, rows = scores(start, size, False)
        m_old = m_ref[...]
        p = jnp.exp2(s_t - m_old)
        pv = jnp.dot(vt_ref[:, rows], p.astype(BF16), preferred_element_type=F32)
        m_new = jnp.maximum(m_old, jnp.max(s_t, axis=0, keepdims=True))
        alpha = jnp.exp2(m_old - m_new)
        l_ref[...] = (l_ref[...] + jnp.sum(p, axis=0, keepdims=True)) * alpha
        acc_ref[...] = (acc_ref[...] + pv) * alpha
        m_ref[...] = m_new

    def reset():
        m_ref[...] = jnp.full_like(m_ref, MASK_VALUE)
        l_ref[...] = jnp.zeros_like(l_ref)
        acc_ref[...] = jnp.zeros_like(acc_ref)

    def finish():
        out = acc_ref[...] / l_ref[...]
        o_ref[...] = out.T.astype(o_ref.dtype)
        return out

    reset()
    exact_block(i * t, True)
    gb = ATTN_GROUP_BLOCKS
    n_groups = i // gb
    rest = i - n_groups * gb

    def group_cond(c):
        g, live = c
        return jnp.logical_and(g < n_groups, live)

    def group_body(c):
        g, _ = c
        first = i - (g + 1) * gb
        lagged_group(first * t, gb * t)
        return g + 1, alive(first - 1)

    _, live = lax.while_loop(group_cond, group_body, (0, alive(i - 1)))

    @pl.when(jnp.logical_and((rest & 2) != 0, live))
    def _():
        lagged_group((rest - 2) * t, 2 * t)

    @pl.when(jnp.logical_and((rest & 1) != 0, jnp.logical_and(live, alive(0))))
    def _():
        lagged_group(0, t)

    out = finish()
    overflowed = jnp.max(jnp.where(jnp.abs(out) < FINITE_LIMIT, 0.0, 1.0)) > 0.0

    @pl.when(overflowed)
    def _():
        reset()

        def exact_body(j, c):
            exact_block(j * t, False)
            return c

        lax.fori_loop(0, i, exact_body, 0)
        exact_block(i * t, True)
        finish()


def _attention(qg3, k3, aux, vt, block_end, block_knorm, n_heads):
    b, s, _ = k3.shape
    hd = k3.shape[2] // n_heads
    assert hd == LANES
    t = _tile(s, ATTN_BLOCK)
    by_head = lambda tab: tab[:, :, 0, :n_heads].transpose(0, 2, 1).reshape(b, n_heads, 1, s // t)
    end_tab = by_head(block_end)
    kmax_tab = jnp.sqrt(lax.cummax(by_head(block_knorm), axis=3))
    table = pl.BlockSpec((None, None, 1, s // t), lambda bi, h, i: (bi, h, 0, 0),
                         memory_space=pltpu.SMEM)
    return pl.pallas_call(
        _attn_kernel,
        out_shape=jax.ShapeDtypeStruct((b, s, n_heads * hd), BF16),
        grid=(b, n_heads, s // t),
        in_specs=[
            table,
            table,
            pl.BlockSpec((None, t, hd), lambda bi, h, i: (bi, i, h)),
            pl.BlockSpec((None, s, hd), lambda bi, h, i: (bi, 0, h)),
            pl.BlockSpec((None, None, s, LANES), lambda bi, h, i: (bi, h, 0, 0)),
            pl.BlockSpec((None, hd, s), lambda bi, h, i: (bi, h, 0)),
        ],
        out_specs=pl.BlockSpec((None, t, hd), lambda bi, h, i: (bi, i, h)),
        scratch_shapes=[
            pltpu.VMEM((t, hd + LANES), BF16),
            pltpu.VMEM((1, t), F32),
            pltpu.VMEM((1, t), F32),
            pltpu.VMEM((hd, t), F32),
        ],
        compiler_params=_cparams(("parallel", "parallel", "arbitrary"), 48),
        name="fox_attention",
    )(end_tab, kmax_tab, qg3, k3, aux, vt)


def _gated_out_kernel(o_ref, gate_ref, x_ref, w_ref, out_ref):
    g = o_ref[...].astype(F32) * jax.nn.sigmoid(gate_ref[...].astype(F32))
    out_ref[...] = x_ref[...] + jnp.dot(g.astype(BF16), w_ref[...], preferred_element_type=F32)


def _gated_out(o2, qg, x2, w_o):
    n, d = x2.shape
    hd_all = o2.shape[1]
    tm = _tile(n, 512)
    return pl.pallas_call(
        _gated_out_kernel,
        out_shape=jax.ShapeDtypeStruct((n, d), F32),
        grid=(n // tm,),
        in_specs=[
            pl.BlockSpec((tm, hd_all), lambda i: (i, 0)),
            pl.BlockSpec((tm, hd_all), lambda i: (i, 1)),
            pl.BlockSpec((tm, d), lambda i: (i, 0)),
            pl.BlockSpec((hd_all, d), lambda i: (0, 0)),
        ],
        out_specs=pl.BlockSpec((tm, d), lambda i: (i, 0)),
        compiler_params=_cparams(("parallel",), 48),
        name="gated_out_proj",
    )(o2, qg, x2, w_o)


def kernel(x, a_norm, a_w_in, a_conv_w, a_conv_b, a_w_rec, a_b_rec, a_w_inp, a_b_inp, a_lambda, a_w_out, kv_norm, kv_w, kv_b_forget, b_norm, b_w_qg, b_w_o, m_norm, m_w_group, m_b_group, m_w_router, m_b_router, m_w_in, m_w_out, final_norm):
    b, s, d = x.shape
    n = b * s
    depth = m_norm.shape[0]
    n_a = a_norm.shape[0]
    n_heads = kv_b_forget.shape[0]
    hd_all = b_w_o.shape[1]
    head_dim = hd_all // n_heads
    x2 = x.reshape(n, d)
    k3 = aux = vt = block_end = block_knorm = None
    for layer in range(depth):
        if layer < n_a:
            i = layer
            d_rnn = a_w_out.shape[1]
            proj = _norm_proj(x2, a_norm[i], a_w_in[i].astype(BF16), jnp.ones((2 * d_rnn,), F32),
                              BF16, "rglru_in_proj")
            x2 = _rglru(proj, x2.reshape(b, s, d), a_conv_w[i], a_conv_b[i], a_w_rec[i], a_b_rec[i],
                        a_w_inp[i], a_b_inp[i], a_lambda[i], a_w_out[i]).reshape(n, d)
        else:
            j = layer - n_a
            q_scale = jnp.concatenate([jnp.full((hd_all,), head_dim ** -0.5 * LOG2E, F32),
                                       jnp.ones((hd_all,), F32)])
            qg = _norm_proj(x2, b_norm[j], b_w_qg[j].astype(BF16), q_scale, BF16, "fox_qg_proj")
            o = _attention(qg.reshape(b, s, 2 * hd_all), k3, aux, vt, block_end, block_knorm,
                           n_heads)
            x2 = _gated_out(o.reshape(n, hd_all), qg, x2, b_w_o[j].astype(BF16))
        last = layer == depth - 1
        x2 = _moe(x2, m_norm[layer], m_w_group[layer], m_b_group[layer], m_w_router[layer],
                  m_b_router[layer], m_w_in, m_w_out, layer, final_norm, last)
        if layer == n_a - 1:
            k3, vt, aux, block_end, block_knorm = _shared_kv(
                x2.reshape(b, s, d), kv_norm, kv_w[:, :hd_all].astype(BF16),
                kv_w[:, hd_all:2 * hd_all].T.astype(BF16), kv_w[:, 2 * hd_all:], kv_b_forget)
    if depth == 0:
        x2 = _rms(x2, final_norm)
    return x2.reshape(b, s, d)
```

```python
import functools
import math

import jax
import jax.numpy as jnp
from jax import lax
from jax.experimental import pallas as pl
from jax.experimental.pallas import tpu as pltpu

F32 = jnp.float32
BF16 = jnp.bfloat16
I32 = jnp.int32

EPS = 1e-6
LRU_C = 8.0
TOP_K = 2
LOG2E = 1.4426950408889634
MASK_VALUE = -1e30

V7X_VMEM_BYTES = 64 * 1024 * 1024
SUBLANES = 8
LANES = 128
MIB = 1024 * 1024


def _cparams(semantics, vmem_mib):
    assert vmem_mib * MIB < V7X_VMEM_BYTES
    return pltpu.CompilerParams(dimension_semantics=semantics, vmem_limit_bytes=vmem_mib * MIB)


def _tile(dim, pref):
    t = min(dim, pref)
    assert dim % t == 0, (dim, pref)
    return t


def _rms(x, gain):
    return x * lax.rsqrt(jnp.mean(x * x, axis=-1, keepdims=True) + EPS) * gain


def _sigmoid(x):
    return 0.5 * jnp.tanh(0.5 * x) + 0.5


def _gelu_tanh(y):
    return 0.5 * y * (1.0 + jnp.tanh(math.sqrt(2.0 / math.pi) * (y + 0.044715 * (y * y * y))))


def _norm_proj_kernel(x_ref, g_ref, w_ref, s_ref, o_ref, hn_ref, *, gelu_from_tile):
    j = pl.program_id(1)

    @pl.when(j == 0)
    def _():
        hn_ref[...] = _rms(x_ref[...], g_ref[...]).astype(BF16)

    acc = jnp.dot(hn_ref[...], w_ref[...], preferred_element_type=F32) * s_ref[...]
    if gelu_from_tile is None:
        o_ref[...] = acc.astype(o_ref.dtype)
    else:
        @pl.when(j < gelu_from_tile)
        def _():
            o_ref[...] = acc.astype(o_ref.dtype)

        @pl.when(j >= gelu_from_tile)
        def _():
            o_ref[...] = _gelu_tanh(acc).astype(o_ref.dtype)


def _norm_proj(x, gain, w, col_scale, out_dtype, name, gelu_from_col=None):
    n, d = x.shape
    n_out = w.shape[1]
    tm = _tile(n, 1024)
    tn = _tile(n_out, 1024)
    gelu_from_tile = None
    if gelu_from_col is not None:
        tn = math.gcd(tn, gelu_from_col)
        gelu_from_tile = gelu_from_col // tn
    return pl.pallas_call(
        functools.partial(_norm_proj_kernel, gelu_from_tile=gelu_from_tile),
        out_shape=jax.ShapeDtypeStruct((n, n_out), out_dtype),
        grid=(n // tm, n_out // tn),
        in_specs=[
            pl.BlockSpec((tm, d), lambda i, j: (i, 0)),
            pl.BlockSpec((1, d), lambda i, j: (0, 0)),
            pl.BlockSpec((d, tn), lambda i, j: (0, j)),
            pl.BlockSpec((1, tn), lambda i, j: (0, j)),
        ],
        out_specs=pl.BlockSpec((tm, tn), lambda i, j: (i, j)),
        scratch_shapes=[pltpu.VMEM((tm, d), BF16)],
        compiler_params=_cparams(("parallel", "arbitrary"), 48),
        name=name,
    )(x, gain.reshape(1, d), w, col_scale.reshape(1, n_out))


SCAN_STRIP = 1024


def _rglru_kernel(u_ref, y_ref, x_ref, cw_ref, cb_ref, wr_ref, br_ref, wi_ref, bi_ref,
                  lam_ref, wo_ref, o_ref, ubuf_ref, a_ref, b_ref, h_ref, *, conv_width):
    ts, d = u_ref.shape
    nb, blk, _ = wr_ref.shape
    halo = SUBLANES

    @pl.when(pl.program_id(1) == 0)
    def _():
        ubuf_ref[0:halo, :] = jnp.zeros((halo, d), F32)
        h_ref[...] = jnp.zeros_like(h_ref)

    u = u_ref[...].astype(F32)
    ubuf_ref[halo:halo + ts, :] = u
    uc = u * cw_ref[conv_width - 1:conv_width, :] + cb_ref[...]
    for j in range(conv_width - 1):
        shift = conv_width - 1 - j
        uc = uc + ubuf_ref[halo - shift:halo - shift + ts, :] * cw_ref[j:j + 1, :]
    ubuf_ref[0:halo, :] = u[ts - halo:, :]

    ub = uc.astype(BF16)
    r_parts, i_parts = [], []
    for n in range(nb):
        ubn = ub[:, n * blk:(n + 1) * blk]
        r_parts.append(jnp.dot(ubn, wr_ref[n], preferred_element_type=F32))
        i_parts.append(jnp.dot(ubn, wi_ref[n], preferred_element_type=F32))
    r = _sigmoid(jnp.concatenate(r_parts, axis=1) + br_ref[...])
    gi = _sigmoid(jnp.concatenate(i_parts, axis=1) + bi_ref[...])
    lam = lam_ref[...]
    sp = jnp.maximum(-lam, 0.0) + jnp.log(1.0 + jnp.exp(-jnp.abs(lam)))
    log_a = (-LRU_C * r) * sp
    th = jnp.tanh(0.5 * log_a)
    inv = 1.0 / (1.0 - th)
    a_ref[...] = (1.0 + th) * inv
    b_ref[...] = (2.0 * inv) * jnp.sqrt(-th) * (gi * uc)

    row = lax.broadcasted_iota(I32, (SUBLANES, SCAN_STRIP), 0)
    strip = min(SCAN_STRIP, d)
    for c in range(d // strip):
        cols = pl.ds(c * strip, strip)

        def tile_step(t, h):
            rows = pl.ds(pl.multiple_of(t * SUBLANES, SUBLANES), SUBLANES)
            a = a_ref[rows, cols]
            bb = b_ref[rows, cols]
            for sh in (1, 2, 4):
                keep = row[:, :strip] >= sh
                a_prev = jnp.where(keep, pltpu.roll(a, sh, 0), 1.0)
                b_prev = jnp.where(keep, pltpu.roll(bb, sh, 0), 0.0)
                bb = a * b_prev + bb
                a = a * a_prev
            hs = a * h + bb
            b_ref[rows, cols] = hs
            return jnp.broadcast_to(hs[SUBLANES - 1:SUBLANES, :], (SUBLANES, strip))

        h_ref[:, cols] = lax.fori_loop(0, ts // SUBLANES, tile_step, h_ref[:, cols], unroll=2)

    g = (b_ref[...] * y_ref[...].astype(F32)).astype(BF16)
    o_ref[...] = x_ref[...] + jnp.dot(g, wo_ref[...], preferred_element_type=F32)


def _rglru(proj, x3, conv_w, conv_b, w_rec, b_rec, w_inp, b_inp, lam, w_out):
    b, s, d = x3.shape
    d_rnn = w_out.shape[0]
    nb, blk, _ = w_rec.shape
    width = conv_w.shape[0]
    assert width - 1 <= SUBLANES
    ts = _tile(s, 256)
    proj3 = proj.reshape(b, s, 2 * d_rnn)
    row = lambda v: v.reshape(1, -1)
    const2 = lambda bi, i: (0, 0)
    const3 = lambda bi, i: (0, 0, 0)
    return pl.pallas_call(
        functools.partial(_rglru_kernel, conv_width=width),
        out_shape=jax.ShapeDtypeStruct((b, s, d), F32),
        grid=(b, s // ts),
        in_specs=[
            pl.BlockSpec((None, ts, d_rnn), lambda bi, i: (bi, i, 0)),
            pl.BlockSpec((None, ts, d_rnn), lambda bi, i: (bi, i, 1)),
            pl.BlockSpec((None, ts, d), lambda bi, i: (bi, i, 0)),
            pl.BlockSpec((width, d_rnn), const2),
            pl.BlockSpec((1, d_rnn), const2),
            pl.BlockSpec((nb, blk, blk), const3),
            pl.BlockSpec((1, d_rnn), const2),
            pl.BlockSpec((nb, blk, blk), const3),
            pl.BlockSpec((1, d_rnn), const2),
            pl.BlockSpec((1, d_rnn), const2),
            pl.BlockSpec((d_rnn, d), const2),
        ],
        out_specs=pl.BlockSpec((None, ts, d), lambda bi, i: (bi, i, 0)),
        scratch_shapes=[
            pltpu.VMEM((SUBLANES + ts, d_rnn), F32),
            pltpu.VMEM((ts, d_rnn), F32),
            pltpu.VMEM((ts, d_rnn), F32),
            pltpu.VMEM((SUBLANES, d_rnn), F32),
        ],
        compiler_params=_cparams(("parallel", "arbitrary"), 56),
        name="rglru",
    )(proj3, proj3, x3, conv_w, row(conv_b), w_rec.astype(BF16), row(b_rec),
      w_inp.astype(BF16), row(b_inp), row(lam), w_out.astype(BF16))


def _pack_bf16_pair(x):
    c = x.shape[1] // 2
    lo = lax.bitcast_convert_type(x[:, :c].astype(BF16).astype(F32), I32)
    hi = lax.bitcast_convert_type(x[:, c:].astype(BF16).astype(F32), I32)
    return jnp.bitwise_or(hi, lax.shift_right_logical(lo, jnp.int32(16)))


def _unpack_bf16_pair(w):
    lo = lax.bitcast_convert_type(lax.shift_left(w, jnp.int32(16)), F32)
    hi = lax.bitcast_convert_type(jnp.bitwise_and(w, jnp.int32(-65536)), F32)
    return lo, hi


def _split3(v):
    v1 = v.astype(BF16)
    r1 = v - v1.astype(F32)
    v2 = r1.astype(BF16)
    v3 = (r1 - v2.astype(F32)).astype(BF16)
    return v1, v2, v3


def _dot_nt(a, b):
    return lax.dot_general(a, b, (((1,), (1,)), ((), ())), preferred_element_type=F32)


def _first_argmax(v, n):
    idx = lax.broadcasted_iota(I32, v.shape, 0)
    vmax = jnp.max(v, axis=0, keepdims=True)
    amax = jnp.min(jnp.where(v == vmax, idx, n), axis=0, keepdims=True)
    return amax, vmax


def _router_kernel(x_ref, g_ref, w1_ref, w2_ref, bias_ref, hn_ref, eid_ref, wt_ref,
                   rank_ref, cnt_ref, tri_ref, carry_ref, *, n_groups, per_group):
    tm = x_ref.shape[0]
    n_exp = n_groups * per_group

    @pl.when(pl.program_id(0) == 0)
    def _():
        r = lax.broadcasted_iota(I32, (tm, tm), 0)
        c = lax.broadcasted_iota(I32, (tm, tm), 1)
        tri_ref[...] = (r < c).astype(BF16)
        carry_ref[...] = jnp.zeros_like(carry_ref)

    hn = _rms(x_ref[...], g_ref[...])
    hn_ref[...] = _pack_bf16_pair(hn)
    h1, h2, _ = _split3(hn)
    w1, w2 = w1_ref[...], w2_ref[...]
    logits = _dot_nt(w1, h1) + (_dot_nt(w1, h2) + _dot_nt(w2, h1))
    logits = logits + bias_ref[:, 0:1]

    gl = logits[0:n_groups, :]
    g_idx, g_max = _first_argmax(gl, n_groups)
    gp_top = 1.0 / jnp.sum(jnp.exp(gl - g_max), axis=0, keepdims=True)
    el = jnp.zeros((per_group, tm), F32)
    for g in range(n_groups):
        lo = n_groups + g * per_group
        el = jnp.where(g_idx == g, logits[lo:lo + per_group, :], el)
    e_max = jnp.max(el, axis=0, keepdims=True)
    ex = jnp.exp(el - e_max)
    ep = ex / jnp.sum(ex, axis=0, keepdims=True)
    e1, p1 = _first_argmax(ep, per_group)
    sub = lax.broadcasted_iota(I32, ep.shape, 0)
    e2, p2 = _first_argmax(jnp.where(sub == e1, -1.0, ep), per_group)
    denom = p1 + p2
    eid1 = g_idx * per_group + e1
    eid2 = g_idx * per_group + e2
    eid_ref[...] = jnp.concatenate([eid1, eid2], axis=0)
    wt_ref[...] = jnp.concatenate([gp_top * p1 / denom, gp_top * p2 / denom], axis=0)

    e_iota = lax.broadcasted_iota(I32, (n_exp, tm), 0)
    hot1 = e_iota == eid1
    hot2 = e_iota == eid2
    chosen = jnp.logical_or(hot1, hot2)
    before = jnp.dot(chosen.astype(BF16), tri_ref[...], preferred_element_type=F32)
    base = (before + carry_ref[:, 0:1]).astype(I32)
    rank1 = jnp.sum(jnp.where(hot1, base, 0), axis=0, keepdims=True)
    rank2 = jnp.sum(jnp.where(hot2, base, 0), axis=0, keepdims=True)
    rank_ref[...] = jnp.concatenate([rank1, rank2], axis=0)
    carry_ref[...] = carry_ref[...] + jnp.sum(chosen.astype(F32), axis=1, keepdims=True)
    cnt_ref[...] = carry_ref[...].astype(I32)


def _router(x2, gain, w_group, b_group, w_router, b_router):
    n, d = x2.shape
    n_groups = w_group.shape[1]
    n_exp = w_router.shape[1]
    per_group = n_exp // n_groups
    assert per_group == SUBLANES and n_groups == SUBLANES
    rows = n_groups + n_exp
    rows_p = -(-rows // LANES) * LANES
    w_t = jnp.concatenate([w_group, w_router], axis=1).T
    w_t = jnp.pad(w_t, ((0, rows_p - rows), (0, 0)))
    w1, w2, _ = _split3(w_t)
    bias = jnp.pad(jnp.concatenate([b_group, b_router]), (0, rows_p - rows))
    bias = jnp.broadcast_to(bias[:, None], (rows_p, LANES))
    tm = _tile(n, 512)
    const = lambda i: (0, 0)
    return pl.pallas_call(
        functools.partial(_router_kernel, n_groups=n_groups, per_group=per_group),
        out_shape=(
            jax.ShapeDtypeStruct((n, d // 2), I32),
            jax.ShapeDtypeStruct((TOP_K, n), I32),
            jax.ShapeDtypeStruct((TOP_K, n), F32),
            jax.ShapeDtypeStruct((TOP_K, n), I32),
            jax.ShapeDtypeStruct((n_exp, LANES), I32),
        ),
        grid=(n // tm,),
        in_specs=[
            pl.BlockSpec((tm, d), lambda i: (i, 0)),
            pl.BlockSpec((1, d), const),
            pl.BlockSpec((rows_p, d), const),
            pl.BlockSpec((rows_p, d), const),
            pl.BlockSpec((rows_p, LANES), const),
        ],
        out_specs=(
            pl.BlockSpec((tm, d // 2), lambda i: (i, 0)),
            pl.BlockSpec((TOP_K, tm), lambda i: (0, i)),
            pl.BlockSpec((TOP_K, tm), lambda i: (0, i)),
            pl.BlockSpec((TOP_K, tm), lambda i: (0, i)),
            pl.BlockSpec((n_exp, LANES), const),
        ),
        scratch_shapes=[pltpu.VMEM((tm, tm), BF16), pltpu.VMEM((n_exp, LANES), F32)],
        compiler_params=_cparams(("arbitrary",), 40),
        name="moe_router",
    )(x2, gain.reshape(1, d), w1, w2, bias)


def _dest_kernel(eid_ref, rank_ref, start_ref, dest_ref):
    eid = eid_ref[...]
    n_exp = start_ref.shape[0]
    dest = rank_ref[...]
    for k in range(eid.shape[0]):
        hot = lax.broadcasted_iota(I32, (n_exp, eid.shape[1]), 0) == eid[k:k + 1, :]
        off = jnp.sum(jnp.where(hot, start_ref[:, 0:1], 0), axis=0, keepdims=True)
        dest_ref[k:k + 1, :] = dest[k:k + 1, :] + off


def _dest_rows(eid, rank, seg_start):
    k, n = eid.shape
    n_exp = seg_start.shape[0]
    tm = _tile(n, 2048)
    start = jnp.broadcast_to(seg_start[:, None], (n_exp, LANES)).astype(I32)
    return pl.pallas_call(
        _dest_kernel,
        out_shape=jax.ShapeDtypeStruct((k, n), I32),
        grid=(n // tm,),
        in_specs=[
            pl.BlockSpec((k, tm), lambda i: (0, i)),
            pl.BlockSpec((k, tm), lambda i: (0, i)),
            pl.BlockSpec((n_exp, LANES), lambda i: (0, 0)),
        ],
        out_specs=pl.BlockSpec((k, tm), lambda i: (0, i)),
        compiler_params=_cparams(("parallel",), 16),
        name="moe_dest",
    )(eid, rank, start)


def _dispatch_kernel(nv_ref, dest_ref, hn_ref, xs_ref, zero_ref, sem, zsem, *, chunk_rows):
    k, tm = dest_ref.shape[1], dest_ref.shape[2]
    n_chunks = nv_ref.shape[0]

    @pl.when(pl.program_id(0) == 0)
    def _():
        zero_ref[...] = jnp.zeros_like(zero_ref)

        def zero_copy(c):
            rows = pl.ds(pl.multiple_of(c * chunk_rows, chunk_rows), chunk_rows)
            return pltpu.make_async_copy(zero_ref, xs_ref.at[rows], zsem)

        def start(c, carry):
            @pl.when(nv_ref[c] < chunk_rows)
            def _():
                zero_copy(c).start()
            return carry

        def finish(c, carry):
            @pl.when(nv_ref[c] < chunk_rows)
            def _():
                zero_copy(c).wait()
            return carry

        lax.fori_loop(0, n_chunks, start, 0)
        lax.fori_loop(0, n_chunks, finish, 0)


    def issue(t, c):
        for kk in range(k):
            pltpu.make_async_copy(hn_ref.at[pl.ds(t, 1)],
                                  xs_ref.at[pl.ds(dest_ref[0, kk, t], 1)], sem).start()
        return c

    lax.fori_loop(0, tm, issue, 0, unroll=8)
    for kk in range(k):
        pltpu.make_async_copy(hn_ref, xs_ref.at[pl.ds(0, tm)], sem).wait()


def _dispatch(dest, hn, n_valid, n_rows, chunk_rows):
    k, n = dest.shape
    d = hn.shape[1]
    tm = _tile(n, 512)
    dest3 = dest.reshape(k, n // tm, tm).transpose(1, 0, 2)
    grid_spec = pltpu.PrefetchScalarGridSpec(
        num_scalar_prefetch=1,
        grid=(n // tm,),
        in_specs=[
            pl.BlockSpec((1, k, tm), lambda i, nv: (i, 0, 0), memory_space=pltpu.SMEM),
            pl.BlockSpec((tm, d), lambda i, nv: (i, 0)),
        ],
        out_specs=pl.BlockSpec(memory_space=pl.ANY),
        scratch_shapes=[pltpu.VMEM((chunk_rows, d), hn.dtype), pltpu.SemaphoreType.DMA(()),
                        pltpu.SemaphoreType.DMA(())],
    )
    return pl.pallas_call(
        functools.partial(_dispatch_kernel, chunk_rows=chunk_rows),
        out_shape=jax.ShapeDtypeStruct((n_rows, d), hn.dtype),
        grid_spec=grid_spec,
        compiler_params=_cparams(("arbitrary",), 24),
        name="moe_dispatch",
    )(n_valid, dest3, hn)


def _expert_kernel(ce_ref, nu_ref, fresh_ref, next_ref, slot_ref, xs_ref, wi_hbm, wo_hbm, ys_ref,
                   wif_ref, wof_ref, wib_ref, wob_ref, sem_ref, *, d_expert, layer):
    i = pl.program_id(0)

    def weight_copies(e, s):
        return (pltpu.make_async_copy(wi_hbm.at[layer, e], wif_ref.at[s], sem_ref.at[0, s]),
                pltpu.make_async_copy(wo_hbm.at[layer, e], wof_ref.at[s], sem_ref.at[1, s]))

    @pl.when(i < nu_ref[0])
    def _():
        @pl.when(fresh_ref[i] == 1)
        def _():
            s = slot_ref[i]

            @pl.when(i == 0)
            def _():
                for c in weight_copies(ce_ref[i], s):
                    c.start()

            @pl.when(next_ref[i] >= 0)
            def _():
                for c in weight_copies(next_ref[i], 1 - s):
                    c.start()

            for c in weight_copies(ce_ref[i], s):
                c.wait()
            wib_ref[...] = wif_ref[s].astype(BF16)
            wob_ref[...] = wof_ref[s].astype(BF16)

        x_lo, x_hi = _unpack_bf16_pair(xs_ref[...])
        xb = jnp.concatenate([x_lo, x_hi], axis=1).astype(BF16)
        gu = jnp.dot(xb, wib_ref[...], preferred_element_type=F32)
        gate = gu[:, :d_expert]
        act = (gate * jax.nn.sigmoid(gate) * gu[:, d_expert:]).astype(BF16)
        ys_ref[...] = _pack_bf16_pair(jnp.dot(act, wob_ref[...], preferred_element_type=F32))

    @pl.when(i >= nu_ref[0])
    def _():
        ys_ref[...] = jnp.zeros_like(ys_ref)


def _experts(xs, chunk_e, n_used, fresh, next_e, slot, w_in_all, w_out_all, layer, chunk_rows):
    n_rows, d_packed = xs.shape
    d = 2 * d_packed
    d_expert = w_out_all.shape[2]
    n_chunks = n_rows // chunk_rows
    xs_map = lambda i, ce, nu, fr, nx, sl: (jnp.minimum(i, jnp.maximum(nu[0] - 1, 0)), 0)
    grid_spec = pltpu.PrefetchScalarGridSpec(
        num_scalar_prefetch=5,
        grid=(n_chunks,),
        in_specs=[
            pl.BlockSpec((chunk_rows, d_packed), xs_map),
            pl.BlockSpec(memory_space=pl.ANY),
            pl.BlockSpec(memory_space=pl.ANY),
        ],
        out_specs=pl.BlockSpec((chunk_rows, d_packed), lambda i, ce, nu, fr, nx, sl: (i, 0)),
        scratch_shapes=[
            pltpu.VMEM((2, d, 2 * d_expert), F32),
            pltpu.VMEM((2, d_expert, d), F32),
            pltpu.VMEM((d, 2 * d_expert), BF16),
            pltpu.VMEM((d_expert, d), BF16),
            pltpu.SemaphoreType.DMA((2, 2)),
        ],
    )
    return pl.pallas_call(
        functools.partial(_expert_kernel, d_expert=d_expert, layer=layer),
        out_shape=jax.ShapeDtypeStruct((n_rows, d_packed), I32),
        grid_spec=grid_spec,
        compiler_params=_cparams(("arbitrary",), 56),
        name="moe_experts",
    )(chunk_e, n_used, fresh, next_e, slot, xs, w_in_all, w_out_all)


def _combine_kernel(dest_ref, next_ref, x_ref, wt_ref, g_ref, ys_ref, o_ref, buf_ref, sem_ref, *,
                    final_norm, n_blocks):
    k, tm = dest_ref.shape[1], dest_ref.shape[2]
    i = pl.program_id(0)

    def issue_block(d_ref, slot):
        def body(t, c):
            for kk in range(k):
                pltpu.make_async_copy(ys_ref.at[pl.ds(d_ref[0, kk, t], 1)],
                                      buf_ref.at[slot, kk, pl.ds(t, 1)], sem_ref.at[slot]).start()
            return c
        lax.fori_loop(0, tm, body, 0, unroll=8)

    @pl.when(i == 0)
    def _():
        issue_block(dest_ref, 0)

    @pl.when(i + 1 < n_blocks)
    def _():
        issue_block(next_ref, (i + 1) % 2)

    slot = i % 2
    for kk in range(k):
        pltpu.make_async_copy(ys_ref.at[pl.ds(0, tm)], buf_ref.at[slot, kk], sem_ref.at[slot]).wait()

    c = buf_ref.shape[3]
    out_lo = x_ref[:, :c]
    out_hi = x_ref[:, c:]
    for kk in range(k):
        y_lo, y_hi = _unpack_bf16_pair(buf_ref[slot, kk])
        w = wt_ref[:, kk:kk + 1]
        out_lo = out_lo + w * y_lo
        out_hi = out_hi + w * y_hi
    if final_norm:
        ms = (jnp.sum(out_lo * out_lo, axis=-1, keepdims=True)
              + jnp.sum(out_hi * out_hi, axis=-1, keepdims=True)) / (2 * c)
        scale = lax.rsqrt(ms + EPS)
        out_lo = out_lo * scale * g_ref[:, :c]
        out_hi = out_hi * scale * g_ref[:, c:]
    o_ref[:, :c] = out_lo
    o_ref[:, c:] = out_hi


def _combine(dest, x2, wts, ys, gain, final_norm):
    k, n = dest.shape
    d = x2.shape[1]
    tm = _tile(n, 512)
    n_blocks = n // tm
    dest3 = dest.reshape(k, n_blocks, tm).transpose(1, 0, 2)
    return pl.pallas_call(
        functools.partial(_combine_kernel, final_norm=final_norm, n_blocks=n_blocks),
        out_shape=jax.ShapeDtypeStruct((n, d), F32),
        grid=(n_blocks,),
        in_specs=[
            pl.BlockSpec((1, k, tm), lambda i: (i, 0, 0), memory_space=pltpu.SMEM),
            pl.BlockSpec((1, k, tm), lambda i: (jnp.minimum(i + 1, n_blocks - 1), 0, 0),
                         memory_space=pltpu.SMEM),
            pl.BlockSpec((tm, d), lambda i: (i, 0)),
            pl.BlockSpec((tm, k), lambda i: (i, 0)),
            pl.BlockSpec((1, d), lambda i: (0, 0)),
            pl.BlockSpec(memory_space=pl.ANY),
        ],
        out_specs=pl.BlockSpec((tm, d), lambda i: (i, 0)),
        scratch_shapes=[pltpu.VMEM((2, k, tm, d // 2), I32), pltpu.SemaphoreType.DMA((2,))],
        compiler_params=_cparams(("arbitrary",), 40),
        name="moe_combine",
    )(dest3, dest3, x2, wts.T, gain.reshape(1, d), ys)


EXPERT_CHUNK_ROWS = 256


def _moe(x2, norm_gain, w_group, b_group, w_router, b_router, w_in_all, w_out_all, layer, out_gain,
         final_norm):
    n, d = x2.shape
    n_exp = w_router.shape[1]
    chunk = EXPERT_CHUNK_ROWS
    hn, eid, wts, rank, counts = _router(x2, norm_gain, w_group, b_group, w_router, b_router)
    counts = counts[:, 0]
    padded = (counts + chunk - 1) // chunk * chunk
    seg_end = jnp.cumsum(padded)
    seg_start = seg_end - padded
    n_rows = (-(-(n * TOP_K) // chunk) + n_exp) * chunk
    n_chunks = n_rows // chunk
    chunk_start = jnp.arange(n_chunks, dtype=I32) * chunk
    n_used = (seg_end[-1:] // chunk).astype(I32)
    live_start = jnp.minimum(chunk_start, jnp.maximum(seg_end[-1] - chunk, 0))
    chunk_e = jnp.sum(seg_end[None, :] <= live_start[:, None], axis=1).astype(I32)
    chunk_e = jnp.minimum(chunk_e, n_exp - 1)
    mine = chunk_e[:, None] == jnp.arange(n_exp, dtype=I32)[None, :]
    left = jnp.sum(jnp.where(mine, (counts + seg_start)[None, :], 0), axis=1) - chunk_start
    n_valid = jnp.clip(left, 0, chunk).astype(I32)
    fresh = jnp.concatenate([jnp.ones((1,), I32), (chunk_e[1:] != chunk_e[:-1]).astype(I32)])
    dest = _dest_rows(eid, rank, seg_start)
    xs = _dispatch(dest, hn, n_valid, n_rows, chunk)
    idx = jnp.arange(n_chunks, dtype=I32)
    slot = ((jnp.cumsum(fresh) - 1) % 2).astype(I32)
    starts = jnp.where(jnp.logical_and(fresh == 1, idx < n_used[0]), idx, n_chunks)
    next_start = jnp.concatenate([lax.cummin(starts[::-1])[::-1][1:],
                                  jnp.full((1,), n_chunks, I32)])
    next_e = jnp.sum(jnp.where(idx[None, :] == next_start[:, None], chunk_e[None, :] + 1, 0),
                     axis=1).astype(I32) - 1
    ys = _experts(xs, chunk_e, n_used, fresh, next_e, slot, w_in_all, w_out_all, layer, chunk)
    return _combine(dest, x2, wts, ys, out_gain, final_norm)


def _shared_kv_kernel(x_ref, g_ref, wk_ref, wvt_ref, w_ref, b_ref, k_ref, vt_ref, aux_ref, end_ref,
                      kn_ref, tri_ref, carry_ref, *, n_heads):
    ts = x_ref.shape[0]

    @pl.when(pl.program_id(1) == 0)
    def _():
        r = lax.broadcasted_iota(I32, (ts, ts), 0)
        c = lax.broadcasted_iota(I32, (ts, ts), 1)
        tri_ref[...] = (c <= r).astype(BF16)
        carry_ref[...] = jnp.zeros_like(carry_ref)

    hn = _rms(x_ref[...], g_ref[...]).astype(BF16)
    kb = jnp.dot(hn, wk_ref[...], preferred_element_type=F32).astype(k_ref.dtype)
    k_ref[...] = kb
    sq = kb.astype(F32) * kb.astype(F32)
    hd = kb.shape[1] // n_heads
    tile_lane = lax.broadcasted_iota(I32, (SUBLANES, LANES), 1)
    norms = jnp.zeros((SUBLANES, LANES), F32)
    for h in range(n_heads):
        row_sq = jnp.sum(sq[:, h * hd:(h + 1) * hd], axis=1, keepdims=True)
        norms = jnp.where(tile_lane == h, jnp.max(row_sq, axis=0, keepdims=True), norms)
    kn_ref[...] = norms
    vt_ref[...] = _dot_nt(wvt_ref[...], hn).astype(vt_ref.dtype)
    f = jnp.dot(hn, w_ref[...], preferred_element_type=F32) + b_ref[...]
    logf = jnp.minimum(f, 0.0) - jnp.log(1.0 + jnp.exp(-jnp.abs(f)))
    l1, l2, l3 = _split3(logf)
    tri = tri_ref[...]
    cum = (jnp.dot(tri, l1, preferred_element_type=F32)
           + jnp.dot(tri, l2, preferred_element_type=F32)
           + jnp.dot(tri, l3, preferred_element_type=F32)) + carry_ref[0:1, :]
    carry_ref[...] = jnp.broadcast_to(cum[ts - 1:ts, :], carry_ref.shape)
    end_ref[...] = carry_ref[...] * (-LOG2E)
    c1, c2, c3 = [c.astype(F32) for c in _split3(cum * (-LOG2E))]
    lane = lax.broadcasted_iota(I32, (ts, LANES), 1)
    for h in range(n_heads):
        col = lambda v: jnp.broadcast_to(v[:, h:h + 1], (ts, LANES))
        aux = jnp.where(lane == 0, col(c1), jnp.where(lane == 1, col(c2),
                        jnp.where(lane == 2, col(c3), 0.0)))
        aux_ref[h] = aux.astype(BF16)


def _shared_kv(x3, gain, w_k, w_vt, w_f, b_f):
    b, s, d = x3.shape
    hd_all = w_k.shape[1]
    n_heads = w_f.shape[1]
    assert n_heads <= LANES
    w_p = jnp.pad(w_f, ((0, 0), (0, LANES - n_heads))).astype(BF16)
    b_p = jnp.pad(b_f, (0, LANES - n_heads)).reshape(1, LANES)
    ts = _tile(s, ATTN_BLOCK)
    const = lambda bi, i: (0, 0)
    return pl.pallas_call(
        functools.partial(_shared_kv_kernel, n_heads=n_heads),
        out_shape=(jax.ShapeDtypeStruct((b, s, hd_all), BF16),
                   jax.ShapeDtypeStruct((b, hd_all, s), BF16),
                   jax.ShapeDtypeStruct((b, n_heads, s, LANES), BF16),
                   jax.ShapeDtypeStruct((b, s // ts, SUBLANES, LANES), F32),
                   jax.ShapeDtypeStruct((b, s // ts, SUBLANES, LANES), F32)),
        grid=(b, s // ts),
        in_specs=[
            pl.BlockSpec((None, ts, d), lambda bi, i: (bi, i, 0)),
            pl.BlockSpec((1, d), const),
            pl.BlockSpec((d, hd_all), const, pipeline_mode=pl.Buffered(1)),
            pl.BlockSpec((hd_all, d), const, pipeline_mode=pl.Buffered(1)),
            pl.BlockSpec((d, LANES), const),
            pl.BlockSpec((1, LANES), const),
        ],
        out_specs=(pl.BlockSpec((None, ts, hd_all), lambda bi, i: (bi, i, 0)),
                   pl.BlockSpec((None, hd_all, ts), lambda bi, i: (bi, 0, i)),
                   pl.BlockSpec((None, n_heads, ts, LANES), lambda bi, i: (bi, 0, i, 0)),
                   pl.BlockSpec((None, None, SUBLANES, LANES), lambda bi, i: (bi, i, 0, 0)),
                   pl.BlockSpec((None, None, SUBLANES, LANES), lambda bi, i: (bi, i, 0, 0))),
        scratch_shapes=[pltpu.VMEM((ts, ts), BF16), pltpu.VMEM((SUBLANES, LANES), F32)],
        compiler_params=_cparams(("parallel", "arbitrary"), 56),
        name="shared_kv",
    )(x3, gain.reshape(1, d), w_k, w_vt, w_p, b_p)


ATTN_BLOCK = 512
ATTN_GROUP_BLOCKS = 4
FINITE_LIMIT = 3.0e38
SKIP_MARGIN = 160.0


def _attn_kernel(end_ref, kmax_ref, q_ref, k_ref, aux_ref, vt_ref, o_ref, qp_ref, m_ref, l_ref,
                 acc_ref):
    t, hd = q_ref.shape
    i = pl.program_id(2)
    lane = lax.broadcasted_iota(I32, (t, LANES), 1)
    qp_ref[:, 0:hd] = q_ref[...]
    qp_ref[:, hd:hd + LANES] = jnp.where(lane < 3, 1.0, 0.0).astype(BF16)

    qf = q_ref[...].astype(F32)
    reach = jnp.sqrt(jnp.max(jnp.sum(qf * qf, axis=1, keepdims=True))) * kmax_ref[0, i]

    def alive(block):
        bias_top = end_ref[0, jnp.maximum(block, 0)]
        return jnp.logical_and(block >= 0,
                               reach + bias_top - jnp.min(m_ref[...]) > -SKIP_MARGIN)

    def scores(start, size, masked):
        rows = pl.ds(pl.multiple_of(start, t), size)
        kp = jnp.concatenate([k_ref[rows, :], aux_ref[rows, :]], axis=1)
        s_t = _dot_nt(kp, qp_ref[...])
        if masked:
            kr = lax.broadcasted_iota(I32, (size, t), 0)
            qc = lax.broadcasted_iota(I32, (size, t), 1)
            s_t = jnp.where(kr <= qc, s_t, MASK_VALUE)
        return s_t, rows

    def exact_block(start, masked):
        s_t, rows = scores(start, t, masked)
        m_old = m_ref[...]
        m_new = jnp.maximum(m_old, jnp.max(s_t, axis=0, keepdims=True))
        alpha = jnp.exp2(m_old - m_new)
        p = jnp.exp2(s_t - m_new)
        l_ref[...] = alpha * l_ref[...] + jnp.sum(p, axis=0, keepdims=True)
        pv = jnp.dot(vt_ref[:, rows], p.astype(BF16), preferred_element_type=F32)
        acc_ref[...] = alpha * acc_ref[...] + pv
        m_ref[...] = m_new

    def lagged_group(start, size):
        s_t, rows = scores(start, size, False)
        m_old = m_ref[...]
        p = jnp.exp2(s_t - m_old)
        pv = jnp.dot(vt_ref[:, rows], p.astype(BF16), preferred_element_type=F32)
        m_new = jnp.maximum(m_old, jnp.max(s_t, axis=0, keepdims=True))
        alpha = jnp.exp2(m_old - m_new)
        l_ref[...] = (l_ref[...] + jnp.sum(p, axis=0, keepdims=True)) * alpha
        acc_ref[...] = (acc_ref[...] + pv) * alpha
        m_ref[...] = m_new

    def reset():
        m_ref[...] = jnp.full_like(m_ref, MASK_VALUE)
        l_ref[...] = jnp.zeros_like(l_ref)
        acc_ref[...] = jnp.zeros_like(acc_ref)

    def finish():
        out = acc_ref[...] / l_ref[...]
        o_ref[...] = out.T.astype(o_ref.dtype)
        return out

    reset()
    exact_block(i * t, True)
    gb = ATTN_GROUP_BLOCKS
    n_groups = i // gb
    rest = i - n_groups * gb

    def group_cond(c):
        g, live = c
        return jnp.logical_and(g < n_groups, live)

    def group_body(c):
        g, _ = c
        first = i - (g + 1) * gb
        go_on = alive(first - 1)
        lagged_group(first * t, gb * t)
        return g + 1, go_on

    _, live = lax.while_loop(group_cond, group_body, (0, alive(i - 1)))

    @pl.when(jnp.logical_and((rest & 2) != 0, live))
    def _():
        lagged_group((rest - 2) * t, 2 * t)

    @pl.when(jnp.logical_and((rest & 1) != 0, jnp.logical_and(live, alive(0))))
    def _():
        lagged_group(0, t)

    out = finish()
    overflowed = jnp.max(jnp.where(jnp.abs(out) < FINITE_LIMIT, 0.0, 1.0)) > 0.0

    @pl.when(overflowed)
    def _():
        reset()

        def exact_body(j, c):
            exact_block(j * t, False)
            return c

        lax.fori_loop(0, i, exact_body, 0)
        exact_block(i * t, True)
        finish()


def _attention(qg3, k3, aux, vt, block_end, block_knorm, n_heads):
    b, s, _ = k3.shape
    hd = k3.shape[2] // n_heads
    assert hd == LANES
    t = _tile(s, ATTN_BLOCK)
    by_head = lambda tab: tab[:, :, 0, :n_heads].transpose(0, 2, 1).reshape(b, n_heads, 1, s // t)
    end_tab = by_head(block_end)
    kmax_tab = jnp.sqrt(lax.cummax(by_head(block_knorm), axis=3))
    table = pl.BlockSpec((None, None, 1, s // t), lambda bi, h, i: (bi, h, 0, 0),
                         memory_space=pltpu.SMEM)
    return pl.pallas_call(
        _attn_kernel,
        out_shape=jax.ShapeDtypeStruct((b, s, n_heads * hd), BF16),
        grid=(b, n_heads, s // t),
        in_specs=[
            table,
            table,
            pl.BlockSpec((None, t, hd), lambda bi, h, i: (bi, i, h)),
            pl.BlockSpec((None, s, hd), lambda bi, h, i: (bi, 0, h)),
            pl.BlockSpec((None, None, s, LANES), lambda bi, h, i: (bi, h, 0, 0)),
            pl.BlockSpec((None, hd, s), lambda bi, h, i: (bi, h, 0)),
        ],
        out_specs=pl.BlockSpec((None, t, hd), lambda bi, h, i: (bi, i, h)),
        scratch_shapes=[
            pltpu.VMEM((t, hd + LANES), BF16),
            pltpu.VMEM((1, t), F32),
            pltpu.VMEM((1, t), F32),
            pltpu.VMEM((hd, t), F32),
        ],
        compiler_params=_cparams(("parallel", "parallel", "arbitrary"), 48),
        name="fox_attention",
    )(end_tab, kmax_tab, qg3, k3, aux, vt)


def _gated_out_kernel(o_ref, gate_ref, x_ref, w_ref, out_ref):
    g = o_ref[...].astype(F32) * jax.nn.sigmoid(gate_ref[...].astype(F32))
    out_ref[...] = x_ref[...] + jnp.dot(g.astype(BF16), w_ref[...], preferred_element_type=F32)


def _gated_out(o2, qg, x2, w_o):
    n, d = x2.shape
    hd_all = o2.shape[1]
    tm = _tile(n, 512)
    return pl.pallas_call(
        _gated_out_kernel,
        out_shape=jax.ShapeDtypeStruct((n, d), F32),
        grid=(n // tm,),
        in_specs=[
            pl.BlockSpec((tm, hd_all), lambda i: (i, 0)),
            pl.BlockSpec((tm, hd_all), lambda i: (i, 1)),
            pl.BlockSpec((tm, d), lambda i: (i, 0)),
            pl.BlockSpec((hd_all, d), lambda i: (0, 0)),
        ],
        out_specs=pl.BlockSpec((tm, d), lambda i: (i, 0)),
        compiler_params=_cparams(("parallel",), 48),
        name="gated_out_proj",
    )(o2, qg, x2, w_o)


def kernel(x, a_norm, a_w_in, a_conv_w, a_conv_b, a_w_rec, a_b_rec, a_w_inp, a_b_inp, a_lambda, a_w_out, kv_norm, kv_w, kv_b_forget, b_norm, b_w_qg, b_w_o, m_norm, m_w_group, m_b_group, m_w_router, m_b_router, m_w_in, m_w_out, final_norm):
    b, s, d = x.shape
    n = b * s
    depth = m_norm.shape[0]
    n_a = a_norm.shape[0]
    n_heads = kv_b_forget.shape[0]
    hd_all = b_w_o.shape[1]
    head_dim = hd_all // n_heads
    x2 = x.reshape(n, d)
    k3 = aux = vt = block_end = block_knorm = None
    for layer in range(depth):
        if layer < n_a:
            i = layer
            d_rnn = a_w_out.shape[1]
            proj = _norm_proj(x2, a_norm[i], a_w_in[i].astype(BF16), jnp.ones((2 * d_rnn,), F32),
                              BF16, "rglru_in_proj", gelu_from_col=d_rnn)
            x2 = _rglru(proj, x2.reshape(b, s, d), a_conv_w[i], a_conv_b[i], a_w_rec[i], a_b_rec[i],
                        a_w_inp[i], a_b_inp[i], a_lambda[i], a_w_out[i]).reshape(n, d)
        else:
            j = layer - n_a
            q_scale = jnp.concatenate([jnp.full((hd_all,), head_dim ** -0.5 * LOG2E, F32),
                                       jnp.ones((hd_all,), F32)])
            qg = _norm_proj(x2, b_norm[j], b_w_qg[j].astype(BF16), q_scale, BF16, "fox_qg_proj")
            o = _attention(qg.reshape(b, s, 2 * hd_all), k3, aux, vt, block_end, block_knorm,
                           n_heads)
            x2 = _gated_out(o.reshape(n, hd_all), qg, x2, b_w_o[j].astype(BF16))
        last = layer == depth - 1
        x2 = _moe(x2, m_norm[layer], m_w_group[layer], m_b_group[layer], m_w_router[layer],
                  m_b_router[layer], m_w_in, m_w_out, layer, final_norm, last)
        if layer == n_a - 1:
            k3, vt, aux, block_end, block_knorm = _shared_kv(
                x2.reshape(b, s, d), kv_norm, kv_w[:, :hd_all].astype(BF16),
                kv_w[:, hd_all:2 * hd_all].T.astype(BF16), kv_w[:, 2 * hd_all:], kv_b_forget)
    if depth == 0:
        x2 = _rms(x2, final_norm)
    return x2.reshape(b, s, d)
```
